```python
import jax, jax.numpy as jnp
from jax import lax
import numpy as np

D_MODEL = 1024
BATCH = 2
SEQ = 8192
DEPTH = 2

EPS = 1e-6
NEG_INF = -1e30
CHUNK = 128
A_GROUPS = 4
A_WIDTH = D_MODEL
A_GROUP_DIM = A_WIDTH // A_GROUPS
B_WIDTH = D_MODEL
POOL_WINDOWS = (2, 4, 8, 16)
B_GROUPS = len(POOL_WINDOWS)
B_GROUP_DIM = B_WIDTH // B_GROUPS
MIX0_IN = 3 * A_WIDTH + 2 * B_WIDTH
MIX0_OUT = A_WIDTH + B_WIDTH
N_HEADS = 16
N_KV_HEADS = 4
HEAD_DIM = 64
GQA_GROUP = N_HEADS // N_KV_HEADS
C_WIDTH = N_HEADS * HEAD_DIM
KV_WIDTH = N_KV_HEADS * HEAD_DIM
MIX1_IN = 2 * C_WIDTH + 2 * KV_WIDTH
WINDOW = 128
ATTN_BLOCK = 128
ROPE_THETA = 500000.0
ROT_DIM = HEAD_DIM // 4

kernel_name = "hybrid_gmlp_pool_swa_encoder"


def rms_norm(x, g):
    xf = x.astype(jnp.float32)
    y = xf * lax.rsqrt(jnp.mean(xf * xf, axis=-1, keepdims=True) + EPS)
    return (y * g.astype(jnp.float32)).astype(x.dtype)


def spatial_gating_mixer(u, v, w_s, b_s, g_v):
    bn, s, _ = u.shape
    u = jax.nn.gelu(u)
    v = rms_norm(jax.nn.gelu(v), g_v)
    v = v.reshape(bn, s // CHUNK, CHUNK, A_GROUPS, A_GROUP_DIM)
    mixed = jnp.einsum('hij,bcjhd->bcihd', w_s, v) + b_s.T[None, None, :, :, None]
    return u * mixed.reshape(bn, s, A_WIDTH)


def multiscale_pool_mixer(xb, w_g, scale):
    bn, s, _ = xb.shape
    xf = xb.astype(jnp.float32)
    cs = jnp.concatenate([jnp.zeros((bn, 1, B_WIDTH), jnp.float32), lax.cumsum(xf, axis=1)], axis=1)
    t = jnp.arange(s)
    outs = []
    for gi, w in enumerate(POOL_WINDOWS):
        lo = jnp.clip(t - w // 2, 0, s)
        hi = jnp.clip(t + w // 2, 0, s)
        sl = slice(gi * B_GROUP_DIM, (gi + 1) * B_GROUP_DIM)
        csg = cs[:, :, sl]
        win_sum = jnp.take(csg, hi, axis=1) - jnp.take(csg, lo, axis=1)
        mean = win_sum / (hi - lo).astype(jnp.float32)[None, :, None]
        outs.append(mean - xf[:, :, sl])
    p = jnp.stack(outs, axis=2).astype(xb.dtype)
    y = jnp.einsum('bsgd,gde->bsge', p, w_g).reshape(bn, s, B_WIDTH)
    return y * scale


def partial_rope(x, pos):
    inv = ROPE_THETA ** (-jnp.arange(0, ROT_DIM, 2, dtype=jnp.float32) / ROT_DIM)
    ang = pos.astype(jnp.float32)[:, None] * inv[None, :]
    cos = jnp.cos(ang)[None, :, None, :]
    sin = jnp.sin(ang)[None, :, None, :]
    xr = x[..., :ROT_DIM].astype(jnp.float32)
    x1, x2 = xr[..., :ROT_DIM // 2], xr[..., ROT_DIM // 2:]
    rot = jnp.concatenate([x1 * cos - x2 * sin, x2 * cos + x1 * sin], axis=-1)
    return jnp.concatenate([rot.astype(x.dtype), x[..., ROT_DIM:]], axis=-1)


def windowed_gqa(q, k, v, sink):
    bn, s = q.shape[:2]
    nb = s // ATTN_BLOCK
    qb = q.reshape(bn, nb, ATTN_BLOCK, N_KV_HEADS, GQA_GROUP, HEAD_DIM)

    def band(t):
        tp = jnp.pad(t, ((0, 0), (ATTN_BLOCK, ATTN_BLOCK), (0, 0), (0, 0)))
        tp = tp.reshape(bn, nb + 2, ATTN_BLOCK, N_KV_HEADS, HEAD_DIM)
        return jnp.concatenate([tp[:, :nb], tp[:, 1:nb + 1], tp[:, 2:]], axis=2)

    kb, vb = band(k), band(v)
    scores = jnp.einsum('bnqkgd,bnskd->bnkgqs', qb, kb).astype(jnp.float32) * (HEAD_DIM ** -0.5)
    qi = jnp.arange(nb)[:, None, None] * ATTN_BLOCK + jnp.arange(ATTN_BLOCK)[None, :, None]
    kj = (jnp.arange(nb)[:, None, None] - 1) * ATTN_BLOCK + jnp.arange(3 * ATTN_BLOCK)[None, None, :]
    allowed = (jnp.abs(qi - kj) <= WINDOW) & (kj >= 0) & (kj < s)
    scores = jnp.where(allowed[None, :, None, None], scores, NEG_INF)
    sink_col = jnp.broadcast_to(sink.astype(jnp.float32).reshape(1, 1, N_KV_HEADS, GQA_GROUP, 1, 1),
                                scores.shape[:-1] + (1,))
    probs = jax.nn.softmax(jnp.concatenate([scores, sink_col], axis=-1), axis=-1)[..., :-1]
    out = jnp.einsum('bnkgqs,bnskd->bnqkgd', probs.astype(v.dtype), vb)
    return out.reshape(bn, s, C_WIDTH)


def setup_inputs(seed: int = 0) -> dict:
    key = jax.random.key(seed)
    ks = jax.random.split(key, 16)
    f32 = jnp.float32
    nrm = lambda k, shp, sc: jax.random.normal(k, shp, f32) * sc
    return {
        "x": nrm(ks[0], (BATCH, SEQ, D_MODEL), 1.0),
        "norm_0": 1.0 + nrm(ks[1], (D_MODEL,), 0.05),
        "w_in_0": nrm(ks[2], (D_MODEL, MIX0_IN), D_MODEL ** -0.5),
        "a_v_norm_0": 1.0 + nrm(ks[3], (A_WIDTH,), 0.05),
        "a_spatial_w_0": nrm(ks[4], (A_GROUPS, CHUNK, CHUNK), CHUNK ** -0.5),
        "a_spatial_b_0": 1.0 + nrm(ks[5], (A_GROUPS, CHUNK), 0.1),
        "b_group_w_0": nrm(ks[6], (B_GROUPS, B_GROUP_DIM, B_GROUP_DIM), B_GROUP_DIM ** -0.5),
        "b_scale_0": 1.0 + nrm(ks[7], (B_WIDTH,), 0.05),
        "w_out_0": nrm(ks[8], (MIX0_OUT, D_MODEL), MIX0_OUT ** -0.5),
        "norm_1": 1.0 + nrm(ks[9], (D_MODEL,), 0.05),
        "w_in_1": nrm(ks[10], (D_MODEL, MIX1_IN), D_MODEL ** -0.5),
        "sink_1": nrm(ks[11], (N_HEADS,), 0.5),
        "w_out_1": nrm(ks[12], (C_WIDTH, D_MODEL), C_WIDTH ** -0.5),
        "final_norm": 1.0 + nrm(ks[13], (D_MODEL,), 0.05),
    }


def reference(x, norm_0, w_in_0, a_v_norm_0, a_spatial_w_0, a_spatial_b_0, b_group_w_0, b_scale_0,
              w_out_0, norm_1, w_in_1, sink_1, w_out_1, final_norm):
    bn, s, _ = x.shape
    pos = jnp.arange(s)
    for layer in range(DEPTH):
        if layer % 2 == 0:
            h = rms_norm(x, norm_0)
            z = h @ w_in_0
            a_u, a_v, a_gate, b_x, b_gate = jnp.split(
                z, [A_WIDTH, 2 * A_WIDTH, 3 * A_WIDTH, 3 * A_WIDTH + B_WIDTH], axis=-1)
            ya = spatial_gating_mixer(a_u, a_v, a_spatial_w_0, a_spatial_b_0, a_v_norm_0) * jax.nn.silu(a_gate)
            yb = multiscale_pool_mixer(b_x, b_group_w_0, b_scale_0) * jax.nn.silu(b_gate)
            x = x + jnp.concatenate([ya, yb], axis=-1) @ w_out_0
        else:
            h = rms_norm(x, norm_1)
            z = h @ w_in_1
            q, k, v, gate = jnp.split(z, [C_WIDTH, C_WIDTH + KV_WIDTH, C_WIDTH + 2 * KV_WIDTH], axis=-1)
            q = partial_rope(q.reshape(bn, s, N_HEADS, HEAD_DIM), pos)
            k = partial_rope(k.reshape(bn, s, N_KV_HEADS, HEAD_DIM), pos)
            v = v.reshape(bn, s, N_KV_HEADS, HEAD_DIM)
            y = windowed_gqa(q, k, v, sink_1) * jax.nn.silu(gate)
            x = x + y @ w_out_1
    return rms_norm(x, final_norm)
```

```python
import functools

import jax
import jax.numpy as jnp
from jax import lax
from jax.experimental import pallas as pl
from jax.experimental.pallas import tpu as pltpu

F32 = jnp.float32
BF16 = jnp.bfloat16

D_MODEL = 1024
EPS = 1e-6
NEG_INF = -1e30
CHUNK = 128
A_GROUPS = 4
A_WIDTH = D_MODEL
A_GROUP_DIM = A_WIDTH // A_GROUPS
B_WIDTH = D_MODEL
POOL_WINDOWS = (2, 4, 8, 16)
B_GROUP_DIM = B_WIDTH // len(POOL_WINDOWS)
POOL_HALO = max(POOL_WINDOWS) // 2
N_HEADS = 16
N_KV_HEADS = 4
HEAD_DIM = 64
C_WIDTH = N_HEADS * HEAD_DIM
KV_WIDTH = N_KV_HEADS * HEAD_DIM
WINDOW = 128
ATTN_BLOCK = 128
ROPE_THETA = 500000.0
ROT_DIM = HEAD_DIM // 4
LANES = 128
HEAD_PAIRS = N_HEADS // 2
KV_PAD_WIDTH = N_KV_HEADS * LANES

TILE_L0 = 256
TILE_QKV = 512
TILE_ATTN = 512
VMEM_LIMIT = 56 * 1024 * 1024


def _rms(x, g):
    return x * lax.rsqrt(jnp.mean(x * x, axis=-1, keepdims=True) + EPS) * g


def _silu(x):
    return x / (1.0 + jnp.exp(-x))


def _dot(a, b):
    return jnp.dot(a, b, preferred_element_type=F32)


def _dot_nt(a, b):
    return lax.dot_general(a, b, (((1,), (1,)), ((), ())), preferred_element_type=F32)


def _layer0_kernel(x_ref, xp_ref, xn_ref, n0_ref, win_ref, gv_ref, ws_ref, bsb_ref, wg_ref,
                   sc_ref, wout_ref, o_ref, hext_ref, bx_ref, v_ref, cat_ref, *, tile, seq):
    j = pl.program_id(1)
    nj = pl.num_programs(1)
    n0 = n0_ref[...]
    x = x_ref[0]
    h = _rms(x, n0).astype(BF16)
    hext_ref[0:tile, :] = h
    hp = jnp.where(j > 0, _rms(xp_ref[0], n0), 0.0)
    hn = jnp.where(j < nj - 1, _rms(xn_ref[0], n0), 0.0)
    hext_ref[tile:tile + 2 * POOL_HALO, :] = jnp.concatenate([hp, hn], axis=0).astype(BF16)

    def col(k):
        return k * D_MODEL

    a_v = jax.nn.gelu(_dot(h, win_ref[:, col(1):col(2)]))
    v_ref[...] = _rms(a_v, gv_ref[...]).astype(BF16)
    for g in range(A_GROUPS):
        c0 = g * A_GROUP_DIM
        c1 = c0 + A_GROUP_DIM
        u = jax.nn.gelu(_dot(h, win_ref[:, col(0) + c0:col(0) + c1]))
        gate = _silu(_dot(h, win_ref[:, col(2) + c0:col(2) + c1]))
        for c in range(tile // CHUNK):
            r0 = c * CHUNK
            r1 = r0 + CHUNK
            mixed = _dot(ws_ref[g], v_ref[r0:r1, c0:c1]) + bsb_ref[:, c0:c1]
            cat_ref[r0:r1, c0:c1] = (u[r0:r1] * mixed * gate[r0:r1]).astype(BF16)

    bxe = _dot(hext_ref[...], win_ref[:, col(3):col(4)])
    bx_ref[POOL_HALO:POOL_HALO + tile, :] = bxe[0:tile]
    bx_ref[0:POOL_HALO, :] = bxe[tile:tile + POOL_HALO]
    bx_ref[POOL_HALO + tile:, :] = bxe[tile + POOL_HALO:]
    t = j * tile + lax.broadcasted_iota(jnp.int32, (tile, 1), 0)
    for gi, w in enumerate(POOL_WINDOWS):
        c0 = gi * B_GROUP_DIM
        c1 = c0 + B_GROUP_DIM
        half = w // 2
        win_sum = bx_ref[POOL_HALO - half:POOL_HALO - half + tile, c0:c1]
        for d in range(-half + 1, half):
            win_sum = win_sum + bx_ref[POOL_HALO + d:POOL_HALO + d + tile, c0:c1]
        cnt = (jnp.minimum(t + half, seq) - jnp.maximum(t - half, 0)).astype(F32)
        p = (win_sum / cnt - bx_ref[POOL_HALO:POOL_HALO + tile, c0:c1]).astype(BF16)
        gate = _silu(_dot(h, win_ref[:, col(4) + c0:col(4) + c1]))
        yb = _dot(p, wg_ref[gi]) * sc_ref[:, c0:c1] * gate
        cat_ref[:, A_WIDTH + c0:A_WIDTH + c1] = yb.astype(BF16)

    o_ref[0] = x + _dot(cat_ref[...], wout_ref[...])


def _layer0(x, norm_0, w_in_0, a_v_norm_0, a_spatial_w_0, a_spatial_b_0, b_group_w_0, b_scale_0,
            w_out_0):
    bn, seq, d = x.shape
    tile = TILE_L0
    nt = seq // tile
    halo_blocks = seq // POOL_HALO
    win = w_in_0.astype(BF16)
    wout = w_out_0.astype(BF16)
    ws = a_spatial_w_0.astype(BF16)
    wg = b_group_w_0.astype(BF16)
    bsb = jnp.repeat(a_spatial_b_0.T, A_GROUP_DIM, axis=1)

    def const(shape):
        return pl.BlockSpec(shape, lambda b, j: (0,) * len(shape), pipeline_mode=pl.Buffered(1))

    per_halo = tile // POOL_HALO
    kern = functools.partial(_layer0_kernel, tile=tile, seq=seq)
    return pl.pallas_call(
        kern,
        grid=(bn, nt),
        in_specs=[
            pl.BlockSpec((1, tile, d), lambda b, j: (b, j, 0)),
            pl.BlockSpec((1, POOL_HALO, d), lambda b, j: (b, jnp.maximum(j * per_halo - 1, 0), 0)),
            pl.BlockSpec((1, POOL_HALO, d),
                         lambda b, j: (b, jnp.minimum((j + 1) * per_halo, halo_blocks - 1), 0)),
            const((1, d)),
            const(win.shape),
            const((1, A_WIDTH)),
            const(ws.shape),
            const(bsb.shape),
            const(wg.shape),
            const((1, B_WIDTH)),
            const(wout.shape),
        ],
        out_specs=pl.BlockSpec((1, tile, d), lambda b, j: (b, j, 0)),
        out_shape=jax.ShapeDtypeStruct(x.shape, F32),
        scratch_shapes=[
            pltpu.VMEM((tile + 2 * POOL_HALO, d), BF16),
            pltpu.VMEM((tile + 2 * POOL_HALO, B_WIDTH), F32),
            pltpu.VMEM((tile, A_WIDTH), BF16),
            pltpu.VMEM((tile, A_WIDTH + B_WIDTH), BF16),
        ],
        compiler_params=pltpu.CompilerParams(
            dimension_semantics=("arbitrary", "arbitrary"), vmem_limit_bytes=VMEM_LIMIT),
        name="layer0_mixer",
    )(x, x, x, norm_0.reshape(1, d), win, a_v_norm_0.reshape(1, A_WIDTH), ws, bsb, wg,
      b_scale_0.reshape(1, B_WIDTH), wout)


def _rope(xb, cos, sin_hi, sin_lo):
    return (xb * cos + pltpu.roll(xb, ROT_DIM // 2, 1) * sin_hi
            + pltpu.roll(xb, LANES - ROT_DIM // 2, 1) * sin_lo)


def _qkv_kernel(x_ref, n1_ref, w_ref, cos_ref, shi_ref, slo_ref,
                q_ref, ka_ref, kb_ref, va_ref, vb_ref, *, tile):
    h = _rms(x_ref[0], n1_ref[...]).astype(BF16)
    cos = cos_ref[...]
    shi = shi_ref[...]
    slo = slo_ref[...]
    low = lax.broadcasted_iota(jnp.int32, (tile, LANES), 1) < HEAD_DIM
    scale = HEAD_DIM ** -0.5
    for cb in range(C_WIDTH // LANES):
        qb = _dot(h, w_ref[:, cb * LANES:(cb + 1) * LANES])
        q_ref[0, :, cb * LANES:(cb + 1) * LANES] = (_rope(qb, cos, shi, slo) * scale).astype(BF16)

    def split(xb, a_ref, b_ref, cb):
        sw = pltpu.roll(xb, HEAD_DIM, 1)
        e = 2 * cb
        o = 2 * cb + 1
        a_ref[0, :, e * LANES:(e + 1) * LANES] = jnp.where(low, xb, 0.0).astype(BF16)
        b_ref[0, :, e * LANES:(e + 1) * LANES] = jnp.where(low, 0.0, sw).astype(BF16)
        a_ref[0, :, o * LANES:(o + 1) * LANES] = jnp.where(low, sw, 0.0).astype(BF16)
        b_ref[0, :, o * LANES:(o + 1) * LANES] = jnp.where(low, 0.0, xb).astype(BF16)

    for cb in range(KV_WIDTH // LANES):
        k0 = C_WIDTH + cb * LANES
        kb = _rope(_dot(h, w_ref[:, k0:k0 + LANES]), cos, shi, slo)
        split(kb, ka_ref, kb_ref, cb)
        v0 = C_WIDTH + KV_WIDTH + cb * LANES
        split(_dot(h, w_ref[:, v0:v0 + LANES]), va_ref, vb_ref, cb)


def _rope_tables(seq):
    inv = ROPE_THETA ** (-jnp.arange(0, ROT_DIM, 2, dtype=F32) / ROT_DIM)
    ang = jnp.arange(seq).astype(F32)[:, None] * inv[None, :]
    cos = jnp.cos(ang)
    sin = jnp.sin(ang)
    half = ROT_DIM // 2
    rest = HEAD_DIM - ROT_DIM
    ones = jnp.ones((seq, rest), F32)
    zeros = jnp.zeros((seq, rest), F32)
    zh = jnp.zeros((seq, half), F32)
    cos_t = jnp.concatenate([cos, cos, ones], axis=1)
    shi_t = jnp.concatenate([zh, sin, zeros], axis=1)
    slo_t = jnp.concatenate([-sin, zh, zeros], axis=1)
    rep = LANES // HEAD_DIM
    return (jnp.tile(cos_t, (1, rep)), jnp.tile(shi_t, (1, rep)), jnp.tile(slo_t, (1, rep)))


def _qkv(x, norm_1, w_qkv):
    bn, seq, d = x.shape
    tile = TILE_QKV
    nt = seq // tile
    cos_t, shi_t, slo_t = _rope_tables(seq)
    kern = functools.partial(_qkv_kernel, tile=tile)
    tab = pl.BlockSpec((tile, LANES), lambda b, j: (j, 0))
    kv_spec = pl.BlockSpec((1, tile, KV_PAD_WIDTH), lambda b, j: (b, j, 0))
    kv_shape = jax.ShapeDtypeStruct((bn, seq, KV_PAD_WIDTH), BF16)
    return pl.pallas_call(
        kern,
        grid=(bn, nt),
        in_specs=[
            pl.BlockSpec((1, tile, d), lambda b, j: (b, j, 0)),
            pl.BlockSpec((1, d), lambda b, j: (0, 0), pipeline_mode=pl.Buffered(1)),
            pl.BlockSpec(w_qkv.shape, lambda b, j: (0, 0), pipeline_mode=pl.Buffered(1)),
            tab, tab, tab,
        ],
        out_specs=[pl.BlockSpec((1, tile, C_WIDTH), lambda b, j: (b, j, 0)),
                   kv_spec, kv_spec, kv_spec, kv_spec],
        out_shape=[jax.ShapeDtypeStruct((bn, seq, C_WIDTH), BF16),
                   kv_shape, kv_shape, kv_shape, kv_shape],
        compiler_params=pltpu.CompilerParams(
            dimension_semantics=("arbitrary", "arbitrary"), vmem_limit_bytes=VMEM_LIMIT),
        name="layer1_qkv",
    )(x, norm_1.reshape(1, d), w_qkv, cos_t, shi_t, slo_t)


def _attn_kernel(sink_ref, x_ref, q_ref,
                 kap_ref, kat_ref, kan_ref, kbp_ref, kbt_ref, kbn_ref,
                 vap_ref, vat_ref, van_ref, vbp_ref, vbt_ref, vbn_ref,
                 n1_ref, wgate_ref, wout_ref, fn_ref, o_ref,
                 kae_ref, kbe_ref, vae_ref, vbe_ref, sg_ref, y_ref, *, tile, seq):
    j = pl.program_id(1)
    nblk = tile // ATTN_BLOCK
    band = 3 * ATTN_BLOCK
    x = x_ref[0]
    h = _rms(x, n1_ref[...]).astype(BF16)
    sg_ref[...] = _silu(_dot(h, wgate_ref[...]))

    for ext, prv, cur, nxt in ((kae_ref, kap_ref, kat_ref, kan_ref),
                               (kbe_ref, kbp_ref, kbt_ref, kbn_ref),
                               (vae_ref, vap_ref, vat_ref, van_ref),
                               (vbe_ref, vbp_ref, vbt_ref, vbn_ref)):
        ext[0:ATTN_BLOCK, :] = prv[0]
        ext[ATTN_BLOCK:ATTN_BLOCK + tile, :] = cur[0]
        ext[ATTN_BLOCK + tile:, :] = nxt[0]

    row = lax.broadcasted_iota(jnp.int32, (ATTN_BLOCK, band), 0)
    colr = lax.broadcasted_iota(jnp.int32, (ATTN_BLOCK, band), 1) - ATTN_BLOCK
    in_band = jnp.abs(colr - row) <= WINDOW
    lane_low = lax.broadcasted_iota(jnp.int32, (ATTN_BLOCK, LANES), 1) < HEAD_DIM

    def softmax_unnorm(s, allowed, sink):
        s = jnp.where(allowed, s, NEG_INF)
        m = jnp.maximum(jnp.max(s, axis=-1, keepdims=True), sink)
        p = jnp.exp(s - m)
        l = jnp.sum(p, axis=-1, keepdims=True) + jnp.exp(sink - m)
        return p.astype(BF16), l

    def qblock(i, carry):
        r0 = pl.multiple_of(i * ATTN_BLOCK, ATTN_BLOCK)
        kj = (j * nblk + i) * ATTN_BLOCK + colr
        allowed = in_band & (kj >= 0) & (kj < seq)
        for kh in range(N_KV_HEADS):
            cs = slice(kh * LANES, (kh + 1) * LANES)
            ka = kae_ref[pl.ds(r0, band), cs]
            kb = kbe_ref[pl.ds(r0, band), cs]
            va = vae_ref[pl.ds(r0, band), cs]
            vb = vbe_ref[pl.ds(r0, band), cs]
            for pp in range(2):
                pair = 2 * kh + pp
                ps = slice(pair * LANES, (pair + 1) * LANES)
                qp = q_ref[0, pl.ds(r0, ATTN_BLOCK), ps]
                pa, la = softmax_unnorm(_dot_nt(qp, ka), allowed, sink_ref[2 * pair])
                pb, lb = softmax_unnorm(_dot_nt(qp, kb), allowed, sink_ref[2 * pair + 1])
                o = _dot(pa, va) + _dot(pb, vb)
                inv = jnp.where(lane_low, 1.0 / la, 1.0 / lb)
                y = o * inv * sg_ref[pl.ds(r0, ATTN_BLOCK), ps]
                y_ref[pl.ds(r0, ATTN_BLOCK), ps] = y.astype(BF16)
        return carry

    lax.fori_loop(0, nblk, qblock, 0)
    out = x + _dot(y_ref[...], wout_ref[...])
    o_ref[0] = _rms(out, fn_ref[...])


def _attention(x, q, ka, kb, va, vb, norm_1, w_gate, sink_1, w_out_1, final_norm):
    bn, seq, d = x.shape
    tile = TILE_ATTN
    nt = seq // tile
    per = tile // ATTN_BLOCK
    nb = seq // ATTN_BLOCK

    def const(shape):
        return pl.BlockSpec(shape, lambda b, j: (0,) * len(shape), pipeline_mode=pl.Buffered(1))

    prev = pl.BlockSpec((1, ATTN_BLOCK, KV_PAD_WIDTH),
                        lambda b, j: (b, jnp.maximum(j * per - 1, 0), 0))
    cur = pl.BlockSpec((1, tile, KV_PAD_WIDTH), lambda b, j: (b, j, 0))
    nxt = pl.BlockSpec((1, ATTN_BLOCK, KV_PAD_WIDTH),
                       lambda b, j: (b, jnp.minimum((j + 1) * per, nb - 1), 0))
    kern = functools.partial(_attn_kernel, tile=tile, seq=seq)
    ext = pltpu.VMEM((tile + 2 * ATTN_BLOCK, KV_PAD_WIDTH), BF16)
    return pl.pallas_call(
        kern,
        grid=(bn, nt),
        in_specs=[
            pl.BlockSpec(memory_space=pltpu.SMEM),
            pl.BlockSpec((1, tile, d), lambda b, j: (b, j, 0)),
            pl.BlockSpec((1, tile, C_WIDTH), lambda b, j: (b, j, 0)),
            prev, cur, nxt, prev, cur, nxt, prev, cur, nxt, prev, cur, nxt,
            const((1, d)),
            const(w_gate.shape),
            const(w_out_1.shape),
            const((1, d)),
        ],
        out_specs=pl.BlockSpec((1, tile, d), lambda b, j: (b, j, 0)),
        out_shape=jax.ShapeDtypeStruct(x.shape, F32),
        scratch_shapes=[ext, ext, ext, ext,
                        pltpu.VMEM((tile, C_WIDTH), F32),
                        pltpu.VMEM((tile, C_WIDTH), BF16)],
        compiler_params=pltpu.CompilerParams(
            dimension_semantics=("arbitrary", "arbitrary"), vmem_limit_bytes=VMEM_LIMIT),
        name="layer1_attention",
    )(sink_1, x, q, ka, ka, ka, kb, kb, kb, va, va, va, vb, vb, vb,
      norm_1.reshape(1, d), w_gate, w_out_1, final_norm.reshape(1, d))


def kernel(x, norm_0, w_in_0, a_v_norm_0, a_spatial_w_0, a_spatial_b_0, b_group_w_0, b_scale_0,
           w_out_0, norm_1, w_in_1, sink_1, w_out_1, final_norm):
    x1 = _layer0(x, norm_0, w_in_0, a_v_norm_0, a_spatial_w_0, a_spatial_b_0, b_group_w_0,
                 b_scale_0, w_out_0)
    qkv_cols = C_WIDTH + 2 * KV_WIDTH
    w_qkv = w_in_1[:, :qkv_cols].astype(BF16)
    w_gate = w_in_1[:, qkv_cols:].astype(BF16)
    q, ka, kb, va, vb = _qkv(x1, norm_1, w_qkv)
    return _attention(x1, q, ka, kb, va, vb, norm_1, w_gate, sink_1, w_out_1.astype(BF16),
                      final_norm)
```

```python
import functools

import jax
import jax.numpy as jnp
from jax import lax
from jax.experimental import pallas as pl
from jax.experimental.pallas import tpu as pltpu

F32 = jnp.float32
BF16 = jnp.bfloat16

D_MODEL = 1024
EPS = 1e-6
NEG_INF = -1e30
CHUNK = 128
A_GROUPS = 4
A_WIDTH = D_MODEL
A_GROUP_DIM = A_WIDTH // A_GROUPS
B_WIDTH = D_MODEL
POOL_WINDOWS = (2, 4, 8, 16)
B_GROUP_DIM = B_WIDTH // len(POOL_WINDOWS)
POOL_HALO = max(POOL_WINDOWS) // 2
N_HEADS = 16
N_KV_HEADS = 4
HEAD_DIM = 64
GQA_GROUP = N_HEADS // N_KV_HEADS
C_WIDTH = N_HEADS * HEAD_DIM
KV_WIDTH = N_KV_HEADS * HEAD_DIM
WINDOW = 128
ATTN_BLOCK = 128
BAND = 3 * ATTN_BLOCK
ROPE_THETA = 500000.0
ROT_DIM = HEAD_DIM // 4
LANES = 128
KV_PAD_WIDTH = N_KV_HEADS * LANES
GROUP_Q = GQA_GROUP * ATTN_BLOCK

TILE_L0 = 256
TILE_QKV = 512
TILE_ATTN = 512
VMEM_LIMIT = 56 * 1024 * 1024


def _rms(x, g):
    return x * lax.rsqrt(jnp.mean(x * x, axis=-1, keepdims=True) + EPS) * g


def _silu(x):
    return x / (1.0 + jnp.exp(-x))


def _dot(a, b):
    return jnp.dot(a, b, preferred_element_type=F32)


def _dot_nt(a, b):
    return lax.dot_general(a, b, (((1,), (1,)), ((), ())), preferred_element_type=F32)


def _layer0_kernel(x_ref, xp_ref, xn_ref, n0_ref, win_ref, gv_ref, ws_ref, bsb_ref, wg_ref,
                   sc_ref, wout_ref, o_ref, hext_ref, bx_ref, v_ref, cat_ref, *, tile, seq):
    j = pl.program_id(1)
    nj = pl.num_programs(1)
    n0 = n0_ref[...]
    x = x_ref[0]
    h = _rms(x, n0).astype(BF16)
    hext_ref[0:tile, :] = h
    hp = jnp.where(j > 0, _rms(xp_ref[0], n0), 0.0)
    hn = jnp.where(j < nj - 1, _rms(xn_ref[0], n0), 0.0)
    hext_ref[tile:tile + 2 * POOL_HALO, :] = jnp.concatenate([hp, hn], axis=0).astype(BF16)

    def col(k):
        return k * D_MODEL

    a_v = jax.nn.gelu(_dot(h, win_ref[:, col(1):col(2)]))
    v_ref[...] = _rms(a_v, gv_ref[...]).astype(BF16)
    for g in range(A_GROUPS):
        c0 = g * A_GROUP_DIM
        c1 = c0 + A_GROUP_DIM
        u = jax.nn.gelu(_dot(h, win_ref[:, col(0) + c0:col(0) + c1]))
        gate = _silu(_dot(h, win_ref[:, col(2) + c0:col(2) + c1]))
        for c in range(tile // CHUNK):
            r0 = c * CHUNK
            r1 = r0 + CHUNK
            mixed = _dot(ws_ref[g], v_ref[r0:r1, c0:c1]) + bsb_ref[:, c0:c1]
            cat_ref[r0:r1, c0:c1] = (u[r0:r1] * mixed * gate[r0:r1]).astype(BF16)

    bxe = _dot(hext_ref[...], win_ref[:, col(3):col(4)])
    bx_ref[POOL_HALO:POOL_HALO + tile, :] = bxe[0:tile]
    bx_ref[0:POOL_HALO, :] = bxe[tile:tile + POOL_HALO]
    bx_ref[POOL_HALO + tile:, :] = bxe[tile + POOL_HALO:]
    t = j * tile + lax.broadcasted_iota(jnp.int32, (tile, 1), 0)
    for gi, w in enumerate(POOL_WINDOWS):
        c0 = gi * B_GROUP_DIM
        c1 = c0 + B_GROUP_DIM
        half = w // 2
        win_sum = bx_ref[POOL_HALO - half:POOL_HALO - half + tile, c0:c1]
        for d in range(-half + 1, half):
            win_sum = win_sum + bx_ref[POOL_HALO + d:POOL_HALO + d + tile, c0:c1]
        cnt = (jnp.minimum(t + half, seq) - jnp.maximum(t - half, 0)).astype(F32)
        p = (win_sum / cnt - bx_ref[POOL_HALO:POOL_HALO + tile, c0:c1]).astype(BF16)
        gate = _silu(_dot(h, win_ref[:, col(4) + c0:col(4) + c1]))
        yb = _dot(p, wg_ref[gi]) * sc_ref[:, c0:c1] * gate
        cat_ref[:, A_WIDTH + c0:A_WIDTH + c1] = yb.astype(BF16)

    o_ref[0] = x + _dot(cat_ref[...], wout_ref[...])


def _layer0(x, norm_0, w_in_0, a_v_norm_0, a_spatial_w_0, a_spatial_b_0, b_group_w_0, b_scale_0,
            w_out_0):
    bn, seq, d = x.shape
    tile = TILE_L0
    nt = seq // tile
    halo_blocks = seq // POOL_HALO
    win = w_in_0.astype(BF16)
    wout = w_out_0.astype(BF16)
    ws = a_spatial_w_0.astype(BF16)
    wg = b_group_w_0.astype(BF16)
    bsb = jnp.repeat(a_spatial_b_0.T, A_GROUP_DIM, axis=1)

    def const(shape):
        return pl.BlockSpec(shape, lambda b, j: (0,) * len(shape), pipeline_mode=pl.Buffered(1))

    per_halo = tile // POOL_HALO
    kern = functools.partial(_layer0_kernel, tile=tile, seq=seq)
    return pl.pallas_call(
        kern,
        grid=(bn, nt),
        in_specs=[
            pl.BlockSpec((1, tile, d), lambda b, j: (b, j, 0)),
            pl.BlockSpec((1, POOL_HALO, d), lambda b, j: (b, jnp.maximum(j * per_halo - 1, 0), 0)),
            pl.BlockSpec((1, POOL_HALO, d),
                         lambda b, j: (b, jnp.minimum((j + 1) * per_halo, halo_blocks - 1), 0)),
            const((1, d)),
            const(win.shape),
            const((1, A_WIDTH)),
            const(ws.shape),
            const(bsb.shape),
            const(wg.shape),
            const((1, B_WIDTH)),
            const(wout.shape),
        ],
        out_specs=pl.BlockSpec((1, tile, d), lambda b, j: (b, j, 0)),
        out_shape=jax.ShapeDtypeStruct(x.shape, F32),
        scratch_shapes=[
            pltpu.VMEM((tile + 2 * POOL_HALO, d), BF16),
            pltpu.VMEM((tile + 2 * POOL_HALO, B_WIDTH), F32),
            pltpu.VMEM((tile, A_WIDTH), BF16),
            pltpu.VMEM((tile, A_WIDTH + B_WIDTH), BF16),
        ],
        compiler_params=pltpu.CompilerParams(
            dimension_semantics=("arbitrary", "arbitrary"), vmem_limit_bytes=VMEM_LIMIT),
        name="layer0_mixer",
    )(x, x, x, norm_0.reshape(1, d), win, a_v_norm_0.reshape(1, A_WIDTH), ws, bsb, wg,
      b_scale_0.reshape(1, B_WIDTH), wout)


def _rope_lanes(xb, cos, sin_hi, sin_lo):
    return (xb * cos + pltpu.roll(xb, ROT_DIM // 2, 1) * sin_hi
            + pltpu.roll(xb, LANES - ROT_DIM // 2, 1) * sin_lo)


def _qkv_kernel(x_ref, n1_ref, wqv_ref, wk_ref, cos_ref, shi_ref, slo_ref, cost_ref, sint_ref,
                qt_ref, ka_ref, vt_ref, *, tile):
    h = _rms(x_ref[0], n1_ref[...]).astype(BF16)
    half = ROT_DIM // 2
    scale = HEAD_DIM ** -0.5

    qv = _dot_nt(wqv_ref[...], h)
    ct = cost_ref[...]
    st = sint_ref[...]
    for hd in range(N_HEADS):
        r0 = hd * HEAD_DIM
        x1 = qv[r0:r0 + half]
        x2 = qv[r0 + half:r0 + ROT_DIM]
        rot = jnp.concatenate([x1 * ct - x2 * st, x2 * ct + x1 * st], axis=0)
        qt_ref[0, r0:r0 + ROT_DIM, :] = (rot * scale).astype(BF16)
        qt_ref[0, r0 + ROT_DIM:r0 + HEAD_DIM, :] = (
            qv[r0 + ROT_DIM:r0 + HEAD_DIM] * scale).astype(BF16)
    vt_ref[0] = qv[C_WIDTH:].astype(BF16)

    cos = cos_ref[...]
    shi = shi_ref[...]
    slo = slo_ref[...]
    low = lax.broadcasted_iota(jnp.int32, (tile, LANES), 1) < HEAD_DIM
    for cb in range(KV_WIDTH // LANES):
        kb = _rope_lanes(_dot(h, wk_ref[:, cb * LANES:(cb + 1) * LANES]), cos, shi, slo)
        sw = pltpu.roll(kb, HEAD_DIM, 1)
        e = 2 * cb
        o = 2 * cb + 1
        ka_ref[0, :, e * LANES:(e + 1) * LANES] = jnp.where(low, kb, 0.0).astype(BF16)
        ka_ref[0, :, o * LANES:(o + 1) * LANES] = jnp.where(low, sw, 0.0).astype(BF16)


def _rope_tables(seq):
    inv = ROPE_THETA ** (-jnp.arange(0, ROT_DIM, 2, dtype=F32) / ROT_DIM)
    ang = jnp.arange(seq).astype(F32)[:, None] * inv[None, :]
    cos = jnp.cos(ang)
    sin = jnp.sin(ang)
    half = ROT_DIM // 2
    rest = HEAD_DIM - ROT_DIM
    ones = jnp.ones((seq, rest), F32)
    zeros = jnp.zeros((seq, rest), F32)
    zh = jnp.zeros((seq, half), F32)
    cos_t = jnp.concatenate([cos, cos, ones], axis=1)
    shi_t = jnp.concatenate([zh, sin, zeros], axis=1)
    slo_t = jnp.concatenate([-sin, zh, zeros], axis=1)
    rep = LANES // HEAD_DIM
    return (jnp.tile(cos_t, (1, rep)), jnp.tile(shi_t, (1, rep)), jnp.tile(slo_t, (1, rep)),
            cos.T, sin.T)


def _qkv(x, norm_1, w_qv_t, w_k):
    bn, seq, d = x.shape
    tile = TILE_QKV
    nt = seq // tile
    cos_t, shi_t, slo_t, cos_fm, sin_fm = _rope_tables(seq)
    kern = functools.partial(_qkv_kernel, tile=tile)
    tab = pl.BlockSpec((tile, LANES), lambda b, j: (j, 0))
    tab_fm = pl.BlockSpec((ROT_DIM // 2, tile), lambda b, j: (0, j))

    def const(shape):
        return pl.BlockSpec(shape, lambda b, j: (0,) * len(shape), pipeline_mode=pl.Buffered(1))

    return pl.pallas_call(
        kern,
        grid=(bn, nt),
        in_specs=[
            pl.BlockSpec((1, tile, d), lambda b, j: (b, j, 0)),
            const((1, d)),
            const(w_qv_t.shape),
            const(w_k.shape),
            tab, tab, tab, tab_fm, tab_fm,
        ],
        out_specs=[pl.BlockSpec((1, C_WIDTH, tile), lambda b, j: (b, 0, j)),
                   pl.BlockSpec((1, tile, KV_PAD_WIDTH), lambda b, j: (b, j, 0)),
                   pl.BlockSpec((1, KV_WIDTH, tile), lambda b, j: (b, 0, j))],
        out_shape=[jax.ShapeDtypeStruct((bn, C_WIDTH, seq), BF16),
                   jax.ShapeDtypeStruct((bn, seq, KV_PAD_WIDTH), BF16),
                   jax.ShapeDtypeStruct((bn, KV_WIDTH, seq), BF16)],
        compiler_params=pltpu.CompilerParams(
            dimension_semantics=("arbitrary", "arbitrary"), vmem_limit_bytes=VMEM_LIMIT),
        name="layer1_qkv",
    )(x, norm_1.reshape(1, d), w_qv_t, w_k, cos_t, shi_t, slo_t, cos_fm, sin_fm)


def _attn_kernel(sink_ref, x_ref, qt_ref, kap_ref, kat_ref, kan_ref, vtp_ref, vtt_ref, vtn_ref,
                 n1_ref, wgate_ref, wout_ref, fn_ref, o_ref,
                 kae_ref, vte_ref, s_ref, p_ref, sg_ref, y_ref, *, tile, seq):
    j = pl.program_id(1)
    nblk = tile // ATTN_BLOCK
    last_blk = seq // ATTN_BLOCK - 1
    x = x_ref[0]
    h = _rms(x, n1_ref[...]).astype(BF16)
    sg_ref[...] = _silu(_dot(h, wgate_ref[...]))

    kae_ref[0:ATTN_BLOCK, :] = kap_ref[0]
    kae_ref[ATTN_BLOCK:ATTN_BLOCK + tile, :] = kat_ref[0]
    kae_ref[ATTN_BLOCK + tile:, :] = kan_ref[0]
    vte_ref[:, 0:ATTN_BLOCK] = vtp_ref[0]
    vte_ref[:, ATTN_BLOCK:ATTN_BLOCK + tile] = vtt_ref[0]
    vte_ref[:, ATTN_BLOCK + tile:] = vtn_ref[0]
    diff = (lax.broadcasted_iota(jnp.int32, (ATTN_BLOCK, ATTN_BLOCK), 0)
            - lax.broadcasted_iota(jnp.int32, (ATTN_BLOCK, ATTN_BLOCK), 1))
    never = 2 * ATTN_BLOCK

    def scores(u):
        i, kh = divmod(u, N_KV_HEADS)
        slot = u % 2
        c0 = i * ATTN_BLOCK
        h0 = kh * GQA_GROUP
        qg = jnp.concatenate(
            [qt_ref[0, (h0 + g) * HEAD_DIM:(h0 + g + 1) * HEAD_DIM, c0:c0 + ATTN_BLOCK]
             for g in range(GQA_GROUP)], axis=1)
        rq = jnp.concatenate([qg, jnp.zeros((LANES - HEAD_DIM, GROUP_Q), BF16)], axis=0)
        ka = kae_ref[c0:c0 + BAND, kh * LANES:(kh + 1) * LANES]
        s_ref[slot] = _dot(ka, rq)

    def softmax(u):
        i, kh = divmod(u, N_KV_HEADS)
        slot = u % 2
        gblk = j * nblk + i
        ok_prev = diff >= jnp.where(gblk > 0, 0, never)
        ok_next = diff <= jnp.where(gblk < last_blk, 0, -never)
        inv = []
        for g in range(GQA_GROUP):
            cs = slice(g * ATTN_BLOCK, (g + 1) * ATTN_BLOCK)
            sink = sink_ref[kh * GQA_GROUP + g]
            s0 = jnp.where(ok_prev, s_ref[slot, 0:ATTN_BLOCK, cs], NEG_INF)
            s1 = s_ref[slot, ATTN_BLOCK:2 * ATTN_BLOCK, cs]
            s2 = jnp.where(ok_next, s_ref[slot, 2 * ATTN_BLOCK:, cs], NEG_INF)
            m = jnp.max(jnp.maximum(jnp.maximum(s0, s1), s2), axis=0, keepdims=True)
            m = jnp.maximum(m, sink)
            p0 = jnp.exp(s0 - m)
            p1 = jnp.exp(s1 - m)
            p2 = jnp.exp(s2 - m)
            l = jnp.sum(p0 + p1 + p2, axis=0, keepdims=True) + jnp.exp(sink - m)
            p_ref[slot, 0:ATTN_BLOCK, cs] = p0.astype(BF16)
            p_ref[slot, ATTN_BLOCK:2 * ATTN_BLOCK, cs] = p1.astype(BF16)
            p_ref[slot, 2 * ATTN_BLOCK:, cs] = p2.astype(BF16)
            inv.append(1.0 / l)
        return inv

    def weighted(u, inv):
        i, kh = divmod(u, N_KV_HEADS)
        slot = u % 2
        c0 = i * ATTN_BLOCK
        vt = vte_ref[kh * HEAD_DIM:(kh + 1) * HEAD_DIM, c0:c0 + BAND]
        ot = _dot(vt, p_ref[slot])
        yt = jnp.concatenate(
            [ot[:, g * ATTN_BLOCK:(g + 1) * ATTN_BLOCK] * inv[g] for g in range(GQA_GROUP)],
            axis=0)
        cs = slice(kh * GQA_GROUP * HEAD_DIM, (kh + 1) * GQA_GROUP * HEAD_DIM)
        y = yt.T * sg_ref[c0:c0 + ATTN_BLOCK, cs]
        y_ref[c0:c0 + ATTN_BLOCK, cs] = y.astype(BF16)

    units = nblk * N_KV_HEADS
    pending = {}
    for step in range(units + 2):
        if step < units:
            scores(step)
        if 1 <= step <= units:
            pending[step - 1] = softmax(step - 1)
        if step >= 2:
            weighted(step - 2, pending.pop(step - 2))

    out = x + _dot(y_ref[...], wout_ref[...])
    o_ref[0] = _rms(out, fn_ref[...])


def _attention(x, qt, ka, vt, norm_1, w_gate, sink_1, w_out_1, final_norm):
    bn, seq, d = x.shape
    tile = TILE_ATTN
    nt = seq // tile
    per = tile // ATTN_BLOCK
    nb = seq // ATTN_BLOCK

    def const(shape):
        return pl.BlockSpec(shape, lambda b, j: (0,) * len(shape), pipeline_mode=pl.Buffered(1))

    def prev_blk(j):
        return jnp.maximum(j * per - 1, 0)

    def next_blk(j):
        return jnp.minimum((j + 1) * per, nb - 1)

    kern = functools.partial(_attn_kernel, tile=tile, seq=seq)
    return pl.pallas_call(
        kern,
        grid=(bn, nt),
        in_specs=[
            pl.BlockSpec(memory_space=pltpu.SMEM),
            pl.BlockSpec((1, tile, d), lambda b, j: (b, j, 0)),
            pl.BlockSpec((1, C_WIDTH, tile), lambda b, j: (b, 0, j)),
            pl.BlockSpec((1, ATTN_BLOCK, KV_PAD_WIDTH), lambda b, j: (b, prev_blk(j), 0)),
            pl.BlockSpec((1, tile, KV_PAD_WIDTH), lambda b, j: (b, j, 0)),
            pl.BlockSpec((1, ATTN_BLOCK, KV_PAD_WIDTH), lambda b, j: (b, next_blk(j), 0)),
            pl.BlockSpec((1, KV_WIDTH, ATTN_BLOCK), lambda b, j: (b, 0, prev_blk(j))),
            pl.BlockSpec((1, KV_WIDTH, tile), lambda b, j: (b, 0, j)),
            pl.BlockSpec((1, KV_WIDTH, ATTN_BLOCK), lambda b, j: (b, 0, next_blk(j))),
            const((1, d)),
            const(w_gate.shape),
            const(w_out_1.shape),
            const((1, d)),
        ],
        out_specs=pl.BlockSpec((1, tile, d), lambda b, j: (b, j, 0)),
        out_shape=jax.ShapeDtypeStruct(x.shape, F32),
        scratch_shapes=[
            pltpu.VMEM((tile + 2 * ATTN_BLOCK, KV_PAD_WIDTH), BF16),
            pltpu.VMEM((KV_WIDTH, tile + 2 * ATTN_BLOCK), BF16),
            pltpu.VMEM((2, BAND, GROUP_Q), F32),
            pltpu.VMEM((2, BAND, GROUP_Q), BF16),
            pltpu.VMEM((tile, C_WIDTH), F32),
            pltpu.VMEM((tile, C_WIDTH), BF16),
        ],
        compiler_params=pltpu.CompilerParams(
            dimension_semantics=("arbitrary", "arbitrary"), vmem_limit_bytes=VMEM_LIMIT),
        name="layer1_attention",
    )(sink_1, x, qt, ka, ka, ka, vt, vt, vt,
      norm_1.reshape(1, d), w_gate, w_out_1, final_norm.reshape(1, d))


def kernel(x, norm_0, w_in_0, a_v_norm_0, a_spatial_w_0, a_spatial_b_0, b_group_w_0, b_scale_0,
           w_out_0, norm_1, w_in_1, sink_1, w_out_1, final_norm):
    x1 = _layer0(x, norm_0, w_in_0, a_v_norm_0, a_spatial_w_0, a_spatial_b_0, b_group_w_0,
                 b_scale_0, w_out_0)
    k0 = C_WIDTH
    v0 = C_WIDTH + KV_WIDTH
    g0 = C_WIDTH + 2 * KV_WIDTH
    w_qv_t = jnp.concatenate([w_in_1[:, :k0], w_in_1[:, v0:g0]], axis=1).T.astype(BF16)
    w_k = w_in_1[:, k0:v0].astype(BF16)
    w_gate = w_in_1[:, g0:].astype(BF16)
    qt, ka, vt = _qkv(x1, norm_1, w_qv_t, w_k)
    return _attention(x1, qt, ka, vt, norm_1, w_gate, sink_1, w_out_1.astype(BF16), final_norm)
```

```python
import functools

import jax
import jax.numpy as jnp
from jax import lax
from jax.experimental import pallas as pl
from jax.experimental.pallas import tpu as pltpu

F32 = jnp.float32
BF16 = jnp.bfloat16

D_MODEL = 1024
EPS = 1e-6
NEG_INF = -1e30
CHUNK = 128
A_GROUPS = 4
A_WIDTH = D_MODEL
A_GROUP_DIM = A_WIDTH // A_GROUPS
B_WIDTH = D_MODEL
POOL_WINDOWS = (2, 4, 8, 16)
B_GROUP_DIM = B_WIDTH // len(POOL_WINDOWS)
POOL_HALO = max(POOL_WINDOWS) // 2
N_HEADS = 16
N_KV_HEADS = 4
HEAD_DIM = 64
GQA_GROUP = N_HEADS // N_KV_HEADS
C_WIDTH = N_HEADS * HEAD_DIM
KV_WIDTH = N_KV_HEADS * HEAD_DIM
WINDOW = 128
ATTN_BLOCK = 128
BAND = 3 * ATTN_BLOCK
ROPE_THETA = 500000.0
ROT_DIM = HEAD_DIM // 4
LANES = 128
KV_PAD_WIDTH = N_KV_HEADS * LANES
GROUP_Q = GQA_GROUP * ATTN_BLOCK

TILE_L0 = 512
TILE_QKV = 512
TILE_ATTN = 512
VMEM_LIMIT = 56 * 1024 * 1024


def _rms(x, g):
    return x * lax.rsqrt(jnp.mean(x * x, axis=-1, keepdims=True) + EPS) * g


def _silu(x):
    return x / (1.0 + jnp.exp(-x))


def _dot(a, b):
    return jnp.dot(a, b, preferred_element_type=F32)


def _dot_nt(a, b):
    return lax.dot_general(a, b, (((1,), (1,)), ((), ())), preferred_element_type=F32)


def _layer0_kernel(x_ref, xp_ref, xn_ref, n0_ref, win_ref, gv_ref, ws_ref, bsb_ref, wg_ref,
                   sc_ref, wout_ref, o_ref, hext_ref, v_ref, cat_ref, *, tile, seq):
    j = pl.program_id(1)
    nj = pl.num_programs(1)
    n0 = n0_ref[...]
    x = x_ref[0]
    h = _rms(x, n0).astype(BF16)
    hext_ref[0:tile, :] = h
    hp = jnp.where(j > 0, _rms(xp_ref[0], n0), 0.0)
    hn = jnp.where(j < nj - 1, _rms(xn_ref[0], n0), 0.0)
    hext_ref[tile:tile + 2 * POOL_HALO, :] = jnp.concatenate([hp, hn], axis=0).astype(BF16)

    def col(k):
        return k * D_MODEL

    a_v = jax.nn.gelu(_dot(h, win_ref[:, col(1):col(2)]))
    v_ref[...] = _rms(a_v, gv_ref[...]).astype(BF16)
    for g in range(A_GROUPS):
        c0 = g * A_GROUP_DIM
        c1 = c0 + A_GROUP_DIM
        u = jax.nn.gelu(_dot(h, win_ref[:, col(0) + c0:col(0) + c1]))
        gate = _silu(_dot(h, win_ref[:, col(2) + c0:col(2) + c1]))
        for c in range(tile // CHUNK):
            r0 = c * CHUNK
            r1 = r0 + CHUNK
            mixed = _dot(ws_ref[g], v_ref[r0:r1, c0:c1]) + bsb_ref[:, c0:c1]
            cat_ref[r0:r1, c0:c1] = (u[r0:r1] * mixed * gate[r0:r1]).astype(BF16)

    bxe = _dot(hext_ref[...], win_ref[:, col(3):col(4)])
    n_ext = tile + 2 * POOL_HALO
    edge_row = lax.broadcasted_iota(jnp.int32, (POOL_HALO, 1), 0)
    for gi, w in enumerate(POOL_WINDOWS):
        c0 = gi * B_GROUP_DIM
        c1 = c0 + B_GROUP_DIM
        half = w // 2
        e = jnp.concatenate([bxe[tile:tile + POOL_HALO, c0:c1], bxe[0:tile, c0:c1],
                             bxe[tile + POOL_HALO:, c0:c1]], axis=0)
        a = e
        span = 1
        while span < half:
            a = a + pltpu.roll(a, n_ext - span, 0)
            span *= 2
        win_sum = (a + pltpu.roll(a, half, 0))[POOL_HALO:POOL_HALO + tile]

        def edge_mean(slab, row0, half=half):
            t = j * tile + row0 + edge_row
            cnt = (jnp.minimum(t + half, seq) - jnp.maximum(t - half, 0)).astype(F32)
            return slab / cnt

        mean = jnp.concatenate(
            [edge_mean(win_sum[0:POOL_HALO], 0),
             win_sum[POOL_HALO:tile - POOL_HALO] * (1.0 / w),
             edge_mean(win_sum[tile - POOL_HALO:], tile - POOL_HALO)], axis=0)
        p = (mean - bxe[0:tile, c0:c1]).astype(BF16)
        gate = _silu(_dot(h, win_ref[:, col(4) + c0:col(4) + c1]))
        yb = _dot(p, wg_ref[gi]) * sc_ref[:, c0:c1] * gate
        cat_ref[:, A_WIDTH + c0:A_WIDTH + c1] = yb.astype(BF16)

    o_ref[0] = x + _dot(cat_ref[...], wout_ref[...])


def _layer0(x, norm_0, w_in_0, a_v_norm_0, a_spatial_w_0, a_spatial_b_0, b_group_w_0, b_scale_0,
            w_out_0):
    bn, seq, d = x.shape
    tile = TILE_L0
    nt = seq // tile
    halo_blocks = seq // POOL_HALO
    win = w_in_0.astype(BF16)
    wout = w_out_0.astype(BF16)
    ws = a_spatial_w_0.astype(BF16)
    wg = b_group_w_0.astype(BF16)
    bsb = jnp.repeat(a_spatial_b_0.T, A_GROUP_DIM, axis=1)

    def const(shape):
        return pl.BlockSpec(shape, lambda b, j: (0,) * len(shape), pipeline_mode=pl.Buffered(1))

    per_halo = tile // POOL_HALO
    kern = functools.partial(_layer0_kernel, tile=tile, seq=seq)
    return pl.pallas_call(
        kern,
        grid=(bn, nt),
        in_specs=[
            pl.BlockSpec((1, tile, d), lambda b, j: (b, j, 0)),
            pl.BlockSpec((1, POOL_HALO, d), lambda b, j: (b, jnp.maximum(j * per_halo - 1, 0), 0)),
            pl.BlockSpec((1, POOL_HALO, d),
                         lambda b, j: (b, jnp.minimum((j + 1) * per_halo, halo_blocks - 1), 0)),
            const((1, d)),
            const(win.shape),
            const((1, A_WIDTH)),
            const(ws.shape),
            const(bsb.shape),
            const(wg.shape),
            const((1, B_WIDTH)),
            const(wout.shape),
        ],
        out_specs=pl.BlockSpec((1, tile, d), lambda b, j: (b, j, 0)),
        out_shape=jax.ShapeDtypeStruct(x.shape, F32),
        scratch_shapes=[
            pltpu.VMEM((tile + 2 * POOL_HALO, d), BF16),
            pltpu.VMEM((tile, A_WIDTH), BF16),
            pltpu.VMEM((tile, A_WIDTH + B_WIDTH), BF16),
        ],
        compiler_params=pltpu.CompilerParams(
            dimension_semantics=("arbitrary", "arbitrary"), vmem_limit_bytes=VMEM_LIMIT),
        name="layer0_mixer",
    )(x, x, x, norm_0.reshape(1, d), win, a_v_norm_0.reshape(1, A_WIDTH), ws, bsb, wg,
      b_scale_0.reshape(1, B_WIDTH), wout)


def _rope_lanes(xb, cos, sin_hi, sin_lo):
    return (xb * cos + pltpu.roll(xb, ROT_DIM // 2, 1) * sin_hi
            + pltpu.roll(xb, LANES - ROT_DIM // 2, 1) * sin_lo)


def _qkv_kernel(x_ref, n1_ref, wqv_ref, wk_ref, cos_ref, shi_ref, slo_ref, cost_ref, sint_ref,
                qt_ref, ka_ref, vt_ref, *, tile):
    h = _rms(x_ref[0], n1_ref[...]).astype(BF16)
    half = ROT_DIM // 2
    scale = HEAD_DIM ** -0.5

    qv = _dot_nt(wqv_ref[...], h)
    ct = cost_ref[...]
    st = sint_ref[...]
    for hd in range(N_HEADS):
        r0 = hd * HEAD_DIM
        x1 = qv[r0:r0 + half]
        x2 = qv[r0 + half:r0 + ROT_DIM]
        rot = jnp.concatenate([x1 * ct - x2 * st, x2 * ct + x1 * st], axis=0)
        qt_ref[0, r0:r0 + ROT_DIM, :] = (rot * scale).astype(BF16)
        qt_ref[0, r0 + ROT_DIM:r0 + HEAD_DIM, :] = (
            qv[r0 + ROT_DIM:r0 + HEAD_DIM] * scale).astype(BF16)
    vt_ref[0] = qv[C_WIDTH:].astype(BF16)

    cos = cos_ref[...]
    shi = shi_ref[...]
    slo = slo_ref[...]
    low = lax.broadcasted_iota(jnp.int32, (tile, LANES), 1) < HEAD_DIM
    for cb in range(KV_WIDTH // LANES):
        kb = _rope_lanes(_dot(h, wk_ref[:, cb * LANES:(cb + 1) * LANES]), cos, shi, slo)
        sw = pltpu.roll(kb, HEAD_DIM, 1)
        e = 2 * cb
        o = 2 * cb + 1
        ka_ref[0, :, e * LANES:(e + 1) * LANES] = jnp.where(low, kb, 0.0).astype(BF16)
        ka_ref[0, :, o * LANES:(o + 1) * LANES] = jnp.where(low, sw, 0.0).astype(BF16)


def _rope_tables(seq):
    inv = ROPE_THETA ** (-jnp.arange(0, ROT_DIM, 2, dtype=F32) / ROT_DIM)
    ang = jnp.arange(seq).astype(F32)[:, None] * inv[None, :]
    cos = jnp.cos(ang)
    sin = jnp.sin(ang)
    half = ROT_DIM // 2
    rest = HEAD_DIM - ROT_DIM
    ones = jnp.ones((seq, rest), F32)
    zeros = jnp.zeros((seq, rest), F32)
    zh = jnp.zeros((seq, half), F32)
    cos_t = jnp.concatenate([cos, cos, ones], axis=1)
    shi_t = jnp.concatenate([zh, sin, zeros], axis=1)
    slo_t = jnp.concatenate([-sin, zh, zeros], axis=1)
    rep = LANES // HEAD_DIM
    return (jnp.tile(cos_t, (1, rep)), jnp.tile(shi_t, (1, rep)), jnp.tile(slo_t, (1, rep)),
            cos.T, sin.T)


def _qkv(x, norm_1, w_qv_t, w_k):
    bn, seq, d = x.shape
    tile = TILE_QKV
    nt = seq // tile
    cos_t, shi_t, slo_t, cos_fm, sin_fm = _rope_tables(seq)
    kern = functools.partial(_qkv_kernel, tile=tile)
    tab = pl.BlockSpec((tile, LANES), lambda b, j: (j, 0))
    tab_fm = pl.BlockSpec((ROT_DIM // 2, tile), lambda b, j: (0, j))

    def const(shape):
        return pl.BlockSpec(shape, lambda b, j: (0,) * len(shape), pipeline_mode=pl.Buffered(1))

    return pl.pallas_call(
        kern,
        grid=(bn, nt),
        in_specs=[
            pl.BlockSpec((1, tile, d), lambda b, j: (b, j, 0)),
            const((1, d)),
            const(w_qv_t.shape),
            const(w_k.shape),
            tab, tab, tab, tab_fm, tab_fm,
        ],
        out_specs=[pl.BlockSpec((1, C_WIDTH, tile), lambda b, j: (b, 0, j)),
                   pl.BlockSpec((1, tile, KV_PAD_WIDTH), lambda b, j: (b, j, 0)),
                   pl.BlockSpec((1, KV_WIDTH, tile), lambda b, j: (b, 0, j))],
        out_shape=[jax.ShapeDtypeStruct((bn, C_WIDTH, seq), BF16),
                   jax.ShapeDtypeStruct((bn, seq, KV_PAD_WIDTH), BF16),
                   jax.ShapeDtypeStruct((bn, KV_WIDTH, seq), BF16)],
        compiler_params=pltpu.CompilerParams(
            dimension_semantics=("arbitrary", "arbitrary"), vmem_limit_bytes=VMEM_LIMIT),
        name="layer1_qkv",
    )(x, norm_1.reshape(1, d), w_qv_t, w_k, cos_t, shi_t, slo_t, cos_fm, sin_fm)


def _attn_kernel(sink_ref, x_ref, qt_ref, kap_ref, kat_ref, kan_ref, vtp_ref, vtt_ref, vtn_ref,
                 n1_ref, wgate_ref, wout_ref, fn_ref, o_ref,
                 kae_ref, vte_ref, s_ref, p_ref, sg_ref, y_ref, *, tile, seq):
    j = pl.program_id(1)
    nblk = tile // ATTN_BLOCK
    last_blk = seq // ATTN_BLOCK - 1
    x = x_ref[0]
    h = _rms(x, n1_ref[...]).astype(BF16)
    sg_ref[...] = _silu(_dot(h, wgate_ref[...]))

    kae_ref[0:ATTN_BLOCK, :] = kap_ref[0]
    kae_ref[ATTN_BLOCK:ATTN_BLOCK + tile, :] = kat_ref[0]
    kae_ref[ATTN_BLOCK + tile:, :] = kan_ref[0]
    vte_ref[:, 0:ATTN_BLOCK] = vtp_ref[0]
    vte_ref[:, ATTN_BLOCK:ATTN_BLOCK + tile] = vtt_ref[0]
    vte_ref[:, ATTN_BLOCK + tile:] = vtn_ref[0]
    diff = (lax.broadcasted_iota(jnp.int32, (ATTN_BLOCK, ATTN_BLOCK), 0)
            - lax.broadcasted_iota(jnp.int32, (ATTN_BLOCK, ATTN_BLOCK), 1))
    never = 2 * ATTN_BLOCK

    def scores(u):
        i, kh = divmod(u, N_KV_HEADS)
        slot = u % 2
        c0 = i * ATTN_BLOCK
        h0 = kh * GQA_GROUP
        qg = jnp.concatenate(
            [qt_ref[0, (h0 + g) * HEAD_DIM:(h0 + g + 1) * HEAD_DIM, c0:c0 + ATTN_BLOCK]
             for g in range(GQA_GROUP)], axis=1)
        rq = jnp.concatenate([qg, jnp.zeros((LANES - HEAD_DIM, GROUP_Q), BF16)], axis=0)
        ka = kae_ref[c0:c0 + BAND, kh * LANES:(kh + 1) * LANES]
        s_ref[slot] = _dot(ka, rq)

    def softmax(u):
        i, kh = divmod(u, N_KV_HEADS)
        slot = u % 2
        gblk = j * nblk + i
        ok_prev = diff >= jnp.where(gblk > 0, 0, never)
        ok_next = diff <= jnp.where(gblk < last_blk, 0, -never)
        inv = []
        for g in range(GQA_GROUP):
            cs = slice(g * ATTN_BLOCK, (g + 1) * ATTN_BLOCK)
            sink = sink_ref[kh * GQA_GROUP + g]
            s0 = jnp.where(ok_prev, s_ref[slot, 0:ATTN_BLOCK, cs], NEG_INF)
            s1 = s_ref[slot, ATTN_BLOCK:2 * ATTN_BLOCK, cs]
            s2 = jnp.where(ok_next, s_ref[slot, 2 * ATTN_BLOCK:, cs], NEG_INF)
            m = jnp.max(jnp.maximum(jnp.maximum(s0, s1), s2), axis=0, keepdims=True)
            m = jnp.maximum(m, sink)
            p0 = jnp.exp(s0 - m)
            p1 = jnp.exp(s1 - m)
            p2 = jnp.exp(s2 - m)
            l = jnp.sum(p0 + p1 + p2, axis=0, keepdims=True) + jnp.exp(sink - m)
            p_ref[slot, 0:ATTN_BLOCK, cs] = p0.astype(BF16)
            p_ref[slot, ATTN_BLOCK:2 * ATTN_BLOCK, cs] = p1.astype(BF16)
            p_ref[slot, 2 * ATTN_BLOCK:, cs] = p2.astype(BF16)
            inv.append(1.0 / l)
        return inv

    def weighted(u, inv):
        i, kh = divmod(u, N_KV_HEADS)
        slot = u % 2
        c0 = i * ATTN_BLOCK
        vt = vte_ref[kh * HEAD_DIM:(kh + 1) * HEAD_DIM, c0:c0 + BAND]
        ot = _dot(vt, p_ref[slot])
        yt = jnp.concatenate(
            [ot[:, g * ATTN_BLOCK:(g + 1) * ATTN_BLOCK] * inv[g] for g in range(GQA_GROUP)],
            axis=0)
        cs = slice(kh * GQA_GROUP * HEAD_DIM, (kh + 1) * GQA_GROUP * HEAD_DIM)
        y = yt.T * sg_ref[c0:c0 + ATTN_BLOCK, cs]
        y_ref[c0:c0 + ATTN_BLOCK, cs] = y.astype(BF16)

    units = nblk * N_KV_HEADS
    pending = {}
    for step in range(units + 2):
        if step < units:
            scores(step)
        if 1 <= step <= units:
            pending[step - 1] = softmax(step - 1)
        if step >= 2:
            weighted(step - 2, pending.pop(step - 2))

    out = x + _dot(y_ref[...], wout_ref[...])
    o_ref[0] = _rms(out, fn_ref[...])


def _attention(x, qt, ka, vt, norm_1, w_gate, sink_1, w_out_1, final_norm):
    bn, seq, d = x.shape
    tile = TILE_ATTN
    nt = seq // tile
    per = tile // ATTN_BLOCK
    nb = seq // ATTN_BLOCK

    def const(shape):
        return pl.BlockSpec(shape, lambda b, j: (0,) * len(shape), pipeline_mode=pl.Buffered(1))

    def prev_blk(j):
        return jnp.maximum(j * per - 1, 0)

    def next_blk(j):
        return jnp.minimum((j + 1) * per, nb - 1)

    kern = functools.partial(_attn_kernel, tile=tile, seq=seq)
    return pl.pallas_call(
        kern,
        grid=(bn, nt),
        in_specs=[
            pl.BlockSpec(memory_space=pltpu.SMEM),
            pl.BlockSpec((1, tile, d), lambda b, j: (b, j, 0)),
            pl.BlockSpec((1, C_WIDTH, tile), lambda b, j: (b, 0, j)),
            pl.BlockSpec((1, ATTN_BLOCK, KV_PAD_WIDTH), lambda b, j: (b, prev_blk(j), 0)),
            pl.BlockSpec((1, tile, KV_PAD_WIDTH), lambda b, j: (b, j, 0)),
            pl.BlockSpec((1, ATTN_BLOCK, KV_PAD_WIDTH), lambda b, j: (b, next_blk(j), 0)),
            pl.BlockSpec((1, KV_WIDTH, ATTN_BLOCK), lambda b, j: (b, 0, prev_blk(j))),
            pl.BlockSpec((1, KV_WIDTH, tile), lambda b, j: (b, 0, j)),
            pl.BlockSpec((1, KV_WIDTH, ATTN_BLOCK), lambda b, j: (b, 0, next_blk(j))),
            const((1, d)),
            const(w_gate.shape),
            const(w_out_1.shape),
            const((1, d)),
        ],
        out_specs=pl.BlockSpec((1, tile, d), lambda b, j: (b, j, 0)),
        out_shape=jax.ShapeDtypeStruct(x.shape, F32),
        scratch_shapes=[
            pltpu.VMEM((tile + 2 * ATTN_BLOCK, KV_PAD_WIDTH), BF16),
            pltpu.VMEM((KV_WIDTH, tile + 2 * ATTN_BLOCK), BF16),
            pltpu.VMEM((2, BAND, GROUP_Q), F32),
            pltpu.VMEM((2, BAND, GROUP_Q), BF16),
            pltpu.VMEM((tile, C_WIDTH), F32),
            pltpu.VMEM((tile, C_WIDTH), BF16),
        ],
        compiler_params=pltpu.CompilerParams(
            dimension_semantics=("arbitrary", "arbitrary"), vmem_limit_bytes=VMEM_LIMIT),
        name="layer1_attention",
    )(sink_1, x, qt, ka, ka, ka, vt, vt, vt,
      norm_1.reshape(1, d), w_gate, w_out_1, final_norm.reshape(1, d))


def kernel(x, norm_0, w_in_0, a_v_norm_0, a_spatial_w_0, a_spatial_b_0, b_group_w_0, b_scale_0,
           w_out_0, norm_1, w_in_1, sink_1, w_out_1, final_norm):
    x1 = _layer0(x, norm_0, w_in_0, a_v_norm_0, a_spatial_w_0, a_spatial_b_0, b_group_w_0,
                 b_scale_0, w_out_0)
    k0 = C_WIDTH
    v0 = C_WIDTH + KV_WIDTH
    g0 = C_WIDTH + 2 * KV_WIDTH
    w_qv_t = jnp.concatenate([w_in_1[:, :k0], w_in_1[:, v0:g0]], axis=1).T.astype(BF16)
    w_k = w_in_1[:, k0:v0].astype(BF16)
    w_gate = w_in_1[:, g0:].astype(BF16)
    qt, ka, vt = _qkv(x1, norm_1, w_qv_t, w_k)
    return _attention(x1, qt, ka, vt, norm_1, w_gate, sink_1, w_out_1.astype(BF16), final_norm)
```

```python
import functools
import math

import jax
import jax.numpy as jnp
from jax import lax
from jax.experimental import pallas as pl
from jax.experimental.pallas import tpu as pltpu

F32 = jnp.float32
BF16 = jnp.bfloat16

D_MODEL = 1024
EPS = 1e-6
NEG_INF = -1e30
LOG2_E = math.log2(math.e)
CHUNK = 128
A_GROUPS = 4
A_WIDTH = D_MODEL
A_GROUP_DIM = A_WIDTH // A_GROUPS
B_WIDTH = D_MODEL
POOL_WINDOWS = (2, 4, 8, 16)
B_GROUP_DIM = B_WIDTH // len(POOL_WINDOWS)
POOL_HALO = max(POOL_WINDOWS) // 2
N_HEADS = 16
N_KV_HEADS = 4
HEAD_DIM = 64
GQA_GROUP = N_HEADS // N_KV_HEADS
C_WIDTH = N_HEADS * HEAD_DIM
KV_WIDTH = N_KV_HEADS * HEAD_DIM
WINDOW = 128
ATTN_BLOCK = 128
BAND = 3 * ATTN_BLOCK
ROPE_THETA = 500000.0
ROT_DIM = HEAD_DIM // 4
LANES = 128
KV_PAD_WIDTH = N_KV_HEADS * LANES
GROUP_Q = GQA_GROUP * ATTN_BLOCK
HEADS_PER_HALF = GQA_GROUP // 2
HALF_Q = HEADS_PER_HALF * ATTN_BLOCK

TILE_L0 = 512
TILE_QKV = 512
TILE_ATTN = 512
VMEM_LIMIT = 56 * 1024 * 1024


def _rms(x, g):
    return x * lax.rsqrt(jnp.mean(x * x, axis=-1, keepdims=True) + EPS) * g


def _silu(x):
    return x / (1.0 + jnp.exp(-x))


def _dot(a, b):
    return jnp.dot(a, b, preferred_element_type=F32)


def _dot_nt(a, b):
    return lax.dot_general(a, b, (((1,), (1,)), ((), ())), preferred_element_type=F32)


def _layer0_kernel(x_ref, xp_ref, xn_ref, n0_ref, win_ref, gv_ref, ws_ref, bsb_ref, wg_ref,
                   sc_ref, wout_ref, o_ref, hext_ref, v_ref, cat_ref, *, tile, seq):
    j = pl.program_id(1)
    nj = pl.num_programs(1)
    n0 = n0_ref[...]
    x = x_ref[0]
    h = _rms(x, n0).astype(BF16)
    hext_ref[0:tile, :] = h
    hp = jnp.where(j > 0, _rms(xp_ref[0], n0), 0.0)
    hn = jnp.where(j < nj - 1, _rms(xn_ref[0], n0), 0.0)
    hext_ref[tile:tile + 2 * POOL_HALO, :] = jnp.concatenate([hp, hn], axis=0).astype(BF16)

    def col(k):
        return k * D_MODEL

    a_v = jax.nn.gelu(_dot(h, win_ref[:, col(1):col(2)]))
    v_ref[...] = _rms(a_v, gv_ref[...]).astype(BF16)
    for g in range(A_GROUPS):
        c0 = g * A_GROUP_DIM
        c1 = c0 + A_GROUP_DIM
        u = jax.nn.gelu(_dot(h, win_ref[:, col(0) + c0:col(0) + c1]))
        gate = _silu(_dot(h, win_ref[:, col(2) + c0:col(2) + c1]))
        for c in range(tile // CHUNK):
            r0 = c * CHUNK
            r1 = r0 + CHUNK
            mixed = _dot(ws_ref[g], v_ref[r0:r1, c0:c1]) + bsb_ref[:, c0:c1]
            cat_ref[r0:r1, c0:c1] = (u[r0:r1] * mixed * gate[r0:r1]).astype(BF16)

    bxe = _dot(hext_ref[...], win_ref[:, col(3):col(4)])
    n_ext = tile + 2 * POOL_HALO
    edge_row = lax.broadcasted_iota(jnp.int32, (POOL_HALO, 1), 0)
    for gi, w in enumerate(POOL_WINDOWS):
        c0 = gi * B_GROUP_DIM
        c1 = c0 + B_GROUP_DIM
        half = w // 2
        e = jnp.concatenate([bxe[tile:tile + POOL_HALO, c0:c1], bxe[0:tile, c0:c1],
                             bxe[tile + POOL_HALO:, c0:c1]], axis=0)
        a = e
        span = 1
        while span < half:
            a = a + pltpu.roll(a, n_ext - span, 0)
            span *= 2
        win_sum = (a + pltpu.roll(a, half, 0))[POOL_HALO:POOL_HALO + tile]

        def edge_mean(slab, row0, half=half):
            t = j * tile + row0 + edge_row
            cnt = (jnp.minimum(t + half, seq) - jnp.maximum(t - half, 0)).astype(F32)
            return slab / cnt

        mean = jnp.concatenate(
            [edge_mean(win_sum[0:POOL_HALO], 0),
             win_sum[POOL_HALO:tile - POOL_HALO] * (1.0 / w),
             edge_mean(win_sum[tile - POOL_HALO:], tile - POOL_HALO)], axis=0)
        p = (mean - bxe[0:tile, c0:c1]).astype(BF16)
        gate = _silu(_dot(h, win_ref[:, col(4) + c0:col(4) + c1]))
        yb = _dot(p, wg_ref[gi]) * sc_ref[:, c0:c1] * gate
        cat_ref[:, A_WIDTH + c0:A_WIDTH + c1] = yb.astype(BF16)

    o_ref[0] = x + _dot(cat_ref[...], wout_ref[...])


def _layer0(x, norm_0, w_in_0, a_v_norm_0, a_spatial_w_0, a_spatial_b_0, b_group_w_0, b_scale_0,
            w_out_0):
    bn, seq, d = x.shape
    tile = TILE_L0
    nt = seq // tile
    halo_blocks = seq // POOL_HALO
    win = w_in_0.astype(BF16)
    wout = w_out_0.astype(BF16)
    ws = a_spatial_w_0.astype(BF16)
    wg = b_group_w_0.astype(BF16)
    bsb = jnp.repeat(a_spatial_b_0.T, A_GROUP_DIM, axis=1)

    def const(shape):
        return pl.BlockSpec(shape, lambda b, j: (0,) * len(shape), pipeline_mode=pl.Buffered(1))

    per_halo = tile // POOL_HALO
    kern = functools.partial(_layer0_kernel, tile=tile, seq=seq)
    return pl.pallas_call(
        kern,
        grid=(bn, nt),
        in_specs=[
            pl.BlockSpec((1, tile, d), lambda b, j: (b, j, 0)),
            pl.BlockSpec((1, POOL_HALO, d), lambda b, j: (b, jnp.maximum(j * per_halo - 1, 0), 0)),
            pl.BlockSpec((1, POOL_HALO, d),
                         lambda b, j: (b, jnp.minimum((j + 1) * per_halo, halo_blocks - 1), 0)),
            const((1, d)),
            const(win.shape),
            const((1, A_WIDTH)),
            const(ws.shape),
            const(bsb.shape),
            const(wg.shape),
            const((1, B_WIDTH)),
            const(wout.shape),
        ],
        out_specs=pl.BlockSpec((1, tile, d), lambda b, j: (b, j, 0)),
        out_shape=jax.ShapeDtypeStruct(x.shape, F32),
        scratch_shapes=[
            pltpu.VMEM((tile + 2 * POOL_HALO, d), BF16),
            pltpu.VMEM((tile, A_WIDTH), BF16),
            pltpu.VMEM((tile, A_WIDTH + B_WIDTH), BF16),
        ],
        compiler_params=pltpu.CompilerParams(
            dimension_semantics=("arbitrary", "arbitrary"), vmem_limit_bytes=VMEM_LIMIT),
        name="layer0_mixer",
    )(x, x, x, norm_0.reshape(1, d), win, a_v_norm_0.reshape(1, A_WIDTH), ws, bsb, wg,
      b_scale_0.reshape(1, B_WIDTH), wout)


def _rope_lanes(xb, cos, sin_hi, sin_lo):
    return (xb * cos + pltpu.roll(xb, ROT_DIM // 2, 1) * sin_hi
            + pltpu.roll(xb, LANES - ROT_DIM // 2, 1) * sin_lo)


def _qkv_kernel(x_ref, n1_ref, wqv_ref, wk_ref, cos_ref, shi_ref, slo_ref, cost_ref, sint_ref,
                qt_ref, ka_ref, vt_ref, *, tile):
    h = _rms(x_ref[0], n1_ref[...]).astype(BF16)
    half = ROT_DIM // 2
    scale = HEAD_DIM ** -0.5 * LOG2_E

    qv = _dot_nt(wqv_ref[...], h)
    ct = cost_ref[...]
    st = sint_ref[...]
    for hd in range(N_HEADS):
        r0 = hd * HEAD_DIM
        x1 = qv[r0:r0 + half]
        x2 = qv[r0 + half:r0 + ROT_DIM]
        rot = jnp.concatenate([x1 * ct - x2 * st, x2 * ct + x1 * st], axis=0)
        qt_ref[0, r0:r0 + ROT_DIM, :] = (rot * scale).astype(BF16)
        qt_ref[0, r0 + ROT_DIM:r0 + HEAD_DIM, :] = (
            qv[r0 + ROT_DIM:r0 + HEAD_DIM] * scale).astype(BF16)
    vt_ref[0] = qv[C_WIDTH:].astype(BF16)

    cos = cos_ref[...]
    shi = shi_ref[...]
    slo = slo_ref[...]
    low = lax.broadcasted_iota(jnp.int32, (tile, LANES), 1) < HEAD_DIM
    for cb in range(KV_WIDTH // LANES):
        kb = _rope_lanes(_dot(h, wk_ref[:, cb * LANES:(cb + 1) * LANES]), cos, shi, slo)
        sw = pltpu.roll(kb, HEAD_DIM, 1)
        e = 2 * cb
        o = 2 * cb + 1
        ka_ref[0, :, e * LANES:(e + 1) * LANES] = jnp.where(low, kb, 0.0).astype(BF16)
        ka_ref[0, :, o * LANES:(o + 1) * LANES] = jnp.where(low, sw, 0.0).astype(BF16)


def _rope_tables(seq):
    inv = ROPE_THETA ** (-jnp.arange(0, ROT_DIM, 2, dtype=F32) / ROT_DIM)
    ang = jnp.arange(seq).astype(F32)[:, None] * inv[None, :]
    cos = jnp.cos(ang)
    sin = jnp.sin(ang)
    half = ROT_DIM // 2
    rest = HEAD_DIM - ROT_DIM
    ones = jnp.ones((seq, rest), F32)
    zeros = jnp.zeros((seq, rest), F32)
    zh = jnp.zeros((seq, half), F32)
    cos_t = jnp.concatenate([cos, cos, ones], axis=1)
    shi_t = jnp.concatenate([zh, sin, zeros], axis=1)
    slo_t = jnp.concatenate([-sin, zh, zeros], axis=1)
    rep = LANES // HEAD_DIM
    return (jnp.tile(cos_t, (1, rep)), jnp.tile(shi_t, (1, rep)), jnp.tile(slo_t, (1, rep)),
            cos.T, sin.T)


def _qkv(x, norm_1, w_qv_t, w_k):
    bn, seq, d = x.shape
    tile = TILE_QKV
    nt = seq // tile
    cos_t, shi_t, slo_t, cos_fm, sin_fm = _rope_tables(seq)
    kern = functools.partial(_qkv_kernel, tile=tile)
    tab = pl.BlockSpec((tile, LANES), lambda b, j: (j, 0))
    tab_fm = pl.BlockSpec((ROT_DIM // 2, tile), lambda b, j: (0, j))

    def const(shape):
        return pl.BlockSpec(shape, lambda b, j: (0,) * len(shape), pipeline_mode=pl.Buffered(1))

    return pl.pallas_call(
        kern,
        grid=(bn, nt),
        in_specs=[
            pl.BlockSpec((1, tile, d), lambda b, j: (b, j, 0)),
            const((1, d)),
            const(w_qv_t.shape),
            const(w_k.shape),
            tab, tab, tab, tab_fm, tab_fm,
        ],
        out_specs=[pl.BlockSpec((1, C_WIDTH, tile), lambda b, j: (b, 0, j)),
                   pl.BlockSpec((1, tile, KV_PAD_WIDTH), lambda b, j: (b, j, 0)),
                   pl.BlockSpec((1, KV_WIDTH, tile), lambda b, j: (b, 0, j))],
        out_shape=[jax.ShapeDtypeStruct((bn, C_WIDTH, seq), BF16),
                   jax.ShapeDtypeStruct((bn, seq, KV_PAD_WIDTH), BF16),
                   jax.ShapeDtypeStruct((bn, KV_WIDTH, seq), BF16)],
        compiler_params=pltpu.CompilerParams(
            dimension_semantics=("arbitrary", "arbitrary"), vmem_limit_bytes=VMEM_LIMIT),
        name="layer1_qkv",
    )(x, norm_1.reshape(1, d), w_qv_t, w_k, cos_t, shi_t, slo_t, cos_fm, sin_fm)


def _attn_kernel(sink_ref, x_ref, qt_ref, kap_ref, kat_ref, kan_ref, vtp_ref, vtt_ref, vtn_ref,
                 n1_ref, wgate_ref, wout_ref, fn_ref, o_ref,
                 s_ref, p_ref, sg_ref, y_ref, *, tile, seq):
    j = pl.program_id(1)
    nblk = tile // ATTN_BLOCK
    last_blk = seq // ATTN_BLOCK - 1
    h = _rms(x_ref[0], n1_ref[...]).astype(BF16)


    def k_band(i, kh):
        cols = slice(kh * LANES, (kh + 1) * LANES)
        lo = max(i - 1, 0) * ATTN_BLOCK
        hi = min(i + 2, nblk) * ATTN_BLOCK
        parts = [kat_ref[0, lo:hi, cols]]
        if i == 0:
            parts.insert(0, kap_ref[0, :, cols])
        if i == nblk - 1:
            parts.append(kan_ref[0, :, cols])
        return jnp.concatenate(parts, axis=0)

    def vt_band(i, kh):
        rows = slice(kh * HEAD_DIM, (kh + 1) * HEAD_DIM)
        lo = max(i - 1, 0) * ATTN_BLOCK
        hi = min(i + 2, nblk) * ATTN_BLOCK
        parts = [vtt_ref[0, rows, lo:hi]]
        if i == 0:
            parts.insert(0, vtp_ref[0, rows, :])
        if i == nblk - 1:
            parts.append(vtn_ref[0, rows, :])
        return parts
    diff = (lax.broadcasted_iota(jnp.int32, (ATTN_BLOCK, ATTN_BLOCK), 0)
            - lax.broadcasted_iota(jnp.int32, (ATTN_BLOCK, ATTN_BLOCK), 1))
    never = 2 * ATTN_BLOCK
    ones_rows = jnp.ones((HEAD_DIM, BAND), BF16)

    def scores(u, half):
        i, kh = divmod(u, N_KV_HEADS)
        slot = u % 2
        c0 = i * ATTN_BLOCK
        h0 = kh * GQA_GROUP + half * HEADS_PER_HALF
        qg = jnp.concatenate(
            [qt_ref[0, (h0 + g) * HEAD_DIM:(h0 + g + 1) * HEAD_DIM, c0:c0 + ATTN_BLOCK]
             for g in range(HEADS_PER_HALF)], axis=1)
        rq = jnp.concatenate([qg, jnp.zeros((LANES - HEAD_DIM, HALF_Q), BF16)], axis=0)
        ka = k_band(i, kh)
        s_ref[slot, :, half * HALF_Q:(half + 1) * HALF_Q] = _dot(ka, rq)

    def softmax_head(u, g):
        i, kh = divmod(u, N_KV_HEADS)
        slot = u % 2
        gblk = j * nblk + i
        ok_prev = diff >= jnp.where(gblk > 0, 0, never)
        ok_next = diff <= jnp.where(gblk < last_blk, 0, -never)
        cs = slice(g * ATTN_BLOCK, (g + 1) * ATTN_BLOCK)
        sink = sink_ref[kh * GQA_GROUP + g] * LOG2_E
        s0 = jnp.where(ok_prev, s_ref[slot, 0:ATTN_BLOCK, cs], NEG_INF)
        s1 = s_ref[slot, ATTN_BLOCK:2 * ATTN_BLOCK, cs]
        s2 = jnp.where(ok_next, s_ref[slot, 2 * ATTN_BLOCK:, cs], NEG_INF)
        m = jnp.max(jnp.maximum(jnp.maximum(s0, s1), s2), axis=0, keepdims=True)
        m = jnp.maximum(m, sink)
        p0 = jnp.exp2(s0 - m)
        p1 = jnp.exp2(s1 - m)
        p2 = jnp.exp2(s2 - m)
        p_ref[slot, 0:ATTN_BLOCK, cs] = p0.astype(BF16)
        p_ref[slot, ATTN_BLOCK:2 * ATTN_BLOCK, cs] = p1.astype(BF16)
        p_ref[slot, 2 * ATTN_BLOCK:, cs] = p2.astype(BF16)
        return jnp.exp2(sink - m)

    def weighted(u, half, sink_share):
        i, kh = divmod(u, N_KV_HEADS)
        slot = u % 2
        c0 = i * ATTN_BLOCK
        vt = jnp.concatenate([jnp.concatenate(vt_band(i, kh), axis=1), ones_rows],
                             axis=0)
        ot = _dot(vt, p_ref[slot, :, half * HALF_Q:(half + 1) * HALF_Q])
        parts = []
        for g in range(HEADS_PER_HALF):
            cq = slice(g * ATTN_BLOCK, (g + 1) * ATTN_BLOCK)
            l = ot[HEAD_DIM:HEAD_DIM + 1, cq] + sink_share[g]
            parts.append(ot[0:HEAD_DIM, cq] * (1.0 / l))
        yt = jnp.concatenate(parts, axis=0)
        f0 = (kh * GQA_GROUP + half * HEADS_PER_HALF) * HEAD_DIM
        cs = slice(f0, f0 + HEADS_PER_HALF * HEAD_DIM)
        y = yt.T * sg_ref[c0:c0 + ATTN_BLOCK, cs]
        y_ref[c0:c0 + ATTN_BLOCK, cs] = y.astype(BF16)

    def gate_chunk(kh):
        cs = slice(kh * GQA_GROUP * HEAD_DIM, (kh + 1) * GQA_GROUP * HEAD_DIM)
        sg_ref[:, cs] = _silu(_dot(h, wgate_ref[:, cs]))

    def finish_rows(r0, n):
        rows = slice(r0, r0 + n)
        out = x_ref[0, rows, :] + _dot(y_ref[rows, :], wout_ref[...])
        o_ref[0, rows, :] = _rms(out, fn_ref[...])

    units = nblk * N_KV_HEADS
    inv = {}
    for step in range(units + 2):
        sc = step if step < units else None
        sm = step - 1 if 1 <= step <= units else None
        wt = step - 2 if step >= 2 else None
        if step < N_KV_HEADS:
            gate_chunk(step)
        for half in range(2):
            if sc is not None:
                scores(sc, half)
            if sm is not None:
                inv[sm, 2 * half] = softmax_head(sm, 2 * half)
            if wt is not None:
                weighted(wt, half, [inv.pop((wt, half * HEADS_PER_HALF + g))
                                    for g in range(HEADS_PER_HALF)])
            if sm is not None:
                inv[sm, 2 * half + 1] = softmax_head(sm, 2 * half + 1)
        if wt is not None and (wt + 1) % (units // 2) == 0:
            finish_rows((wt + 1 - units // 2) // N_KV_HEADS * ATTN_BLOCK, tile // 2)


def _attention(x, qt, ka, vt, norm_1, w_gate, sink_1, w_out_1, final_norm):
    bn, seq, d = x.shape
    tile = TILE_ATTN
    nt = seq // tile
    per = tile // ATTN_BLOCK
    nb = seq // ATTN_BLOCK

    def const(shape):
        return pl.BlockSpec(shape, lambda b, j: (0,) * len(shape), pipeline_mode=pl.Buffered(1))

    def prev_blk(j):
        return jnp.maximum(j * per - 1, 0)

    def next_blk(j):
        return jnp.minimum((j + 1) * per, nb - 1)

    kern = functools.partial(_attn_kernel, tile=tile, seq=seq)
    return pl.pallas_call(
        kern,
        grid=(bn, nt),
        in_specs=[
            pl.BlockSpec(memory_space=pltpu.SMEM),
            pl.BlockSpec((1, tile, d), lambda b, j: (b, j, 0)),
            pl.BlockSpec((1, C_WIDTH, tile), lambda b, j: (b, 0, j)),
            pl.BlockSpec((1, ATTN_BLOCK, KV_PAD_WIDTH), lambda b, j: (b, prev_blk(j), 0)),
            pl.BlockSpec((1, tile, KV_PAD_WIDTH), lambda b, j: (b, j, 0)),
            pl.BlockSpec((1, ATTN_BLOCK, KV_PAD_WIDTH), lambda b, j: (b, next_blk(j), 0)),
            pl.BlockSpec((1, KV_WIDTH, ATTN_BLOCK), lambda b, j: (b, 0, prev_blk(j))),
            pl.BlockSpec((1, KV_WIDTH, tile), lambda b, j: (b, 0, j)),
            pl.BlockSpec((1, KV_WIDTH, ATTN_BLOCK), lambda b, j: (b, 0, next_blk(j))),
            const((1, d)),
            const(w_gate.shape),
            const(w_out_1.shape),
            const((1, d)),
        ],
        out_specs=pl.BlockSpec((1, tile, d), lambda b, j: (b, j, 0)),
        out_shape=jax.ShapeDtypeStruct(x.shape, F32),
        scratch_shapes=[
            pltpu.VMEM((2, BAND, GROUP_Q), F32),
            pltpu.VMEM((2, BAND, GROUP_Q), BF16),
            pltpu.VMEM((tile, C_WIDTH), F32),
            pltpu.VMEM((tile, C_WIDTH), BF16),
        ],
        compiler_params=pltpu.CompilerParams(
            dimension_semantics=("arbitrary", "arbitrary"), vmem_limit_bytes=VMEM_LIMIT),
        name="layer1_attention",
    )(sink_1, x, qt, ka, ka, ka, vt, vt, vt,
      norm_1.reshape(1, d), w_gate, w_out_1, final_norm.reshape(1, d))


def kernel(x, norm_0, w_in_0, a_v_norm_0, a_spatial_w_0, a_spatial_b_0, b_group_w_0, b_scale_0,
           w_out_0, norm_1, w_in_1, sink_1, w_out_1, final_norm):
    x1 = _layer0(x, norm_0, w_in_0, a_v_norm_0, a_spatial_w_0, a_spatial_b_0, b_group_w_0,
                 b_scale_0, w_out_0)
    k0 = C_WIDTH
    v0 = C_WIDTH + KV_WIDTH
    g0 = C_WIDTH + 2 * KV_WIDTH
    w_qv_t = jnp.concatenate([w_in_1[:, :k0], w_in_1[:, v0:g0]], axis=1).T.astype(BF16)
    w_k = w_in_1[:, k0:v0].astype(BF16)
    w_gate = w_in_1[:, g0:].astype(BF16)
    qt, ka, vt = _qkv(x1, norm_1, w_qv_t, w_k)
    return _attention(x1, qt, ka, vt, norm_1, w_gate, sink_1, w_out_1.astype(BF16), final_norm)
```

```python
import functools
import math

import jax
import jax.numpy as jnp
from jax import lax
from jax.experimental import pallas as pl
from jax.experimental.pallas import tpu as pltpu

F32 = jnp.float32
BF16 = jnp.bfloat16

D_MODEL = 1024
EPS = 1e-6
NEG_INF = -1e30
LOG2_E = math.log2(math.e)
CHUNK = 128
A_GROUPS = 4
A_WIDTH = D_MODEL
A_GROUP_DIM = A_WIDTH // A_GROUPS
B_WIDTH = D_MODEL
POOL_WINDOWS = (2, 4, 8, 16)
B_GROUP_DIM = B_WIDTH // len(POOL_WINDOWS)
POOL_HALO = max(POOL_WINDOWS) // 2
N_HEADS = 16
N_KV_HEADS = 4
HEAD_DIM = 64
GQA_GROUP = N_HEADS // N_KV_HEADS
C_WIDTH = N_HEADS * HEAD_DIM
KV_WIDTH = N_KV_HEADS * HEAD_DIM
WINDOW = 128
ATTN_BLOCK = 128
BAND = 3 * ATTN_BLOCK
ROPE_THETA = 500000.0
ROT_DIM = HEAD_DIM // 4
LANES = 128
KV_PAD_WIDTH = N_KV_HEADS * LANES
GROUP_Q = GQA_GROUP * ATTN_BLOCK
HEADS_PER_HALF = GQA_GROUP // 2
HALF_Q = HEADS_PER_HALF * ATTN_BLOCK

TILE_L0 = 512
TILE_QKV = 512
TILE_ATTN = 512
VMEM_LIMIT = 56 * 1024 * 1024


def _rms(x, g):
    return x * lax.rsqrt(jnp.mean(x * x, axis=-1, keepdims=True) + EPS) * g


def _silu(x):
    return x / (1.0 + jnp.exp(-x))


def _dot(a, b):
    return jnp.dot(a, b, preferred_element_type=F32)


def _dot_nt(a, b):
    return lax.dot_general(a, b, (((1,), (1,)), ((), ())), preferred_element_type=F32)


def _layer0_kernel(x_ref, xp_ref, xn_ref, n0_ref, win_ref, gv_ref, ws_ref, bsb_ref, wg_ref,
                   sc_ref, wout_ref, o_ref, hext_ref, v_ref, cat_ref, *, tile, seq):
    j = pl.program_id(1)
    nj = pl.num_programs(1)
    n0 = n0_ref[...]
    x = x_ref[0]
    h = _rms(x, n0).astype(BF16)
    hext_ref[0:tile, :] = h
    hp = jnp.where(j > 0, _rms(xp_ref[0], n0), 0.0)
    hn = jnp.where(j < nj - 1, _rms(xn_ref[0], n0), 0.0)
    hext_ref[tile:tile + 2 * POOL_HALO, :] = jnp.concatenate([hp, hn], axis=0).astype(BF16)

    def col(k):
        return k * D_MODEL

    a_v = jax.nn.gelu(_dot(h, win_ref[:, col(1):col(2)]))
    v_ref[...] = _rms(a_v, gv_ref[...]).astype(BF16)
    for g in range(A_GROUPS):
        c0 = g * A_GROUP_DIM
        c1 = c0 + A_GROUP_DIM
        u = jax.nn.gelu(_dot(h, win_ref[:, col(0) + c0:col(0) + c1]))
        gate = _silu(_dot(h, win_ref[:, col(2) + c0:col(2) + c1]))
        for c in range(tile // CHUNK):
            r0 = c * CHUNK
            r1 = r0 + CHUNK
            mixed = _dot(ws_ref[g], v_ref[r0:r1, c0:c1]) + bsb_ref[:, c0:c1]
            cat_ref[r0:r1, c0:c1] = (u[r0:r1] * mixed * gate[r0:r1]).astype(BF16)

    bxe = _dot(hext_ref[...], win_ref[:, col(3):col(4)])
    n_ext = tile + 2 * POOL_HALO
    edge_row = lax.broadcasted_iota(jnp.int32, (POOL_HALO, 1), 0)
    for gi, w in enumerate(POOL_WINDOWS):
        c0 = gi * B_GROUP_DIM
        c1 = c0 + B_GROUP_DIM
        half = w // 2
        e = jnp.concatenate([bxe[tile:tile + POOL_HALO, c0:c1], bxe[0:tile, c0:c1],
                             bxe[tile + POOL_HALO:, c0:c1]], axis=0)
        a = e
        span = 1
        while span < half:
            a = a + pltpu.roll(a, n_ext - span, 0)
            span *= 2
        win_sum = (a + pltpu.roll(a, half, 0))[POOL_HALO:POOL_HALO + tile]

        def edge_mean(slab, row0, half=half):
            t = j * tile + row0 + edge_row
            cnt = (jnp.minimum(t + half, seq) - jnp.maximum(t - half, 0)).astype(F32)
            return slab / cnt

        mean = jnp.concatenate(
            [edge_mean(win_sum[0:POOL_HALO], 0),
             win_sum[POOL_HALO:tile - POOL_HALO] * (1.0 / w),
             edge_mean(win_sum[tile - POOL_HALO:], tile - POOL_HALO)], axis=0)
        p = (mean - bxe[0:tile, c0:c1]).astype(BF16)
        gate = _silu(_dot(h, win_ref[:, col(4) + c0:col(4) + c1]))
        yb = _dot(p, wg_ref[gi]) * sc_ref[:, c0:c1] * gate
        cat_ref[:, A_WIDTH + c0:A_WIDTH + c1] = yb.astype(BF16)

    o_ref[0] = x + _dot(cat_ref[...], wout_ref[...])


def _layer0(x, norm_0, w_in_0, a_v_norm_0, a_spatial_w_0, a_spatial_b_0, b_group_w_0, b_scale_0,
            w_out_0):
    bn, seq, d = x.shape
    tile = TILE_L0
    nt = seq // tile
    halo_blocks = seq // POOL_HALO
    win = w_in_0.astype(BF16)
    wout = w_out_0.astype(BF16)
    ws = a_spatial_w_0.astype(BF16)
    wg = b_group_w_0.astype(BF16)
    bsb = jnp.repeat(a_spatial_b_0.T, A_GROUP_DIM, axis=1)

    def const(shape):
        return pl.BlockSpec(shape, lambda b, j: (0,) * len(shape), pipeline_mode=pl.Buffered(1))

    per_halo = tile // POOL_HALO
    kern = functools.partial(_layer0_kernel, tile=tile, seq=seq)
    return pl.pallas_call(
        kern,
        grid=(bn, nt),
        in_specs=[
            pl.BlockSpec((1, tile, d), lambda b, j: (b, j, 0)),
            pl.BlockSpec((1, POOL_HALO, d), lambda b, j: (b, jnp.maximum(j * per_halo - 1, 0), 0)),
            pl.BlockSpec((1, POOL_HALO, d),
                         lambda b, j: (b, jnp.minimum((j + 1) * per_halo, halo_blocks - 1), 0)),
            const((1, d)),
            const(win.shape),
            const((1, A_WIDTH)),
            const(ws.shape),
            const(bsb.shape),
            const(wg.shape),
            const((1, B_WIDTH)),
            const(wout.shape),
        ],
        out_specs=pl.BlockSpec((1, tile, d), lambda b, j: (b, j, 0)),
        out_shape=jax.ShapeDtypeStruct(x.shape, F32),
        scratch_shapes=[
            pltpu.VMEM((tile + 2 * POOL_HALO, d), BF16),
            pltpu.VMEM((tile, A_WIDTH), BF16),
            pltpu.VMEM((tile, A_WIDTH + B_WIDTH), BF16),
        ],
        compiler_params=pltpu.CompilerParams(
            dimension_semantics=("arbitrary", "arbitrary"), vmem_limit_bytes=VMEM_LIMIT),
        name="layer0_mixer",
    )(x, x, x, norm_0.reshape(1, d), win, a_v_norm_0.reshape(1, A_WIDTH), ws, bsb, wg,
      b_scale_0.reshape(1, B_WIDTH), wout)


def _qkv_kernel(x_ref, n1_ref, w_ref, cost_ref, sint_ref, qt_ref, ka_ref, vt_ref, *, tile):
    h = _rms(x_ref[0], n1_ref[...]).astype(BF16)
    half = ROT_DIM // 2
    scale = HEAD_DIM ** -0.5 * LOG2_E

    qkv = _dot_nt(w_ref[...], h)
    ct = cost_ref[...]
    st = sint_ref[...]

    def rope_rows(r0):
        x1 = qkv[r0:r0 + half]
        x2 = qkv[r0 + half:r0 + ROT_DIM]
        return jnp.concatenate([x1 * ct - x2 * st, x2 * ct + x1 * st], axis=0)

    for hd in range(N_HEADS):
        r0 = hd * HEAD_DIM
        qt_ref[0, r0:r0 + ROT_DIM, :] = (rope_rows(r0) * scale).astype(BF16)
        qt_ref[0, r0 + ROT_DIM:r0 + HEAD_DIM, :] = (
            qkv[r0 + ROT_DIM:r0 + HEAD_DIM] * scale).astype(BF16)
    vt_ref[0] = qkv[C_WIDTH + KV_WIDTH:].astype(BF16)

    pad = jnp.zeros((LANES - HEAD_DIM, tile), F32)
    for kh in range(N_KV_HEADS):
        r0 = C_WIDTH + kh * HEAD_DIM
        kt = jnp.concatenate([rope_rows(r0), qkv[r0 + ROT_DIM:r0 + HEAD_DIM], pad], axis=0)
        ka_ref[0, :, kh * LANES:(kh + 1) * LANES] = kt.T.astype(BF16)


def _rope_tables(seq):
    inv = ROPE_THETA ** (-jnp.arange(0, ROT_DIM, 2, dtype=F32) / ROT_DIM)
    ang = jnp.arange(seq).astype(F32)[None, :] * inv[:, None]
    return jnp.cos(ang), jnp.sin(ang)


def _qkv(x, norm_1, w_qkv_t):
    bn, seq, d = x.shape
    tile = TILE_QKV
    nt = seq // tile
    cos_fm, sin_fm = _rope_tables(seq)
    kern = functools.partial(_qkv_kernel, tile=tile)
    tab_fm = pl.BlockSpec((ROT_DIM // 2, tile), lambda b, j: (0, j))

    def const(shape):
        return pl.BlockSpec(shape, lambda b, j: (0,) * len(shape), pipeline_mode=pl.Buffered(1))

    return pl.pallas_call(
        kern,
        grid=(bn, nt),
        in_specs=[
            pl.BlockSpec((1, tile, d), lambda b, j: (b, j, 0)),
            const((1, d)),
            const(w_qkv_t.shape),
            tab_fm, tab_fm,
        ],
        out_specs=[pl.BlockSpec((1, C_WIDTH, tile), lambda b, j: (b, 0, j)),
                   pl.BlockSpec((1, tile, KV_PAD_WIDTH), lambda b, j: (b, j, 0)),
                   pl.BlockSpec((1, KV_WIDTH, tile), lambda b, j: (b, 0, j))],
        out_shape=[jax.ShapeDtypeStruct((bn, C_WIDTH, seq), BF16),
                   jax.ShapeDtypeStruct((bn, seq, KV_PAD_WIDTH), BF16),
                   jax.ShapeDtypeStruct((bn, KV_WIDTH, seq), BF16)],
        compiler_params=pltpu.CompilerParams(
            dimension_semantics=("arbitrary", "arbitrary"), vmem_limit_bytes=VMEM_LIMIT),
        name="layer1_qkv",
    )(x, norm_1.reshape(1, d), w_qkv_t, cos_fm, sin_fm)


def _attn_kernel(sink_ref, x_ref, qt_ref, kap_ref, kat_ref, kan_ref, vtp_ref, vtt_ref, vtn_ref,
                 n1_ref, wgate_ref, wout_ref, fn_ref, o_ref,
                 s_ref, p_ref, sg_ref, y_ref, *, tile, seq):
    j = pl.program_id(1)
    nblk = tile // ATTN_BLOCK
    last_blk = seq // ATTN_BLOCK - 1
    h = _rms(x_ref[0], n1_ref[...]).astype(BF16)


    def k_band(i, kh):
        cols = slice(kh * LANES, (kh + 1) * LANES)
        lo = max(i - 1, 0) * ATTN_BLOCK
        hi = min(i + 2, nblk) * ATTN_BLOCK
        parts = [kat_ref[0, lo:hi, cols]]
        if i == 0:
            parts.insert(0, kap_ref[0, :, cols])
        if i == nblk - 1:
            parts.append(kan_ref[0, :, cols])
        return jnp.concatenate(parts, axis=0)

    def vt_band(i, kh):
        rows = slice(kh * HEAD_DIM, (kh + 1) * HEAD_DIM)
        lo = max(i - 1, 0) * ATTN_BLOCK
        hi = min(i + 2, nblk) * ATTN_BLOCK
        parts = [vtt_ref[0, rows, lo:hi]]
        if i == 0:
            parts.insert(0, vtp_ref[0, rows, :])
        if i == nblk - 1:
            parts.append(vtn_ref[0, rows, :])
        return parts
    diff = (lax.broadcasted_iota(jnp.int32, (ATTN_BLOCK, ATTN_BLOCK), 0)
            - lax.broadcasted_iota(jnp.int32, (ATTN_BLOCK, ATTN_BLOCK), 1))
    never = 2 * ATTN_BLOCK
    ones_rows = jnp.ones((HEAD_DIM, BAND), BF16)

    def scores(u, half):
        i, kh = divmod(u, N_KV_HEADS)
        slot = u % 2
        c0 = i * ATTN_BLOCK
        h0 = kh * GQA_GROUP + half * HEADS_PER_HALF
        qg = jnp.concatenate(
            [qt_ref[0, (h0 + g) * HEAD_DIM:(h0 + g + 1) * HEAD_DIM, c0:c0 + ATTN_BLOCK]
             for g in range(HEADS_PER_HALF)], axis=1)
        rq = jnp.concatenate([qg, jnp.zeros((LANES - HEAD_DIM, HALF_Q), BF16)], axis=0)
        ka = k_band(i, kh)
        s_ref[slot, :, half * HALF_Q:(half + 1) * HALF_Q] = _dot(ka, rq)

    def softmax_head(u, g):
        i, kh = divmod(u, N_KV_HEADS)
        slot = u % 2
        gblk = j * nblk + i
        ok_prev = diff >= jnp.where(gblk > 0, 0, never)
        ok_next = diff <= jnp.where(gblk < last_blk, 0, -never)
        cs = slice(g * ATTN_BLOCK, (g + 1) * ATTN_BLOCK)
        sink = sink_ref[kh * GQA_GROUP + g] * LOG2_E
        s0 = jnp.where(ok_prev, s_ref[slot, 0:ATTN_BLOCK, cs], NEG_INF)
        s1 = s_ref[slot, ATTN_BLOCK:2 * ATTN_BLOCK, cs]
        s2 = jnp.where(ok_next, s_ref[slot, 2 * ATTN_BLOCK:, cs], NEG_INF)
        m = jnp.max(jnp.maximum(jnp.maximum(s0, s1), s2), axis=0, keepdims=True)
        m = jnp.maximum(m, sink)
        p0 = jnp.exp2(s0 - m)
        p1 = jnp.exp2(s1 - m)
        p2 = jnp.exp2(s2 - m)
        p_ref[slot, 0:ATTN_BLOCK, cs] = p0.astype(BF16)
        p_ref[slot, ATTN_BLOCK:2 * ATTN_BLOCK, cs] = p1.astype(BF16)
        p_ref[slot, 2 * ATTN_BLOCK:, cs] = p2.astype(BF16)
        return jnp.exp2(sink - m)

    def weighted(u, half, sink_share):
        i, kh = divmod(u, N_KV_HEADS)
        slot = u % 2
        c0 = i * ATTN_BLOCK
        vt = jnp.concatenate([jnp.concatenate(vt_band(i, kh), axis=1), ones_rows],
                             axis=0)
        ot = _dot(vt, p_ref[slot, :, half * HALF_Q:(half + 1) * HALF_Q])
        parts = []
        for g in range(HEADS_PER_HALF):
            cq = slice(g * ATTN_BLOCK, (g + 1) * ATTN_BLOCK)
            l = ot[HEAD_DIM:HEAD_DIM + 1, cq] + sink_share[g]
            parts.append(ot[0:HEAD_DIM, cq] * (1.0 / l))
        yt = jnp.concatenate(parts, axis=0)
        f0 = (kh * GQA_GROUP + half * HEADS_PER_HALF) * HEAD_DIM
        cs = slice(f0, f0 + HEADS_PER_HALF * HEAD_DIM)
        y = yt.T * sg_ref[c0:c0 + ATTN_BLOCK, cs]
        y_ref[c0:c0 + ATTN_BLOCK, cs] = y.astype(BF16)

    def gate_chunk(kh):
        cs = slice(kh * GQA_GROUP * HEAD_DIM, (kh + 1) * GQA_GROUP * HEAD_DIM)
        sg_ref[:, cs] = _silu(_dot(h, wgate_ref[:, cs]))

    def finish_rows(r0, n):
        rows = slice(r0, r0 + n)
        out = x_ref[0, rows, :] + _dot(y_ref[rows, :], wout_ref[...])
        o_ref[0, rows, :] = _rms(out, fn_ref[...])

    units = nblk * N_KV_HEADS
    inv = {}
    for step in range(units + 2):
        sc = step if step < units else None
        sm = step - 1 if 1 <= step <= units else None
        wt = step - 2 if step >= 2 else None
        if step < N_KV_HEADS:
            gate_chunk(step)
        for half in range(2):
            if sc is not None:
                scores(sc, half)
            if sm is not None:
                inv[sm, 2 * half] = softmax_head(sm, 2 * half)
            if wt is not None:
                weighted(wt, half, [inv.pop((wt, half * HEADS_PER_HALF + g))
                                    for g in range(HEADS_PER_HALF)])
            if sm is not None:
                inv[sm, 2 * half + 1] = softmax_head(sm, 2 * half + 1)
        if wt is not None and (wt + 1) % (units // 2) == 0:
            finish_rows((wt + 1 - units // 2) // N_KV_HEADS * ATTN_BLOCK, tile // 2)


def _attention(x, qt, ka, vt, norm_1, w_gate, sink_1, w_out_1, final_norm):
    bn, seq, d = x.shape
    tile = TILE_ATTN
    nt = seq // tile
    per = tile // ATTN_BLOCK
    nb = seq // ATTN_BLOCK

    def const(shape):
        return pl.BlockSpec(shape, lambda b, j: (0,) * len(shape), pipeline_mode=pl.Buffered(1))

    def prev_blk(j):
        return jnp.maximum(j * per - 1, 0)

    def next_blk(j):
        return jnp.minimum((j + 1) * per, nb - 1)

    kern = functools.partial(_attn_kernel, tile=tile, seq=seq)
    return pl.pallas_call(
        kern,
        grid=(bn, nt),
        in_specs=[
            pl.BlockSpec(memory_space=pltpu.SMEM),
            pl.BlockSpec((1, tile, d), lambda b, j: (b, j, 0)),
            pl.BlockSpec((1, C_WIDTH, tile), lambda b, j: (b, 0, j)),
            pl.BlockSpec((1, ATTN_BLOCK, KV_PAD_WIDTH), lambda b, j: (b, prev_blk(j), 0)),
            pl.BlockSpec((1, tile, KV_PAD_WIDTH), lambda b, j: (b, j, 0)),
            pl.BlockSpec((1, ATTN_BLOCK, KV_PAD_WIDTH), lambda b, j: (b, next_blk(j), 0)),
            pl.BlockSpec((1, KV_WIDTH, ATTN_BLOCK), lambda b, j: (b, 0, prev_blk(j))),
            pl.BlockSpec((1, KV_WIDTH, tile), lambda b, j: (b, 0, j)),
            pl.BlockSpec((1, KV_WIDTH, ATTN_BLOCK), lambda b, j: (b, 0, next_blk(j))),
            const((1, d)),
            const(w_gate.shape),
            const(w_out_1.shape),
            const((1, d)),
        ],
        out_specs=pl.BlockSpec((1, tile, d), lambda b, j: (b, j, 0)),
        out_shape=jax.ShapeDtypeStruct(x.shape, F32),
        scratch_shapes=[
            pltpu.VMEM((2, BAND, GROUP_Q), F32),
            pltpu.VMEM((2, BAND, GROUP_Q), BF16),
            pltpu.VMEM((tile, C_WIDTH), F32),
            pltpu.VMEM((tile, C_WIDTH), BF16),
        ],
        compiler_params=pltpu.CompilerParams(
            dimension_semantics=("arbitrary", "arbitrary"), vmem_limit_bytes=VMEM_LIMIT),
        name="layer1_attention",
    )(sink_1, x, qt, ka, ka, ka, vt, vt, vt,
      norm_1.reshape(1, d), w_gate, w_out_1, final_norm.reshape(1, d))


def kernel(x, norm_0, w_in_0, a_v_norm_0, a_spatial_w_0, a_spatial_b_0, b_group_w_0, b_scale_0,
           w_out_0, norm_1, w_in_1, sink_1, w_out_1, final_norm):
    x1 = _layer0(x, norm_0, w_in_0, a_v_norm_0, a_spatial_w_0, a_spatial_b_0, b_group_w_0,
                 b_scale_0, w_out_0)
    g0 = C_WIDTH + 2 * KV_WIDTH
    w_qkv_t = w_in_1[:, :g0].T.astype(BF16)
    w_gate = w_in_1[:, g0:].astype(BF16)
    qt, ka, vt = _qkv(x1, norm_1, w_qkv_t)
    return _attention(x1, qt, ka, vt, norm_1, w_gate, sink_1, w_out_1.astype(BF16), final_norm)
```

```python
import functools
import math

import jax
import jax.numpy as jnp
from jax import lax
from jax.experimental import pallas as pl
from jax.experimental.pallas import tpu as pltpu

F32 = jnp.float32
BF16 = jnp.bfloat16

D_MODEL = 1024
EPS = 1e-6
NEG_INF = -1e30
LOG2_E = math.log2(math.e)
CHUNK = 128
A_GROUPS = 4
A_WIDTH = D_MODEL
A_GROUP_DIM = A_WIDTH // A_GROUPS
B_WIDTH = D_MODEL
POOL_WINDOWS = (2, 4, 8, 16)
B_GROUP_DIM = B_WIDTH // len(POOL_WINDOWS)
POOL_HALO = max(POOL_WINDOWS) // 2
N_HEADS = 16
N_KV_HEADS = 4
HEAD_DIM = 64
GQA_GROUP = N_HEADS // N_KV_HEADS
C_WIDTH = N_HEADS * HEAD_DIM
KV_WIDTH = N_KV_HEADS * HEAD_DIM
WINDOW = 128
ATTN_BLOCK = 128
BAND = 3 * ATTN_BLOCK
ROPE_THETA = 500000.0
ROT_DIM = HEAD_DIM // 4
LANES = 128
KV_PAD_WIDTH = N_KV_HEADS * LANES
GROUP_Q = GQA_GROUP * ATTN_BLOCK
HEADS_PER_HALF = GQA_GROUP // 2
HALF_Q = HEADS_PER_HALF * ATTN_BLOCK

QKV_COLS = C_WIDTH + 2 * KV_WIDTH
W1_CHUNK_ROWS = 128
W1_CHUNKS = D_MODEL // W1_CHUNK_ROWS

TILE_L0 = 512
TILE_QKV = 512
TILE_ATTN = 512
VMEM_LIMIT = 56 * 1024 * 1024


def _rms(x, g):
    return x * lax.rsqrt(jnp.mean(x * x, axis=-1, keepdims=True) + EPS) * g


def _silu(x):
    return x / (1.0 + jnp.exp(-x))


def _dot(a, b):
    return jnp.dot(a, b, preferred_element_type=F32)


def _dot_nt(a, b):
    return lax.dot_general(a, b, (((1,), (1,)), ((), ())), preferred_element_type=F32)


def _layer0_kernel(x_ref, xp_ref, xn_ref, n0_ref, win_ref, gv_ref, ws_ref, bsb_ref, wg_ref,
                   sc_ref, wout_ref, w1_ref, wo1_ref,
                   o_ref, wqkvt_ref, wgate_ref, wo1b_ref,
                   hext_ref, v_ref, cat_ref, *, tile, seq):
    j = pl.program_id(1)
    nj = pl.num_programs(1)

    @pl.when(pl.program_id(0) * nj + j < W1_CHUNKS)
    def _():
        w1 = w1_ref[...]
        wqkvt_ref[...] = w1[:, :QKV_COLS].T.astype(BF16)
        wgate_ref[...] = w1[:, QKV_COLS:].astype(BF16)
        wo1b_ref[...] = wo1_ref[...].astype(BF16)

    n0 = n0_ref[...]
    x = x_ref[0]
    h = _rms(x, n0).astype(BF16)
    hext_ref[0:tile, :] = h
    hp = jnp.where(j > 0, _rms(xp_ref[0], n0), 0.0)
    hn = jnp.where(j < nj - 1, _rms(xn_ref[0], n0), 0.0)
    hext_ref[tile:tile + 2 * POOL_HALO, :] = jnp.concatenate([hp, hn], axis=0).astype(BF16)

    def col(k):
        return k * D_MODEL

    a_v = jax.nn.gelu(_dot(h, win_ref[:, col(1):col(2)]))
    v_ref[...] = _rms(a_v, gv_ref[...]).astype(BF16)
    for g in range(A_GROUPS):
        c0 = g * A_GROUP_DIM
        c1 = c0 + A_GROUP_DIM
        u = jax.nn.gelu(_dot(h, win_ref[:, col(0) + c0:col(0) + c1]))
        gate = _silu(_dot(h, win_ref[:, col(2) + c0:col(2) + c1]))
        for c in range(tile // CHUNK):
            r0 = c * CHUNK
            r1 = r0 + CHUNK
            mixed = _dot(ws_ref[g], v_ref[r0:r1, c0:c1]) + bsb_ref[:, c0:c1]
            cat_ref[r0:r1, c0:c1] = (u[r0:r1] * mixed * gate[r0:r1]).astype(BF16)

    bxe = _dot(hext_ref[...], win_ref[:, col(3):col(4)])
    n_ext = tile + 2 * POOL_HALO
    edge_row = lax.broadcasted_iota(jnp.int32, (POOL_HALO, 1), 0)
    for gi, w in enumerate(POOL_WINDOWS):
        c0 = gi * B_GROUP_DIM
        c1 = c0 + B_GROUP_DIM
        half = w // 2
        e = jnp.concatenate([bxe[tile:tile + POOL_HALO, c0:c1], bxe[0:tile, c0:c1],
                             bxe[tile + POOL_HALO:, c0:c1]], axis=0)
        a = e
        span = 1
        while span < half:
            a = a + pltpu.roll(a, n_ext - span, 0)
            span *= 2
        win_sum = (a + pltpu.roll(a, half, 0))[POOL_HALO:POOL_HALO + tile]

        def edge_mean(slab, row0, half=half):
            t = j * tile + row0 + edge_row
            cnt = (jnp.minimum(t + half, seq) - jnp.maximum(t - half, 0)).astype(F32)
            return slab / cnt

        mean = jnp.concatenate(
            [edge_mean(win_sum[0:POOL_HALO], 0),
             win_sum[POOL_HALO:tile - POOL_HALO] * (1.0 / w),
             edge_mean(win_sum[tile - POOL_HALO:], tile - POOL_HALO)], axis=0)
        p = (mean - bxe[0:tile, c0:c1]).astype(BF16)
        gate = _silu(_dot(h, win_ref[:, col(4) + c0:col(4) + c1]))
        yb = _dot(p, wg_ref[gi]) * sc_ref[:, c0:c1] * gate
        cat_ref[:, A_WIDTH + c0:A_WIDTH + c1] = yb.astype(BF16)

    o_ref[0] = x + _dot(cat_ref[...], wout_ref[...])


def _layer0(x, norm_0, w_in_0, a_v_norm_0, a_spatial_w_0, a_spatial_b_0, b_group_w_0, b_scale_0,
            w_out_0, w_in_1, w_out_1):
    bn, seq, d = x.shape
    tile = TILE_L0
    nt = seq // tile
    assert bn * nt >= W1_CHUNKS

    def w1_chunk(b, j):
        return jnp.minimum(b * nt + j, W1_CHUNKS - 1)
    halo_blocks = seq // POOL_HALO
    win = w_in_0.astype(BF16)
    wout = w_out_0.astype(BF16)
    ws = a_spatial_w_0.astype(BF16)
    wg = b_group_w_0.astype(BF16)
    bsb = jnp.repeat(a_spatial_b_0.T, A_GROUP_DIM, axis=1)

    def const(shape):
        return pl.BlockSpec(shape, lambda b, j: (0,) * len(shape), pipeline_mode=pl.Buffered(1))

    per_halo = tile // POOL_HALO
    kern = functools.partial(_layer0_kernel, tile=tile, seq=seq)
    return pl.pallas_call(
        kern,
        grid=(bn, nt),
        in_specs=[
            pl.BlockSpec((1, tile, d), lambda b, j: (b, j, 0)),
            pl.BlockSpec((1, POOL_HALO, d), lambda b, j: (b, jnp.maximum(j * per_halo - 1, 0), 0)),
            pl.BlockSpec((1, POOL_HALO, d),
                         lambda b, j: (b, jnp.minimum((j + 1) * per_halo, halo_blocks - 1), 0)),
            const((1, d)),
            const(win.shape),
            const((1, A_WIDTH)),
            const(ws.shape),
            const(bsb.shape),
            const(wg.shape),
            const((1, B_WIDTH)),
            const(wout.shape),
            pl.BlockSpec((W1_CHUNK_ROWS, w_in_1.shape[1]), lambda b, j: (w1_chunk(b, j), 0)),
            pl.BlockSpec((W1_CHUNK_ROWS, d), lambda b, j: (w1_chunk(b, j), 0)),
        ],
        out_specs=[
            pl.BlockSpec((1, tile, d), lambda b, j: (b, j, 0)),
            pl.BlockSpec((QKV_COLS, W1_CHUNK_ROWS), lambda b, j: (0, w1_chunk(b, j))),
            pl.BlockSpec((W1_CHUNK_ROWS, C_WIDTH), lambda b, j: (w1_chunk(b, j), 0)),
            pl.BlockSpec((W1_CHUNK_ROWS, d), lambda b, j: (w1_chunk(b, j), 0)),
        ],
        out_shape=[
            jax.ShapeDtypeStruct(x.shape, F32),
            jax.ShapeDtypeStruct((QKV_COLS, d), BF16),
            jax.ShapeDtypeStruct((d, C_WIDTH), BF16),
            jax.ShapeDtypeStruct((C_WIDTH, d), BF16),
        ],
        scratch_shapes=[
            pltpu.VMEM((tile + 2 * POOL_HALO, d), BF16),
            pltpu.VMEM((tile, A_WIDTH), BF16),
            pltpu.VMEM((tile, A_WIDTH + B_WIDTH), BF16),
        ],
        compiler_params=pltpu.CompilerParams(
            dimension_semantics=("arbitrary", "arbitrary"), vmem_limit_bytes=VMEM_LIMIT),
        name="layer0_mixer",
    )(x, x, x, norm_0.reshape(1, d), win, a_v_norm_0.reshape(1, A_WIDTH), ws, bsb, wg,
      b_scale_0.reshape(1, B_WIDTH), wout, w_in_1, w_out_1)


def _qkv_kernel(x_ref, n1_ref, w_ref, cost_ref, sint_ref, qt_ref, ka_ref, vt_ref, *, tile):
    h = _rms(x_ref[0], n1_ref[...]).astype(BF16)
    half = ROT_DIM // 2
    scale = HEAD_DIM ** -0.5 * LOG2_E

    qkv = _dot_nt(w_ref[...], h)
    ct = cost_ref[...]
    st = sint_ref[...]

    def rope_rows(r0):
        x1 = qkv[r0:r0 + half]
        x2 = qkv[r0 + half:r0 + ROT_DIM]
        return jnp.concatenate([x1 * ct - x2 * st, x2 * ct + x1 * st], axis=0)

    for hd in range(N_HEADS):
        r0 = hd * HEAD_DIM
        qt_ref[0, r0:r0 + ROT_DIM, :] = (rope_rows(r0) * scale).astype(BF16)
        qt_ref[0, r0 + ROT_DIM:r0 + HEAD_DIM, :] = (
            qkv[r0 + ROT_DIM:r0 + HEAD_DIM] * scale).astype(BF16)
    vt_ref[0] = qkv[C_WIDTH + KV_WIDTH:].astype(BF16)

    pad = jnp.zeros((LANES - HEAD_DIM, tile), F32)
    for kh in range(N_KV_HEADS):
        r0 = C_WIDTH + kh * HEAD_DIM
        kt = jnp.concatenate([rope_rows(r0), qkv[r0 + ROT_DIM:r0 + HEAD_DIM], pad], axis=0)
        ka_ref[0, :, kh * LANES:(kh + 1) * LANES] = kt.T.astype(BF16)


def _rope_tables(seq):
    inv = ROPE_THETA ** (-jnp.arange(0, ROT_DIM, 2, dtype=F32) / ROT_DIM)
    ang = jnp.arange(seq).astype(F32)[None, :] * inv[:, None]
    return jnp.cos(ang), jnp.sin(ang)


def _qkv(x, norm_1, w_qkv_t):
    bn, seq, d = x.shape
    tile = TILE_QKV
    nt = seq // tile
    cos_fm, sin_fm = _rope_tables(seq)
    kern = functools.partial(_qkv_kernel, tile=tile)
    tab_fm = pl.BlockSpec((ROT_DIM // 2, tile), lambda b, j: (0, j))

    def const(shape):
        return pl.BlockSpec(shape, lambda b, j: (0,) * len(shape), pipeline_mode=pl.Buffered(1))

    return pl.pallas_call(
        kern,
        grid=(bn, nt),
        in_specs=[
            pl.BlockSpec((1, tile, d), lambda b, j: (b, j, 0)),
            const((1, d)),
            const(w_qkv_t.shape),
            tab_fm, tab_fm,
        ],
        out_specs=[pl.BlockSpec((1, C_WIDTH, tile), lambda b, j: (b, 0, j)),
                   pl.BlockSpec((1, tile, KV_PAD_WIDTH), lambda b, j: (b, j, 0)),
                   pl.BlockSpec((1, KV_WIDTH, tile), lambda b, j: (b, 0, j))],
        out_shape=[jax.ShapeDtypeStruct((bn, C_WIDTH, seq), BF16),
                   jax.ShapeDtypeStruct((bn, seq, KV_PAD_WIDTH), BF16),
                   jax.ShapeDtypeStruct((bn, KV_WIDTH, seq), BF16)],
        compiler_params=pltpu.CompilerParams(
            dimension_semantics=("arbitrary", "arbitrary"), vmem_limit_bytes=VMEM_LIMIT),
        name="layer1_qkv",
    )(x, norm_1.reshape(1, d), w_qkv_t, cos_fm, sin_fm)


def _attn_kernel(sink_ref, x_ref, qt_ref, kap_ref, kat_ref, kan_ref, vtp_ref, vtt_ref, vtn_ref,
                 n1_ref, wgate_ref, wout_ref, fn_ref, o_ref,
                 s_ref, p_ref, sg_ref, y_ref, *, tile, seq):
    j = pl.program_id(1)
    nblk = tile // ATTN_BLOCK
    last_blk = seq // ATTN_BLOCK - 1
    h = _rms(x_ref[0], n1_ref[...]).astype(BF16)


    def k_band(i, kh):
        cols = slice(kh * LANES, (kh + 1) * LANES)
        lo = max(i - 1, 0) * ATTN_BLOCK
        hi = min(i + 2, nblk) * ATTN_BLOCK
        parts = [kat_ref[0, lo:hi, cols]]
        if i == 0:
            parts.insert(0, kap_ref[0, :, cols])
        if i == nblk - 1:
            parts.append(kan_ref[0, :, cols])
        return jnp.concatenate(parts, axis=0)

    def vt_band(i, kh):
        rows = slice(kh * HEAD_DIM, (kh + 1) * HEAD_DIM)
        lo = max(i - 1, 0) * ATTN_BLOCK
        hi = min(i + 2, nblk) * ATTN_BLOCK
        parts = [vtt_ref[0, rows, lo:hi]]
        if i == 0:
            parts.insert(0, vtp_ref[0, rows, :])
        if i == nblk - 1:
            parts.append(vtn_ref[0, rows, :])
        return parts
    diff = (lax.broadcasted_iota(jnp.int32, (ATTN_BLOCK, ATTN_BLOCK), 0)
            - lax.broadcasted_iota(jnp.int32, (ATTN_BLOCK, ATTN_BLOCK), 1))
    never = 2 * ATTN_BLOCK
    ones_rows = jnp.ones((HEAD_DIM, BAND), BF16)

    def scores(u, half):
        i, kh = divmod(u, N_KV_HEADS)
        slot = u % 2
        c0 = i * ATTN_BLOCK
        h0 = kh * GQA_GROUP + half * HEADS_PER_HALF
        qg = jnp.concatenate(
            [qt_ref[0, (h0 + g) * HEAD_DIM:(h0 + g + 1) * HEAD_DIM, c0:c0 + ATTN_BLOCK]
             for g in range(HEADS_PER_HALF)], axis=1)
        rq = jnp.concatenate([qg, jnp.zeros((LANES - HEAD_DIM, HALF_Q), BF16)], axis=0)
        ka = k_band(i, kh)
        s_ref[slot, :, half * HALF_Q:(half + 1) * HALF_Q] = _dot(ka, rq)

    def softmax_head(u, g):
        i, kh = divmod(u, N_KV_HEADS)
        slot = u % 2
        gblk = j * nblk + i
        ok_prev = diff >= jnp.where(gblk > 0, 0, never)
        ok_next = diff <= jnp.where(gblk < last_blk, 0, -never)
        cs = slice(g * ATTN_BLOCK, (g + 1) * ATTN_BLOCK)
        sink = sink_ref[kh * GQA_GROUP + g] * LOG2_E
        s0 = jnp.where(ok_prev, s_ref[slot, 0:ATTN_BLOCK, cs], NEG_INF)
        s1 = s_ref[slot, ATTN_BLOCK:2 * ATTN_BLOCK, cs]
        s2 = jnp.where(ok_next, s_ref[slot, 2 * ATTN_BLOCK:, cs], NEG_INF)
        m = jnp.max(jnp.maximum(jnp.maximum(s0, s1), s2), axis=0, keepdims=True)
        m = jnp.maximum(m, sink)
        p0 = jnp.exp2(s0 - m)
        p1 = jnp.exp2(s1 - m)
        p2 = jnp.exp2(s2 - m)
        p_ref[slot, 0:ATTN_BLOCK, cs] = p0.astype(BF16)
        p_ref[slot, ATTN_BLOCK:2 * ATTN_BLOCK, cs] = p1.astype(BF16)
        p_ref[slot, 2 * ATTN_BLOCK:, cs] = p2.astype(BF16)
        return jnp.exp2(sink - m)

    def weighted(u, half, sink_share):
        i, kh = divmod(u, N_KV_HEADS)
        slot = u % 2
        c0 = i * ATTN_BLOCK
        vt = jnp.concatenate([jnp.concatenate(vt_band(i, kh), axis=1), ones_rows],
                             axis=0)
        ot = _dot(vt, p_ref[slot, :, half * HALF_Q:(half + 1) * HALF_Q])
        parts = []
        for g in range(HEADS_PER_HALF):
            cq = slice(g * ATTN_BLOCK, (g + 1) * ATTN_BLOCK)
            l = ot[HEAD_DIM:HEAD_DIM + 1, cq] + sink_share[g]
            parts.append(ot[0:HEAD_DIM, cq] * (1.0 / l))
        yt = jnp.concatenate(parts, axis=0)
        f0 = (kh * GQA_GROUP + half * HEADS_PER_HALF) * HEAD_DIM
        cs = slice(f0, f0 + HEADS_PER_HALF * HEAD_DIM)
        y = yt.T * sg_ref[c0:c0 + ATTN_BLOCK, cs]
        y_ref[c0:c0 + ATTN_BLOCK, cs] = y.astype(BF16)

    def gate_chunk(kh):
        cs = slice(kh * GQA_GROUP * HEAD_DIM, (kh + 1) * GQA_GROUP * HEAD_DIM)
        sg_ref[:, cs] = _silu(_dot(h, wgate_ref[:, cs]))

    def finish_rows(r0, n):
        rows = slice(r0, r0 + n)
        out = x_ref[0, rows, :] + _dot(y_ref[rows, :], wout_ref[...])
        o_ref[0, rows, :] = _rms(out, fn_ref[...])

    units = nblk * N_KV_HEADS
    inv = {}
    for step in range(units + 2):
        sc = step if step < units else None
        sm = step - 1 if 1 <= step <= units else None
        wt = step - 2 if step >= 2 else None
        if step < N_KV_HEADS:
            gate_chunk(step)
        for half in range(2):
            if sc is not None:
                scores(sc, half)
            if sm is not None:
                inv[sm, 2 * half] = softmax_head(sm, 2 * half)
            if wt is not None:
                weighted(wt, half, [inv.pop((wt, half * HEADS_PER_HALF + g))
                                    for g in range(HEADS_PER_HALF)])
            if sm is not None:
                inv[sm, 2 * half + 1] = softmax_head(sm, 2 * half + 1)
        if wt is not None and (wt + 1) % (units // 2) == 0:
            finish_rows((wt + 1 - units // 2) // N_KV_HEADS * ATTN_BLOCK, tile // 2)


def _attention(x, qt, ka, vt, norm_1, w_gate, sink_1, w_out_1, final_norm):
    bn, seq, d = x.shape
    tile = TILE_ATTN
    nt = seq // tile
    per = tile // ATTN_BLOCK
    nb = seq // ATTN_BLOCK

    def const(shape):
        return pl.BlockSpec(shape, lambda b, j: (0,) * len(shape), pipeline_mode=pl.Buffered(1))

    def prev_blk(j):
        return jnp.maximum(j * per - 1, 0)

    def next_blk(j):
        return jnp.minimum((j + 1) * per, nb - 1)

    kern = functools.partial(_attn_kernel, tile=tile, seq=seq)
    return pl.pallas_call(
        kern,
        grid=(bn, nt),
        in_specs=[
            pl.BlockSpec(memory_space=pltpu.SMEM),
            pl.BlockSpec((1, tile, d), lambda b, j: (b, j, 0)),
            pl.BlockSpec((1, C_WIDTH, tile), lambda b, j: (b, 0, j)),
            pl.BlockSpec((1, ATTN_BLOCK, KV_PAD_WIDTH), lambda b, j: (b, prev_blk(j), 0)),
            pl.BlockSpec((1, tile, KV_PAD_WIDTH), lambda b, j: (b, j, 0)),
            pl.BlockSpec((1, ATTN_BLOCK, KV_PAD_WIDTH), lambda b, j: (b, next_blk(j), 0)),
            pl.BlockSpec((1, KV_WIDTH, ATTN_BLOCK), lambda b, j: (b, 0, prev_blk(j))),
            pl.BlockSpec((1, KV_WIDTH, tile), lambda b, j: (b, 0, j)),
            pl.BlockSpec((1, KV_WIDTH, ATTN_BLOCK), lambda b, j: (b, 0, next_blk(j))),
            const((1, d)),
            const(w_gate.shape),
            const(w_out_1.shape),
            const((1, d)),
        ],
        out_specs=pl.BlockSpec((1, tile, d), lambda b, j: (b, j, 0)),
        out_shape=jax.ShapeDtypeStruct(x.shape, F32),
        scratch_shapes=[
            pltpu.VMEM((2, BAND, GROUP_Q), F32),
            pltpu.VMEM((2, BAND, GROUP_Q), BF16),
            pltpu.VMEM((tile, C_WIDTH), F32),
            pltpu.VMEM((tile, C_WIDTH), BF16),
        ],
        compiler_params=pltpu.CompilerParams(
            dimension_semantics=("arbitrary", "arbitrary"), vmem_limit_bytes=VMEM_LIMIT),
        name="layer1_attention",
    )(sink_1, x, qt, ka, ka, ka, vt, vt, vt,
      norm_1.reshape(1, d), w_gate, w_out_1, final_norm.reshape(1, d))


def kernel(x, norm_0, w_in_0, a_v_norm_0, a_spatial_w_0, a_spatial_b_0, b_group_w_0, b_scale_0,
           w_out_0, norm_1, w_in_1, sink_1, w_out_1, final_norm):
    x1, w_qkv_t, w_gate, w_out_1b = _layer0(
        x, norm_0, w_in_0, a_v_norm_0, a_spatial_w_0, a_spatial_b_0, b_group_w_0, b_scale_0,
        w_out_0, w_in_1, w_out_1)
    qt, ka, vt = _qkv(x1, norm_1, w_qkv_t)
    return _attention(x1, qt, ka, vt, norm_1, w_gate, sink_1, w_out_1b, final_norm)
```

```python
import functools
import math

import jax
import jax.numpy as jnp
import numpy as np
from jax import lax
from jax.experimental import pallas as pl
from jax.experimental.pallas import tpu as pltpu

F32 = jnp.float32
BF16 = jnp.bfloat16

D_MODEL = 1024
EPS = 1e-6
NEG_INF = -1e30
LOG2_E = math.log2(math.e)
CHUNK = 128
A_GROUPS = 4
A_WIDTH = D_MODEL
A_GROUP_DIM = A_WIDTH // A_GROUPS
B_WIDTH = D_MODEL
POOL_WINDOWS = (2, 4, 8, 16)
B_GROUP_DIM = B_WIDTH // len(POOL_WINDOWS)
POOL_HALO = max(POOL_WINDOWS) // 2
N_HEADS = 16
N_KV_HEADS = 4
HEAD_DIM = 64
GQA_GROUP = N_HEADS // N_KV_HEADS
C_WIDTH = N_HEADS * HEAD_DIM
KV_WIDTH = N_KV_HEADS * HEAD_DIM
WINDOW = 128
ATTN_BLOCK = 128
BAND = 3 * ATTN_BLOCK
ROPE_THETA = 500000.0
ROT_DIM = HEAD_DIM // 4
LANES = 128
KV_PAD_WIDTH = N_KV_HEADS * LANES
GROUP_Q = GQA_GROUP * ATTN_BLOCK
HEADS_PER_HALF = GQA_GROUP // 2
HALF_Q = HEADS_PER_HALF * ATTN_BLOCK

QKV_COLS = C_WIDTH + 2 * KV_WIDTH
W1_CHUNK_ROWS = 128
W1_CHUNKS = D_MODEL // W1_CHUNK_ROWS

TILE_L0 = 512
TILE_QKV = 512
TILE_ATTN = 512
VMEM_LIMIT = 56 * 1024 * 1024


def _rms(x, g):
    return x * lax.rsqrt(jnp.mean(x * x, axis=-1, keepdims=True) + EPS) * g


def _silu(x):
    return x / (1.0 + jnp.exp(-x))


def _dot(a, b):
    return jnp.dot(a, b, preferred_element_type=F32)


def _dot_nt(a, b):
    return lax.dot_general(a, b, (((1,), (1,)), ((), ())), preferred_element_type=F32)


def _layer0_kernel(x_ref, xp_ref, xn_ref, n0_ref, win_ref, gv_ref, ws_ref, bsb_ref, wg_ref,
                   sc_ref, wout_ref, w1_ref, wo1_ref,
                   o_ref, wqkvt_ref, wgate_ref, wo1b_ref,
                   hext_ref, v_ref, cat_ref, *, tile, seq):
    j = pl.program_id(1)
    nj = pl.num_programs(1)

    @pl.when(pl.program_id(0) * nj + j < W1_CHUNKS)
    def _():
        w1 = w1_ref[...]
        wqkvt_ref[...] = w1[:, :QKV_COLS].T.astype(BF16)
        wgate_ref[...] = w1[:, QKV_COLS:].astype(BF16)
        wo1b_ref[...] = wo1_ref[...].astype(BF16)

    n0 = n0_ref[...]
    x = x_ref[0]
    h = _rms(x, n0).astype(BF16)
    hext_ref[0:tile, :] = h
    hp = jnp.where(j > 0, _rms(xp_ref[0], n0), 0.0)
    hn = jnp.where(j < nj - 1, _rms(xn_ref[0], n0), 0.0)
    hext_ref[tile:tile + 2 * POOL_HALO, :] = jnp.concatenate([hp, hn], axis=0).astype(BF16)

    def col(k):
        return k * D_MODEL

    a_v = jax.nn.gelu(_dot(h, win_ref[:, col(1):col(2)]))
    v_ref[...] = _rms(a_v, gv_ref[...]).astype(BF16)
    for g in range(A_GROUPS):
        c0 = g * A_GROUP_DIM
        c1 = c0 + A_GROUP_DIM
        u = jax.nn.gelu(_dot(h, win_ref[:, col(0) + c0:col(0) + c1]))
        gate = _silu(_dot(h, win_ref[:, col(2) + c0:col(2) + c1]))
        ws_g = ws_ref[g].astype(BF16)
        for c in range(tile // CHUNK):
            r0 = c * CHUNK
            r1 = r0 + CHUNK
            mixed = _dot(ws_g, v_ref[r0:r1, c0:c1]) + bsb_ref[:, c0:c1]
            cat_ref[r0:r1, c0:c1] = (u[r0:r1] * mixed * gate[r0:r1]).astype(BF16)

    bxe = _dot(hext_ref[...], win_ref[:, col(3):col(4)])
    n_ext = tile + 2 * POOL_HALO
    edge_row = lax.broadcasted_iota(jnp.int32, (POOL_HALO, 1), 0)
    for gi, w in enumerate(POOL_WINDOWS):
        c0 = gi * B_GROUP_DIM
        c1 = c0 + B_GROUP_DIM
        half = w // 2
        e = jnp.concatenate([bxe[tile:tile + POOL_HALO, c0:c1], bxe[0:tile, c0:c1],
                             bxe[tile + POOL_HALO:, c0:c1]], axis=0)
        a = e
        span = 1
        while span < half:
            a = a + pltpu.roll(a, n_ext - span, 0)
            span *= 2
        win_sum = (a + pltpu.roll(a, half, 0))[POOL_HALO:POOL_HALO + tile]

        def edge_mean(slab, row0, half=half):
            t = j * tile + row0 + edge_row
            cnt = (jnp.minimum(t + half, seq) - jnp.maximum(t - half, 0)).astype(F32)
            return slab / cnt

        mean = jnp.concatenate(
            [edge_mean(win_sum[0:POOL_HALO], 0),
             win_sum[POOL_HALO:tile - POOL_HALO] * (1.0 / w),
             edge_mean(win_sum[tile - POOL_HALO:], tile - POOL_HALO)], axis=0)
        p = (mean - bxe[0:tile, c0:c1]).astype(BF16)
        gate = _silu(_dot(h, win_ref[:, col(4) + c0:col(4) + c1]))
        yb = _dot(p, wg_ref[gi].astype(BF16)) * sc_ref[:, c0:c1] * gate
        cat_ref[:, A_WIDTH + c0:A_WIDTH + c1] = yb.astype(BF16)

    o_ref[0] = x + _dot(cat_ref[...], wout_ref[...])


def _layer0(x, norm_0, w_in_0, a_v_norm_0, a_spatial_w_0, a_spatial_b_0, b_group_w_0, b_scale_0,
            w_out_0, w_in_1, w_out_1):
    bn, seq, d = x.shape
    tile = TILE_L0
    nt = seq // tile
    assert bn * nt >= W1_CHUNKS

    def w1_chunk(b, j):
        return jnp.minimum(b * nt + j, W1_CHUNKS - 1)
    halo_blocks = seq // POOL_HALO
    win = w_in_0.astype(BF16)
    wout = w_out_0.astype(BF16)
    ws = a_spatial_w_0
    wg = b_group_w_0
    bsb = jnp.repeat(a_spatial_b_0.T, A_GROUP_DIM, axis=1)

    def const(shape):
        return pl.BlockSpec(shape, lambda b, j: (0,) * len(shape), pipeline_mode=pl.Buffered(1))

    per_halo = tile // POOL_HALO
    kern = functools.partial(_layer0_kernel, tile=tile, seq=seq)
    return pl.pallas_call(
        kern,
        grid=(bn, nt),
        in_specs=[
            pl.BlockSpec((1, tile, d), lambda b, j: (b, j, 0)),
            pl.BlockSpec((1, POOL_HALO, d), lambda b, j: (b, jnp.maximum(j * per_halo - 1, 0), 0)),
            pl.BlockSpec((1, POOL_HALO, d),
                         lambda b, j: (b, jnp.minimum((j + 1) * per_halo, halo_blocks - 1), 0)),
            const((1, d)),
            const(win.shape),
            const((1, A_WIDTH)),
            const(ws.shape),
            const(bsb.shape),
            const(wg.shape),
            const((1, B_WIDTH)),
            const(wout.shape),
            pl.BlockSpec((W1_CHUNK_ROWS, w_in_1.shape[1]), lambda b, j: (w1_chunk(b, j), 0)),
            pl.BlockSpec((W1_CHUNK_ROWS, d), lambda b, j: (w1_chunk(b, j), 0)),
        ],
        out_specs=[
            pl.BlockSpec((1, tile, d), lambda b, j: (b, j, 0)),
            pl.BlockSpec((QKV_COLS, W1_CHUNK_ROWS), lambda b, j: (0, w1_chunk(b, j))),
            pl.BlockSpec((W1_CHUNK_ROWS, C_WIDTH), lambda b, j: (w1_chunk(b, j), 0)),
            pl.BlockSpec((W1_CHUNK_ROWS, d), lambda b, j: (w1_chunk(b, j), 0)),
        ],
        out_shape=[
            jax.ShapeDtypeStruct(x.shape, F32),
            jax.ShapeDtypeStruct((QKV_COLS, d), BF16),
            jax.ShapeDtypeStruct((d, C_WIDTH), BF16),
            jax.ShapeDtypeStruct((C_WIDTH, d), BF16),
        ],
        scratch_shapes=[
            pltpu.VMEM((tile + 2 * POOL_HALO, d), BF16),
            pltpu.VMEM((tile, A_WIDTH), BF16),
            pltpu.VMEM((tile, A_WIDTH + B_WIDTH), BF16),
        ],
        compiler_params=pltpu.CompilerParams(
            dimension_semantics=("arbitrary", "arbitrary"), vmem_limit_bytes=VMEM_LIMIT),
        name="layer0_mixer",
    )(x, x, x, norm_0.reshape(1, d), win, a_v_norm_0.reshape(1, A_WIDTH), ws, bsb, wg,
      b_scale_0.reshape(1, B_WIDTH), wout, w_in_1, w_out_1)


def _qkv_kernel(x_ref, n1_ref, w_ref, cost_ref, sint_ref, h_ref, qt_ref, ka_ref, vt_ref, *, tile):
    h = _rms(x_ref[0], n1_ref[...]).astype(BF16)
    h_ref[0] = h
    half = ROT_DIM // 2
    scale = HEAD_DIM ** -0.5 * LOG2_E

    qkv = _dot_nt(w_ref[...], h)
    ct = cost_ref[...]
    st = sint_ref[...]

    def rope_rows(r0):
        x1 = qkv[r0:r0 + half]
        x2 = qkv[r0 + half:r0 + ROT_DIM]
        return jnp.concatenate([x1 * ct - x2 * st, x2 * ct + x1 * st], axis=0)

    for hd in range(N_HEADS):
        r0 = hd * HEAD_DIM
        qt_ref[0, r0:r0 + ROT_DIM, :] = (rope_rows(r0) * scale).astype(BF16)
        qt_ref[0, r0 + ROT_DIM:r0 + HEAD_DIM, :] = (
            qkv[r0 + ROT_DIM:r0 + HEAD_DIM] * scale).astype(BF16)
    vt_ref[0] = qkv[C_WIDTH + KV_WIDTH:].astype(BF16)

    pad = jnp.zeros((LANES - HEAD_DIM, tile), F32)
    for kh in range(N_KV_HEADS):
        r0 = C_WIDTH + kh * HEAD_DIM
        kt = jnp.concatenate([rope_rows(r0), qkv[r0 + ROT_DIM:r0 + HEAD_DIM], pad], axis=0)
        ka_ref[0, :, kh * LANES:(kh + 1) * LANES] = kt.T.astype(BF16)


def _rope_tables(seq):
    inv = ROPE_THETA ** (-np.arange(0, ROT_DIM, 2, dtype=np.float64) / ROT_DIM)
    ang = np.arange(seq, dtype=np.float64)[None, :] * inv[:, None]
    return jnp.asarray(np.cos(ang), F32), jnp.asarray(np.sin(ang), F32)


def _qkv(x, norm_1, w_qkv_t):
    bn, seq, d = x.shape
    tile = TILE_QKV
    nt = seq // tile
    cos_fm, sin_fm = _rope_tables(seq)
    kern = functools.partial(_qkv_kernel, tile=tile)
    tab_fm = pl.BlockSpec((ROT_DIM // 2, tile), lambda b, j: (0, j))

    def const(shape):
        return pl.BlockSpec(shape, lambda b, j: (0,) * len(shape), pipeline_mode=pl.Buffered(1))

    return pl.pallas_call(
        kern,
        grid=(bn, nt),
        in_specs=[
            pl.BlockSpec((1, tile, d), lambda b, j: (b, j, 0)),
            const((1, d)),
            const(w_qkv_t.shape),
            tab_fm, tab_fm,
        ],
        out_specs=[pl.BlockSpec((1, tile, d), lambda b, j: (b, j, 0)),
                   pl.BlockSpec((1, C_WIDTH, tile), lambda b, j: (b, 0, j)),
                   pl.BlockSpec((1, tile, KV_PAD_WIDTH), lambda b, j: (b, j, 0)),
                   pl.BlockSpec((1, KV_WIDTH, tile), lambda b, j: (b, 0, j))],
        out_shape=[jax.ShapeDtypeStruct((bn, seq, d), BF16),
                   jax.ShapeDtypeStruct((bn, C_WIDTH, seq), BF16),
                   jax.ShapeDtypeStruct((bn, seq, KV_PAD_WIDTH), BF16),
                   jax.ShapeDtypeStruct((bn, KV_WIDTH, seq), BF16)],
        compiler_params=pltpu.CompilerParams(
            dimension_semantics=("arbitrary", "arbitrary"), vmem_limit_bytes=VMEM_LIMIT),
        name="layer1_qkv",
    )(x, norm_1.reshape(1, d), w_qkv_t, cos_fm, sin_fm)


def _attn_kernel(sink_ref, x_ref, h_ref, qt_ref, kap_ref, kat_ref, kan_ref, vtp_ref, vtt_ref,
                 vtn_ref, wgate_ref, wout_ref, fn_ref, o_ref,
                 s_ref, p_ref, sg_ref, y_ref, *, tile, seq):
    j = pl.program_id(1)
    nblk = tile // ATTN_BLOCK
    last_blk = seq // ATTN_BLOCK - 1

    def k_band(i, kh):
        cols = slice(kh * LANES, (kh + 1) * LANES)
        lo = max(i - 1, 0) * ATTN_BLOCK
        hi = min(i + 2, nblk) * ATTN_BLOCK
        parts = [kat_ref[0, lo:hi, cols]]
        if i == 0:
            parts.insert(0, kap_ref[0, :, cols])
        if i == nblk - 1:
            parts.append(kan_ref[0, :, cols])
        return jnp.concatenate(parts, axis=0)

    def vt_band(i, kh):
        rows = slice(kh * HEAD_DIM, (kh + 1) * HEAD_DIM)
        lo = max(i - 1, 0) * ATTN_BLOCK
        hi = min(i + 2, nblk) * ATTN_BLOCK
        parts = [vtt_ref[0, rows, lo:hi]]
        if i == 0:
            parts.insert(0, vtp_ref[0, rows, :])
        if i == nblk - 1:
            parts.append(vtn_ref[0, rows, :])
        return parts
    diff = (lax.broadcasted_iota(jnp.int32, (ATTN_BLOCK, ATTN_BLOCK), 0)
            - lax.broadcasted_iota(jnp.int32, (ATTN_BLOCK, ATTN_BLOCK), 1))
    never = 2 * ATTN_BLOCK
    ones_rows = jnp.ones((HEAD_DIM, BAND), BF16)

    def scores(u, half):
        i, kh = divmod(u, N_KV_HEADS)
        slot = u % 2
        c0 = i * ATTN_BLOCK
        h0 = kh * GQA_GROUP + half * HEADS_PER_HALF
        qg = jnp.concatenate(
            [qt_ref[0, (h0 + g) * HEAD_DIM:(h0 + g + 1) * HEAD_DIM, c0:c0 + ATTN_BLOCK]
             for g in range(HEADS_PER_HALF)], axis=1)
        rq = jnp.concatenate([qg, jnp.zeros((LANES - HEAD_DIM, HALF_Q), BF16)], axis=0)
        ka = k_band(i, kh)
        s_ref[slot, :, half * HALF_Q:(half + 1) * HALF_Q] = _dot(ka, rq)

    def softmax_head(u, g):
        i, kh = divmod(u, N_KV_HEADS)
        slot = u % 2
        gblk = j * nblk + i
        ok_prev = diff >= jnp.where(gblk > 0, 0, never)
        ok_next = diff <= jnp.where(gblk < last_blk, 0, -never)
        cs = slice(g * ATTN_BLOCK, (g + 1) * ATTN_BLOCK)
        sink = sink_ref[kh * GQA_GROUP + g] * LOG2_E
        s0 = jnp.where(ok_prev, s_ref[slot, 0:ATTN_BLOCK, cs], NEG_INF)
        s1 = s_ref[slot, ATTN_BLOCK:2 * ATTN_BLOCK, cs]
        s2 = jnp.where(ok_next, s_ref[slot, 2 * ATTN_BLOCK:, cs], NEG_INF)
        m = jnp.max(jnp.maximum(jnp.maximum(s0, s1), s2), axis=0, keepdims=True)
        m = jnp.maximum(m, sink)
        p0 = jnp.exp2(s0 - m)
        p1 = jnp.exp2(s1 - m)
        p2 = jnp.exp2(s2 - m)
        p_ref[slot, 0:ATTN_BLOCK, cs] = p0.astype(BF16)
        p_ref[slot, ATTN_BLOCK:2 * ATTN_BLOCK, cs] = p1.astype(BF16)
        p_ref[slot, 2 * ATTN_BLOCK:, cs] = p2.astype(BF16)
        return jnp.exp2(sink - m)

    def weighted(u, half, sink_share):
        i, kh = divmod(u, N_KV_HEADS)
        slot = u % 2
        c0 = i * ATTN_BLOCK
        vt = jnp.concatenate([jnp.concatenate(vt_band(i, kh), axis=1), ones_rows],
                             axis=0)
        ot = _dot(vt, p_ref[slot, :, half * HALF_Q:(half + 1) * HALF_Q])
        parts = []
        for g in range(HEADS_PER_HALF):
            cq = slice(g * ATTN_BLOCK, (g + 1) * ATTN_BLOCK)
            l = ot[HEAD_DIM:HEAD_DIM + 1, cq] + sink_share[g]
            parts.append(ot[0:HEAD_DIM, cq] * (1.0 / l))
        yt = jnp.concatenate(parts, axis=0)
        f0 = (kh * GQA_GROUP + half * HEADS_PER_HALF) * HEAD_DIM
        cs = slice(f0, f0 + HEADS_PER_HALF * HEAD_DIM)
        y = yt.T * sg_ref[c0:c0 + ATTN_BLOCK, cs]
        y_ref[c0:c0 + ATTN_BLOCK, cs] = y.astype(BF16)

    def gate_chunk(kh):
        cs = slice(kh * GQA_GROUP * HEAD_DIM, (kh + 1) * GQA_GROUP * HEAD_DIM)
        sg_ref[:, cs] = _silu(_dot(h_ref[0], wgate_ref[:, cs]))

    def finish_rows(r0, n):
        rows = slice(r0, r0 + n)
        out = x_ref[0, rows, :] + _dot(y_ref[rows, :], wout_ref[...])
        o_ref[0, rows, :] = _rms(out, fn_ref[...])

    units = nblk * N_KV_HEADS
    inv = {}
    for step in range(units + 2):
        sc = step if step < units else None
        sm = step - 1 if 1 <= step <= units else None
        wt = step - 2 if step >= 2 else None
        if step < N_KV_HEADS:
            gate_chunk(step)
        for half in range(2):
            if sc is not None:
                scores(sc, half)
            if sm is not None:
                inv[sm, 2 * half] = softmax_head(sm, 2 * half)
            if wt is not None:
                weighted(wt, half, [inv.pop((wt, half * HEADS_PER_HALF + g))
                                    for g in range(HEADS_PER_HALF)])
            if sm is not None:
                inv[sm, 2 * half + 1] = softmax_head(sm, 2 * half + 1)
        if wt is not None and (wt + 1) % (units // 2) == 0:
            finish_rows((wt + 1 - units // 2) // N_KV_HEADS * ATTN_BLOCK, tile // 2)


def _attention(x, h, qt, ka, vt, w_gate, sink_1, w_out_1, final_norm):
    bn, seq, d = x.shape
    tile = TILE_ATTN
    nt = seq // tile
    per = tile // ATTN_BLOCK
    nb = seq // ATTN_BLOCK

    def const(shape):
        return pl.BlockSpec(shape, lambda b, j: (0,) * len(shape), pipeline_mode=pl.Buffered(1))

    def prev_blk(j):
        return jnp.maximum(j * per - 1, 0)

    def next_blk(j):
        return jnp.minimum((j + 1) * per, nb - 1)

    kern = functools.partial(_attn_kernel, tile=tile, seq=seq)
    return pl.pallas_call(
        kern,
        grid=(bn, nt),
        in_specs=[
            pl.BlockSpec(memory_space=pltpu.SMEM),
            pl.BlockSpec((1, tile, d), lambda b, j: (b, j, 0)),
            pl.BlockSpec((1, tile, d), lambda b, j: (b, j, 0)),
            pl.BlockSpec((1, C_WIDTH, tile), lambda b, j: (b, 0, j)),
            pl.BlockSpec((1, ATTN_BLOCK, KV_PAD_WIDTH), lambda b, j: (b, prev_blk(j), 0)),
            pl.BlockSpec((1, tile, KV_PAD_WIDTH), lambda b, j: (b, j, 0)),
            pl.BlockSpec((1, ATTN_BLOCK, KV_PAD_WIDTH), lambda b, j: (b, next_blk(j), 0)),
            pl.BlockSpec((1, KV_WIDTH, ATTN_BLOCK), lambda b, j: (b, 0, prev_blk(j))),
            pl.BlockSpec((1, KV_WIDTH, tile), lambda b, j: (b, 0, j)),
            pl.BlockSpec((1, KV_WIDTH, ATTN_BLOCK), lambda b, j: (b, 0, next_blk(j))),
            const(w_gate.shape),
            const(w_out_1.shape),
            const((1, d)),
        ],
        out_specs=pl.BlockSpec((1, tile, d), lambda b, j: (b, j, 0)),
        out_shape=jax.ShapeDtypeStruct(x.shape, F32),
        scratch_shapes=[
            pltpu.VMEM((2, BAND, GROUP_Q), F32),
            pltpu.VMEM((2, BAND, GROUP_Q), BF16),
            pltpu.VMEM((tile, C_WIDTH), F32),
            pltpu.VMEM((tile, C_WIDTH), BF16),
        ],
        compiler_params=pltpu.CompilerParams(
            dimension_semantics=("arbitrary", "arbitrary"), vmem_limit_bytes=VMEM_LIMIT),
        name="layer1_attention",
    )(sink_1, x, h, qt, ka, ka, ka, vt, vt, vt, w_gate, w_out_1, final_norm.reshape(1, d))


def kernel(x, norm_0, w_in_0, a_v_norm_0, a_spatial_w_0, a_spatial_b_0, b_group_w_0, b_scale_0,
           w_out_0, norm_1, w_in_1, sink_1, w_out_1, final_norm):
    x1, w_qkv_t, w_gate, w_out_1b = _layer0(
        x, norm_0, w_in_0, a_v_norm_0, a_spatial_w_0, a_spatial_b_0, b_group_w_0, b_scale_0,
        w_out_0, w_in_1, w_out_1)
    h1, qt, ka, vt = _qkv(x1, norm_1, w_qkv_t)
    return _attention(x1, h1, qt, ka, vt, w_gate, sink_1, w_out_1b, final_norm)
```

```python
import functools
import math

import jax
import jax.numpy as jnp
import numpy as np
from jax import lax
from jax.experimental import pallas as pl
from jax.experimental.pallas import tpu as pltpu

F32 = jnp.float32
BF16 = jnp.bfloat16

D_MODEL = 1024
EPS = 1e-6
NEG_INF = -1e30
LOG2_E = math.log2(math.e)
CHUNK = 128
A_GROUPS = 4
A_WIDTH = D_MODEL
A_GROUP_DIM = A_WIDTH // A_GROUPS
B_WIDTH = D_MODEL
POOL_WINDOWS = (2, 4, 8, 16)
B_GROUP_DIM = B_WIDTH // len(POOL_WINDOWS)
POOL_HALO = max(POOL_WINDOWS) // 2
N_HEADS = 16
N_KV_HEADS = 4
HEAD_DIM = 64
GQA_GROUP = N_HEADS // N_KV_HEADS
C_WIDTH = N_HEADS * HEAD_DIM
KV_WIDTH = N_KV_HEADS * HEAD_DIM
WINDOW = 128
ATTN_BLOCK = 128
BAND = 3 * ATTN_BLOCK
ROPE_THETA = 500000.0
ROT_DIM = HEAD_DIM // 4
LANES = 128
KV_PAD_WIDTH = N_KV_HEADS * LANES
GROUP_Q = GQA_GROUP * ATTN_BLOCK
HEADS_PER_HALF = GQA_GROUP // 2
HALF_Q = HEADS_PER_HALF * ATTN_BLOCK

QKV_COLS = C_WIDTH + 2 * KV_WIDTH
W1_CHUNK_ROWS = 128
W1_CHUNKS = D_MODEL // W1_CHUNK_ROWS

TILE_L0 = 512
TILE_ATTN = 512
VMEM_LIMIT = 56 * 1024 * 1024


def _rms(x, g):
    return x * lax.rsqrt(jnp.mean(x * x, axis=-1, keepdims=True) + EPS) * g


def _silu(x):
    return x / (1.0 + jnp.exp(-x))


def _dot(a, b):
    return jnp.dot(a, b, preferred_element_type=F32)


def _dot_nt(a, b):
    return lax.dot_general(a, b, (((1,), (1,)), ((), ())), preferred_element_type=F32)


def _layer0_kernel(x_ref, xp_ref, xn_ref, n0_ref, win_ref, gv_ref, ws_ref, bsb_ref, wg_ref,
                   sc_ref, wout_ref, w1_ref, wo1_ref, n1_ref, wqkv_ref, cost_ref, sint_ref,
                   o_ref, wgate_ref, wo1b_ref, h1_ref, qt_ref, ka_ref, vt_ref,
                   hext_ref, v_ref, cat_ref, wqkvt_ref, *, tile, seq):
    j = pl.program_id(1)
    nj = pl.num_programs(1)
    step = pl.program_id(0) * nj + j

    @pl.when(step == 0)
    def _():
        for c in range(W1_CHUNKS):
            rows = slice(c * W1_CHUNK_ROWS, (c + 1) * W1_CHUNK_ROWS)
            wqkvt_ref[:, rows] = wqkv_ref[rows, :].T.astype(BF16)

    @pl.when(step < W1_CHUNKS)
    def _():
        wgate_ref[...] = w1_ref[:, QKV_COLS:].astype(BF16)
        wo1b_ref[...] = wo1_ref[...].astype(BF16)

    n0 = n0_ref[...]
    x = x_ref[0]
    h = _rms(x, n0).astype(BF16)
    hext_ref[0:tile, :] = h
    hp = jnp.where(j > 0, _rms(xp_ref[0], n0), 0.0)
    hn = jnp.where(j < nj - 1, _rms(xn_ref[0], n0), 0.0)
    hext_ref[tile:tile + 2 * POOL_HALO, :] = jnp.concatenate([hp, hn], axis=0).astype(BF16)

    def col(k):
        return k * D_MODEL

    a_v = jax.nn.gelu(_dot(h, win_ref[:, col(1):col(2)]))
    v_ref[...] = _rms(a_v, gv_ref[...]).astype(BF16)
    for g in range(A_GROUPS):
        c0 = g * A_GROUP_DIM
        c1 = c0 + A_GROUP_DIM
        u = jax.nn.gelu(_dot(h, win_ref[:, col(0) + c0:col(0) + c1]))
        gate = _silu(_dot(h, win_ref[:, col(2) + c0:col(2) + c1]))
        ws_g = ws_ref[g].astype(BF16)
        for c in range(tile // CHUNK):
            r0 = c * CHUNK
            r1 = r0 + CHUNK
            mixed = _dot(ws_g, v_ref[r0:r1, c0:c1]) + bsb_ref[:, c0:c1]
            cat_ref[r0:r1, c0:c1] = (u[r0:r1] * mixed * gate[r0:r1]).astype(BF16)

    bxe = _dot(hext_ref[...], win_ref[:, col(3):col(4)])
    n_ext = tile + 2 * POOL_HALO
    edge_row = lax.broadcasted_iota(jnp.int32, (POOL_HALO, 1), 0)
    for gi, w in enumerate(POOL_WINDOWS):
        c0 = gi * B_GROUP_DIM
        c1 = c0 + B_GROUP_DIM
        half = w // 2
        e = jnp.concatenate([bxe[tile:tile + POOL_HALO, c0:c1], bxe[0:tile, c0:c1],
                             bxe[tile + POOL_HALO:, c0:c1]], axis=0)
        a = e
        span = 1
        while span < half:
            a = a + pltpu.roll(a, n_ext - span, 0)
            span *= 2
        win_sum = (a + pltpu.roll(a, half, 0))[POOL_HALO:POOL_HALO + tile]

        def edge_mean(slab, row0, half=half):
            t = j * tile + row0 + edge_row
            cnt = (jnp.minimum(t + half, seq) - jnp.maximum(t - half, 0)).astype(F32)
            return slab / cnt

        mean = jnp.concatenate(
            [edge_mean(win_sum[0:POOL_HALO], 0),
             win_sum[POOL_HALO:tile - POOL_HALO] * (1.0 / w),
             edge_mean(win_sum[tile - POOL_HALO:], tile - POOL_HALO)], axis=0)
        p = (mean - bxe[0:tile, c0:c1]).astype(BF16)
        gate = _silu(_dot(h, win_ref[:, col(4) + c0:col(4) + c1]))
        yb = _dot(p, wg_ref[gi].astype(BF16)) * sc_ref[:, c0:c1] * gate
        cat_ref[:, A_WIDTH + c0:A_WIDTH + c1] = yb.astype(BF16)

    n1 = n1_ref[...]
    hrows = tile // 2
    for r0 in range(0, tile, hrows):
        rows = slice(r0, r0 + hrows)
        out = x_ref[0, rows, :] + _dot(cat_ref[rows, :], wout_ref[...])
        o_ref[0, rows, :] = out
        h1 = _rms(out, n1).astype(BF16)
        h1_ref[0, rows, :] = h1
        _qkv_project(h1, wqkvt_ref, cost_ref[:, rows], sint_ref[:, rows], qt_ref, ka_ref, vt_ref,
                     rows)


def _layer0(x, norm_0, w_in_0, a_v_norm_0, a_spatial_w_0, a_spatial_b_0, b_group_w_0, b_scale_0,
            w_out_0, norm_1, w_in_1, w_out_1):
    bn, seq, d = x.shape
    tile = TILE_L0
    nt = seq // tile
    assert bn * nt >= W1_CHUNKS

    def w1_chunk(b, j):
        return jnp.minimum(b * nt + j, W1_CHUNKS - 1)
    halo_blocks = seq // POOL_HALO
    win = w_in_0.astype(BF16)
    wout = w_out_0.astype(BF16)
    ws = a_spatial_w_0
    wg = b_group_w_0
    cos_fm, sin_fm = _rope_tables(seq)
    bsb = jnp.repeat(a_spatial_b_0.T, A_GROUP_DIM, axis=1)

    def const(shape):
        return pl.BlockSpec(shape, lambda b, j: (0,) * len(shape), pipeline_mode=pl.Buffered(1))

    per_halo = tile // POOL_HALO
    kern = functools.partial(_layer0_kernel, tile=tile, seq=seq)
    return pl.pallas_call(
        kern,
        grid=(bn, nt),
        in_specs=[
            pl.BlockSpec((1, tile, d), lambda b, j: (b, j, 0)),
            pl.BlockSpec((1, POOL_HALO, d), lambda b, j: (b, jnp.maximum(j * per_halo - 1, 0), 0)),
            pl.BlockSpec((1, POOL_HALO, d),
                         lambda b, j: (b, jnp.minimum((j + 1) * per_halo, halo_blocks - 1), 0)),
            const((1, d)),
            const(win.shape),
            const((1, A_WIDTH)),
            const(ws.shape),
            const(bsb.shape),
            const(wg.shape),
            const((1, B_WIDTH)),
            const(wout.shape),
            pl.BlockSpec((W1_CHUNK_ROWS, w_in_1.shape[1]), lambda b, j: (w1_chunk(b, j), 0)),
            pl.BlockSpec((W1_CHUNK_ROWS, d), lambda b, j: (w1_chunk(b, j), 0)),
            const((1, d)),
            const((d, QKV_COLS)),
            pl.BlockSpec((ROT_DIM // 2, tile), lambda b, j: (0, j)),
            pl.BlockSpec((ROT_DIM // 2, tile), lambda b, j: (0, j)),
        ],
        out_specs=[
            pl.BlockSpec((1, tile, d), lambda b, j: (b, j, 0)),
            pl.BlockSpec((W1_CHUNK_ROWS, C_WIDTH), lambda b, j: (w1_chunk(b, j), 0)),
            pl.BlockSpec((W1_CHUNK_ROWS, d), lambda b, j: (w1_chunk(b, j), 0)),
            pl.BlockSpec((1, tile, d), lambda b, j: (b, j, 0)),
            pl.BlockSpec((1, C_WIDTH, tile), lambda b, j: (b, 0, j)),
            pl.BlockSpec((1, tile, KV_PAD_WIDTH), lambda b, j: (b, j, 0)),
            pl.BlockSpec((1, KV_WIDTH, tile), lambda b, j: (b, 0, j)),
        ],
        out_shape=[
            jax.ShapeDtypeStruct(x.shape, F32),
            jax.ShapeDtypeStruct((d, C_WIDTH), BF16),
            jax.ShapeDtypeStruct((C_WIDTH, d), BF16),
            jax.ShapeDtypeStruct((bn, seq, d), BF16),
            jax.ShapeDtypeStruct((bn, C_WIDTH, seq), BF16),
            jax.ShapeDtypeStruct((bn, seq, KV_PAD_WIDTH), BF16),
            jax.ShapeDtypeStruct((bn, KV_WIDTH, seq), BF16),
        ],
        scratch_shapes=[
            pltpu.VMEM((tile + 2 * POOL_HALO, d), BF16),
            pltpu.VMEM((tile, A_WIDTH), BF16),
            pltpu.VMEM((tile, A_WIDTH + B_WIDTH), BF16),
            pltpu.VMEM((QKV_COLS, d), BF16),
        ],
        compiler_params=pltpu.CompilerParams(
            dimension_semantics=("arbitrary", "arbitrary"), vmem_limit_bytes=VMEM_LIMIT),
        name="layer0_mixer",
    )(x, x, x, norm_0.reshape(1, d), win, a_v_norm_0.reshape(1, A_WIDTH), ws, bsb, wg,
      b_scale_0.reshape(1, B_WIDTH), wout, w_in_1, w_out_1, norm_1.reshape(1, d), w_in_1,
      cos_fm, sin_fm)


def _qkv_project(h, w_ref, ct, st, qt_ref, ka_ref, vt_ref, pos):
    half = ROT_DIM // 2
    n = h.shape[0]
    scale = HEAD_DIM ** -0.5 * LOG2_E

    qkv = _dot_nt(w_ref[...], h)

    def rope_rows(r0):
        x1 = qkv[r0:r0 + half]
        x2 = qkv[r0 + half:r0 + ROT_DIM]
        return jnp.concatenate([x1 * ct - x2 * st, x2 * ct + x1 * st], axis=0)

    for hd in range(N_HEADS):
        r0 = hd * HEAD_DIM
        qt_ref[0, r0:r0 + ROT_DIM, pos] = (rope_rows(r0) * scale).astype(BF16)
        qt_ref[0, r0 + ROT_DIM:r0 + HEAD_DIM, pos] = (
            qkv[r0 + ROT_DIM:r0 + HEAD_DIM] * scale).astype(BF16)
    vt_ref[0, :, pos] = qkv[C_WIDTH + KV_WIDTH:].astype(BF16)

    pad = jnp.zeros((LANES - HEAD_DIM, n), F32)
    for kh in range(N_KV_HEADS):
        r0 = C_WIDTH + kh * HEAD_DIM
        kt = jnp.concatenate([rope_rows(r0), qkv[r0 + ROT_DIM:r0 + HEAD_DIM], pad], axis=0)
        ka_ref[0, pos, kh * LANES:(kh + 1) * LANES] = kt.T.astype(BF16)


def _rope_tables(seq):
    inv = ROPE_THETA ** (-np.arange(0, ROT_DIM, 2, dtype=np.float64) / ROT_DIM)
    ang = np.arange(seq, dtype=np.float64)[None, :] * inv[:, None]
    return jnp.asarray(np.cos(ang), F32), jnp.asarray(np.sin(ang), F32)


def _attn_kernel(sink_ref, x_ref, h_ref, qt_ref, kap_ref, kat_ref, kan_ref, vtp_ref, vtt_ref,
                 vtn_ref, wgate_ref, wout_ref, fn_ref, o_ref,
                 s_ref, p_ref, sg_ref, y_ref, *, tile, seq):
    j = pl.program_id(1)
    nblk = tile // ATTN_BLOCK
    last_blk = seq // ATTN_BLOCK - 1

    def k_band(i, kh):
        cols = slice(kh * LANES, (kh + 1) * LANES)
        lo = max(i - 1, 0) * ATTN_BLOCK
        hi = min(i + 2, nblk) * ATTN_BLOCK
        parts = [kat_ref[0, lo:hi, cols]]
        if i == 0:
            parts.insert(0, kap_ref[0, :, cols])
        if i == nblk - 1:
            parts.append(kan_ref[0, :, cols])
        return jnp.concatenate(parts, axis=0)

    def vt_band(i, kh):
        rows = slice(kh * HEAD_DIM, (kh + 1) * HEAD_DIM)
        lo = max(i - 1, 0) * ATTN_BLOCK
        hi = min(i + 2, nblk) * ATTN_BLOCK
        parts = [vtt_ref[0, rows, lo:hi]]
        if i == 0:
            parts.insert(0, vtp_ref[0, rows, :])
        if i == nblk - 1:
            parts.append(vtn_ref[0, rows, :])
        return parts
    diff = (lax.broadcasted_iota(jnp.int32, (ATTN_BLOCK, ATTN_BLOCK), 0)
            - lax.broadcasted_iota(jnp.int32, (ATTN_BLOCK, ATTN_BLOCK), 1))
    never = 2 * ATTN_BLOCK
    ones_rows = jnp.ones((HEAD_DIM, BAND), BF16)

    def scores(u, half):
        i, kh = divmod(u, N_KV_HEADS)
        slot = u % 2
        c0 = i * ATTN_BLOCK
        h0 = kh * GQA_GROUP + half * HEADS_PER_HALF
        qg = jnp.concatenate(
            [qt_ref[0, (h0 + g) * HEAD_DIM:(h0 + g + 1) * HEAD_DIM, c0:c0 + ATTN_BLOCK]
             for g in range(HEADS_PER_HALF)], axis=1)
        rq = jnp.concatenate([qg, jnp.zeros((LANES - HEAD_DIM, HALF_Q), BF16)], axis=0)
        ka = k_band(i, kh)
        s_ref[slot, :, half * HALF_Q:(half + 1) * HALF_Q] = _dot(ka, rq)

    def softmax_head(u, g):
        i, kh = divmod(u, N_KV_HEADS)
        slot = u % 2
        gblk = j * nblk + i
        ok_prev = diff >= jnp.where(gblk > 0, 0, never)
        ok_next = diff <= jnp.where(gblk < last_blk, 0, -never)
        cs = slice(g * ATTN_BLOCK, (g + 1) * ATTN_BLOCK)
        sink = sink_ref[kh * GQA_GROUP + g] * LOG2_E
        s0 = jnp.where(ok_prev, s_ref[slot, 0:ATTN_BLOCK, cs], NEG_INF)
        s1 = s_ref[slot, ATTN_BLOCK:2 * ATTN_BLOCK, cs]
        s2 = jnp.where(ok_next, s_ref[slot, 2 * ATTN_BLOCK:, cs], NEG_INF)
        m = jnp.max(jnp.maximum(jnp.maximum(s0, s1), s2), axis=0, keepdims=True)
        m = jnp.maximum(m, sink)
        p0 = jnp.exp2(s0 - m)
        p1 = jnp.exp2(s1 - m)
        p2 = jnp.exp2(s2 - m)
        p_ref[slot, 0:ATTN_BLOCK, cs] = p0.astype(BF16)
        p_ref[slot, ATTN_BLOCK:2 * ATTN_BLOCK, cs] = p1.astype(BF16)
        p_ref[slot, 2 * ATTN_BLOCK:, cs] = p2.astype(BF16)
        return jnp.exp2(sink - m)

    def weighted(u, half, sink_share):
        i, kh = divmod(u, N_KV_HEADS)
        slot = u % 2
        c0 = i * ATTN_BLOCK
        vt = jnp.concatenate([jnp.concatenate(vt_band(i, kh), axis=1), ones_rows],
                             axis=0)
        ot = _dot(vt, p_ref[slot, :, half * HALF_Q:(half + 1) * HALF_Q])
        parts = []
        for g in range(HEADS_PER_HALF):
            cq = slice(g * ATTN_BLOCK, (g + 1) * ATTN_BLOCK)
            l = ot[HEAD_DIM:HEAD_DIM + 1, cq] + sink_share[g]
            parts.append(ot[0:HEAD_DIM, cq] * (1.0 / l))
        yt = jnp.concatenate(parts, axis=0)
        f0 = (kh * GQA_GROUP + half * HEADS_PER_HALF) * HEAD_DIM
        cs = slice(f0, f0 + HEADS_PER_HALF * HEAD_DIM)
        y = yt.T * sg_ref[c0:c0 + ATTN_BLOCK, cs]
        y_ref[c0:c0 + ATTN_BLOCK, cs] = y.astype(BF16)

    def gate_chunk(kh):
        cs = slice(kh * GQA_GROUP * HEAD_DIM, (kh + 1) * GQA_GROUP * HEAD_DIM)
        sg_ref[:, cs] = _silu(_dot(h_ref[0], wgate_ref[:, cs]))

    def finish_rows(r0, n):
        rows = slice(r0, r0 + n)
        out = x_ref[0, rows, :] + _dot(y_ref[rows, :], wout_ref[...])
        o_ref[0, rows, :] = _rms(out, fn_ref[...])

    units = nblk * N_KV_HEADS
    inv = {}
    for step in range(units + 2):
        sc = step if step < units else None
        sm = step - 1 if 1 <= step <= units else None
        wt = step - 2 if step >= 2 else None
        if step < N_KV_HEADS:
            gate_chunk(step)
        for half in range(2):
            if sc is not None:
                scores(sc, half)
            if sm is not None:
                inv[sm, 2 * half] = softmax_head(sm, 2 * half)
            if wt is not None:
                weighted(wt, half, [inv.pop((wt, half * HEADS_PER_HALF + g))
                                    for g in range(HEADS_PER_HALF)])
            if sm is not None:
                inv[sm, 2 * half + 1] = softmax_head(sm, 2 * half + 1)
        if wt is not None and (wt + 1) % (units // 2) == 0:
            finish_rows((wt + 1 - units // 2) // N_KV_HEADS * ATTN_BLOCK, tile // 2)


def _attention(x, h, qt, ka, vt, w_gate, sink_1, w_out_1, final_norm):
    bn, seq, d = x.shape
    tile = TILE_ATTN
    nt = seq // tile
    per = tile // ATTN_BLOCK
    nb = seq // ATTN_BLOCK

    def const(shape):
        return pl.BlockSpec(shape, lambda b, j: (0,) * len(shape), pipeline_mode=pl.Buffered(1))

    def prev_blk(j):
        return jnp.maximum(j * per - 1, 0)

    def next_blk(j):
        return jnp.minimum((j + 1) * per, nb - 1)

    kern = functools.partial(_attn_kernel, tile=tile, seq=seq)
    return pl.pallas_call(
        kern,
        grid=(bn, nt),
        in_specs=[
            pl.BlockSpec(memory_space=pltpu.SMEM),
            pl.BlockSpec((1, tile, d), lambda b, j: (b, j, 0)),
            pl.BlockSpec((1, tile, d), lambda b, j: (b, j, 0)),
            pl.BlockSpec((1, C_WIDTH, tile), lambda b, j: (b, 0, j)),
            pl.BlockSpec((1, ATTN_BLOCK, KV_PAD_WIDTH), lambda b, j: (b, prev_blk(j), 0)),
            pl.BlockSpec((1, tile, KV_PAD_WIDTH), lambda b, j: (b, j, 0)),
            pl.BlockSpec((1, ATTN_BLOCK, KV_PAD_WIDTH), lambda b, j: (b, next_blk(j), 0)),
            pl.BlockSpec((1, KV_WIDTH, ATTN_BLOCK), lambda b, j: (b, 0, prev_blk(j))),
            pl.BlockSpec((1, KV_WIDTH, tile), lambda b, j: (b, 0, j)),
            pl.BlockSpec((1, KV_WIDTH, ATTN_BLOCK), lambda b, j: (b, 0, next_blk(j))),
            const(w_gate.shape),
            const(w_out_1.shape),
            const((1, d)),
        ],
        out_specs=pl.BlockSpec((1, tile, d), lambda b, j: (b, j, 0)),
        out_shape=jax.ShapeDtypeStruct(x.shape, F32),
        scratch_shapes=[
            pltpu.VMEM((2, BAND, GROUP_Q), F32),
            pltpu.VMEM((2, BAND, GROUP_Q), BF16),
            pltpu.VMEM((tile, C_WIDTH), F32),
            pltpu.VMEM((tile, C_WIDTH), BF16),
        ],
        compiler_params=pltpu.CompilerParams(
            dimension_semantics=("arbitrary", "arbitrary"), vmem_limit_bytes=VMEM_LIMIT),
        name="layer1_attention",
    )(sink_1, x, h, qt, ka, ka, ka, vt, vt, vt, w_gate, w_out_1, final_norm.reshape(1, d))


def kernel(x, norm_0, w_in_0, a_v_norm_0, a_spatial_w_0, a_spatial_b_0, b_group_w_0, b_scale_0,
           w_out_0, norm_1, w_in_1, sink_1, w_out_1, final_norm):
    x1, w_gate, w_out_1b, h1, qt, ka, vt = _layer0(
        x, norm_0, w_in_0, a_v_norm_0, a_spatial_w_0, a_spatial_b_0, b_group_w_0, b_scale_0,
        w_out_0, norm_1, w_in_1, w_out_1)
    return _attention(x1, h1, qt, ka, vt, w_gate, sink_1, w_out_1b, final_norm)
```

```python
import functools
import math

import jax
import jax.numpy as jnp
import numpy as np
from jax import lax
from jax.experimental import pallas as pl
from jax.experimental.pallas import tpu as pltpu

F32 = jnp.float32
BF16 = jnp.bfloat16

D_MODEL = 1024
EPS = 1e-6
NEG_INF = -1e30
LOG2_E = math.log2(math.e)
CHUNK = 128
A_GROUPS = 4
A_WIDTH = D_MODEL
A_GROUP_DIM = A_WIDTH // A_GROUPS
B_WIDTH = D_MODEL
POOL_WINDOWS = (2, 4, 8, 16)
B_GROUP_DIM = B_WIDTH // len(POOL_WINDOWS)
POOL_HALO = max(POOL_WINDOWS) // 2
N_HEADS = 16
N_KV_HEADS = 4
HEAD_DIM = 64
GQA_GROUP = N_HEADS // N_KV_HEADS
C_WIDTH = N_HEADS * HEAD_DIM
KV_WIDTH = N_KV_HEADS * HEAD_DIM
WINDOW = 128
ATTN_BLOCK = 128
BAND = 3 * ATTN_BLOCK
ROPE_THETA = 500000.0
ROT_DIM = HEAD_DIM // 4
LANES = 128
KV_PAD_WIDTH = N_KV_HEADS * LANES
GROUP_Q = GQA_GROUP * ATTN_BLOCK
HEADS_PER_HALF = GQA_GROUP // 2
HALF_Q = HEADS_PER_HALF * ATTN_BLOCK

QKV_COLS = C_WIDTH + 2 * KV_WIDTH
W1_CHUNK_ROWS = 128
W1_CHUNKS = D_MODEL // W1_CHUNK_ROWS

TILE_L0 = 512
TILE_ATTN = 512
VMEM_LIMIT = 56 * 1024 * 1024


def _rms(x, g):
    return x * lax.rsqrt(jnp.mean(x * x, axis=-1, keepdims=True) + EPS) * g


def _silu(x):
    return x / (1.0 + jnp.exp(-x))


def _dot(a, b):
    return jnp.dot(a, b, preferred_element_type=F32)


def _dot_nt(a, b):
    return lax.dot_general(a, b, (((1,), (1,)), ((), ())), preferred_element_type=F32)


def _layer0_kernel(x_ref, xp_ref, xn_ref, n0_ref, win_ref, gv_ref, ws_ref, bsb_ref, wg_ref,
                   sc_ref, wout_ref, w1_ref, wo1_ref, n1_ref, wqkv_ref, cost_ref, sint_ref,
                   o_ref, wgate_ref, wo1b_ref, h1_ref, qt_ref, ka_ref, vt_ref,
                   hext_ref, v_ref, cat_ref, wqkvt_ref, *, tile, seq):
    j = pl.program_id(1)
    nj = pl.num_programs(1)
    step = pl.program_id(0) * nj + j

    @pl.when(step == 0)
    def _():
        for c in range(W1_CHUNKS):
            rows = slice(c * W1_CHUNK_ROWS, (c + 1) * W1_CHUNK_ROWS)
            wqkvt_ref[:, rows] = wqkv_ref[rows, :].T.astype(BF16)

    @pl.when(step < W1_CHUNKS)
    def _():
        wgate_ref[...] = w1_ref[:, QKV_COLS:].astype(BF16)
        wo1b_ref[...] = wo1_ref[...].astype(BF16)

    n0 = n0_ref[...]
    x = x_ref[0]
    h = _rms(x, n0).astype(BF16)
    hext_ref[0:tile, :] = h
    hp = jnp.where(j > 0, _rms(xp_ref[0], n0), 0.0)
    hn = jnp.where(j < nj - 1, _rms(xn_ref[0], n0), 0.0)
    hext_ref[tile:tile + 2 * POOL_HALO, :] = jnp.concatenate([hp, hn], axis=0).astype(BF16)

    def col(k):
        return k * D_MODEL

    a_v = jax.nn.gelu(_dot(h, win_ref[:, col(1):col(2)]))
    v_ref[...] = _rms(a_v, gv_ref[...]).astype(BF16)
    for g in range(A_GROUPS):
        c0 = g * A_GROUP_DIM
        c1 = c0 + A_GROUP_DIM
        u = jax.nn.gelu(_dot(h, win_ref[:, col(0) + c0:col(0) + c1]))
        gate = _silu(_dot(h, win_ref[:, col(2) + c0:col(2) + c1]))
        ws_g = ws_ref[g].astype(BF16)
        for c in range(tile // CHUNK):
            r0 = c * CHUNK
            r1 = r0 + CHUNK
            mixed = _dot(ws_g, v_ref[r0:r1, c0:c1]) + bsb_ref[:, c0:c1]
            cat_ref[r0:r1, c0:c1] = (u[r0:r1] * mixed * gate[r0:r1]).astype(BF16)

    bxe = _dot(hext_ref[...], win_ref[:, col(3):col(4)])
    n_ext = tile + 2 * POOL_HALO
    edge_row = lax.broadcasted_iota(jnp.int32, (POOL_HALO, 1), 0)
    for gi, w in enumerate(POOL_WINDOWS):
        c0 = gi * B_GROUP_DIM
        c1 = c0 + B_GROUP_DIM
        half = w // 2
        e = jnp.concatenate([bxe[tile:tile + POOL_HALO, c0:c1], bxe[0:tile, c0:c1],
                             bxe[tile + POOL_HALO:, c0:c1]], axis=0)
        a = e
        span = 1
        while span < half:
            a = a + pltpu.roll(a, n_ext - span, 0)
            span *= 2
        win_sum = (a + pltpu.roll(a, half, 0))[POOL_HALO:POOL_HALO + tile]

        def edge_mean(slab, row0, half=half):
            t = j * tile + row0 + edge_row
            cnt = (jnp.minimum(t + half, seq) - jnp.maximum(t - half, 0)).astype(F32)
            return slab / cnt

        mean = jnp.concatenate(
            [edge_mean(win_sum[0:POOL_HALO], 0),
             win_sum[POOL_HALO:tile - POOL_HALO] * (1.0 / w),
             edge_mean(win_sum[tile - POOL_HALO:], tile - POOL_HALO)], axis=0)
        p = (mean - bxe[0:tile, c0:c1]).astype(BF16)
        gate = _silu(_dot(h, win_ref[:, col(4) + c0:col(4) + c1]))
        yb = _dot(p, wg_ref[gi].astype(BF16)) * sc_ref[:, c0:c1] * gate
        cat_ref[:, A_WIDTH + c0:A_WIDTH + c1] = yb.astype(BF16)

    n1 = n1_ref[...]
    hrows = tile // 2
    for r0 in range(0, tile, hrows):
        rows = slice(r0, r0 + hrows)
        out = x_ref[0, rows, :] + _dot(cat_ref[rows, :], wout_ref[...])
        o_ref[0, rows, :] = out
        h1_ref[0, rows, :] = _rms(out, n1).astype(BF16)
    _qkv_project(h1_ref[0], wqkvt_ref, cost_ref[...], sint_ref[...], qt_ref, ka_ref, vt_ref,
                 slice(0, tile))


def _layer0(x, norm_0, w_in_0, a_v_norm_0, a_spatial_w_0, a_spatial_b_0, b_group_w_0, b_scale_0,
            w_out_0, norm_1, w_in_1, w_out_1):
    bn, seq, d = x.shape
    tile = TILE_L0
    nt = seq // tile
    assert bn * nt >= W1_CHUNKS

    def w1_chunk(b, j):
        return jnp.minimum(b * nt + j, W1_CHUNKS - 1)
    halo_blocks = seq // POOL_HALO
    win = w_in_0.astype(BF16)
    wout = w_out_0.astype(BF16)
    ws = a_spatial_w_0
    wg = b_group_w_0
    cos_fm, sin_fm = _rope_tables(seq)
    bsb = jnp.repeat(a_spatial_b_0.T, A_GROUP_DIM, axis=1)

    def const(shape):
        return pl.BlockSpec(shape, lambda b, j: (0,) * len(shape), pipeline_mode=pl.Buffered(1))

    per_halo = tile // POOL_HALO
    kern = functools.partial(_layer0_kernel, tile=tile, seq=seq)
    return pl.pallas_call(
        kern,
        grid=(bn, nt),
        in_specs=[
            pl.BlockSpec((1, tile, d), lambda b, j: (b, j, 0)),
            pl.BlockSpec((1, POOL_HALO, d), lambda b, j: (b, jnp.maximum(j * per_halo - 1, 0), 0)),
            pl.BlockSpec((1, POOL_HALO, d),
                         lambda b, j: (b, jnp.minimum((j + 1) * per_halo, halo_blocks - 1), 0)),
            const((1, d)),
            const(win.shape),
            const((1, A_WIDTH)),
            const(ws.shape),
            const(bsb.shape),
            const(wg.shape),
            const((1, B_WIDTH)),
            const(wout.shape),
            pl.BlockSpec((W1_CHUNK_ROWS, w_in_1.shape[1]), lambda b, j: (w1_chunk(b, j), 0)),
            pl.BlockSpec((W1_CHUNK_ROWS, d), lambda b, j: (w1_chunk(b, j), 0)),
            const((1, d)),
            const((d, QKV_COLS)),
            pl.BlockSpec((ROT_DIM // 2, tile), lambda b, j: (0, j)),
            pl.BlockSpec((ROT_DIM // 2, tile), lambda b, j: (0, j)),
        ],
        out_specs=[
            pl.BlockSpec((1, tile, d), lambda b, j: (b, j, 0)),
            pl.BlockSpec((W1_CHUNK_ROWS, C_WIDTH), lambda b, j: (w1_chunk(b, j), 0)),
            pl.BlockSpec((W1_CHUNK_ROWS, d), lambda b, j: (w1_chunk(b, j), 0)),
            pl.BlockSpec((1, tile, d), lambda b, j: (b, j, 0)),
            pl.BlockSpec((1, C_WIDTH, tile), lambda b, j: (b, 0, j)),
            pl.BlockSpec((1, tile, KV_PAD_WIDTH), lambda b, j: (b, j, 0)),
            pl.BlockSpec((1, KV_WIDTH, tile), lambda b, j: (b, 0, j)),
        ],
        out_shape=[
            jax.ShapeDtypeStruct(x.shape, F32),
            jax.ShapeDtypeStruct((d, C_WIDTH), BF16),
            jax.ShapeDtypeStruct((C_WIDTH, d), BF16),
            jax.ShapeDtypeStruct((bn, seq, d), BF16),
            jax.ShapeDtypeStruct((bn, C_WIDTH, seq), BF16),
            jax.ShapeDtypeStruct((bn, seq, KV_PAD_WIDTH), BF16),
            jax.ShapeDtypeStruct((bn, KV_WIDTH, seq), BF16),
        ],
        scratch_shapes=[
            pltpu.VMEM((tile + 2 * POOL_HALO, d), BF16),
            pltpu.VMEM((tile, A_WIDTH), BF16),
            pltpu.VMEM((tile, A_WIDTH + B_WIDTH), BF16),
            pltpu.VMEM((QKV_COLS, d), BF16),
        ],
        compiler_params=pltpu.CompilerParams(
            dimension_semantics=("arbitrary", "arbitrary"), vmem_limit_bytes=VMEM_LIMIT),
        name="layer0_mixer",
    )(x, x, x, norm_0.reshape(1, d), win, a_v_norm_0.reshape(1, A_WIDTH), ws, bsb, wg,
      b_scale_0.reshape(1, B_WIDTH), wout, w_in_1, w_out_1, norm_1.reshape(1, d), w_in_1,
      cos_fm, sin_fm)


def _qkv_project(h, w_ref, ct, st, qt_ref, ka_ref, vt_ref, pos):
    half = ROT_DIM // 2
    n = h.shape[0]
    scale = HEAD_DIM ** -0.5 * LOG2_E

    qkv = _dot_nt(w_ref[...], h)

    def rope_rows(r0):
        x1 = qkv[r0:r0 + half]
        x2 = qkv[r0 + half:r0 + ROT_DIM]
        return jnp.concatenate([x1 * ct - x2 * st, x2 * ct + x1 * st], axis=0)

    for hd in range(N_HEADS):
        r0 = hd * HEAD_DIM
        qt_ref[0, r0:r0 + ROT_DIM, pos] = (rope_rows(r0) * scale).astype(BF16)
        qt_ref[0, r0 + ROT_DIM:r0 + HEAD_DIM, pos] = (
            qkv[r0 + ROT_DIM:r0 + HEAD_DIM] * scale).astype(BF16)
    vt_ref[0, :, pos] = qkv[C_WIDTH + KV_WIDTH:].astype(BF16)

    pad = jnp.zeros((LANES - HEAD_DIM, n), F32)
    for kh in range(N_KV_HEADS):
        r0 = C_WIDTH + kh * HEAD_DIM
        kt = jnp.concatenate([rope_rows(r0), qkv[r0 + ROT_DIM:r0 + HEAD_DIM], pad], axis=0)
        ka_ref[0, pos, kh * LANES:(kh + 1) * LANES] = kt.T.astype(BF16)


def _rope_tables(seq):
    inv = ROPE_THETA ** (-np.arange(0, ROT_DIM, 2, dtype=np.float64) / ROT_DIM)
    ang = np.arange(seq, dtype=np.float64)[None, :] * inv[:, None]
    return jnp.asarray(np.cos(ang), F32), jnp.asarray(np.sin(ang), F32)


def _attn_kernel(sink_ref, x_ref, h_ref, qt_ref, kap_ref, kat_ref, kan_ref, vtp_ref, vtt_ref,
                 vtn_ref, wgate_ref, wout_ref, fn_ref, o_ref,
                 s_ref, p_ref, sg_ref, y_ref, *, tile, seq):
    j = pl.program_id(1)
    nblk = tile // ATTN_BLOCK
    last_blk = seq // ATTN_BLOCK - 1

    def k_band(i, kh):
        cols = slice(kh * LANES, (kh + 1) * LANES)
        lo = max(i - 1, 0) * ATTN_BLOCK
        hi = min(i + 2, nblk) * ATTN_BLOCK
        parts = [kat_ref[0, lo:hi, cols]]
        if i == 0:
            parts.insert(0, kap_ref[0, :, cols])
        if i == nblk - 1:
            parts.append(kan_ref[0, :, cols])
        return jnp.concatenate(parts, axis=0)

    def vt_band(i, kh):
        rows = slice(kh * HEAD_DIM, (kh + 1) * HEAD_DIM)
        lo = max(i - 1, 0) * ATTN_BLOCK
        hi = min(i + 2, nblk) * ATTN_BLOCK
        parts = [vtt_ref[0, rows, lo:hi]]
        if i == 0:
            parts.insert(0, vtp_ref[0, rows, :])
        if i == nblk - 1:
            parts.append(vtn_ref[0, rows, :])
        return parts
    diff = (lax.broadcasted_iota(jnp.int32, (ATTN_BLOCK, ATTN_BLOCK), 0)
            - lax.broadcasted_iota(jnp.int32, (ATTN_BLOCK, ATTN_BLOCK), 1))
    never = 2 * ATTN_BLOCK
    ones_rows = jnp.ones((HEAD_DIM, BAND), BF16)

    def scores(u, half):
        i, kh = divmod(u, N_KV_HEADS)
        slot = u % 2
        c0 = i * ATTN_BLOCK
        h0 = kh * GQA_GROUP + half * HEADS_PER_HALF
        qg = jnp.concatenate(
            [qt_ref[0, (h0 + g) * HEAD_DIM:(h0 + g + 1) * HEAD_DIM, c0:c0 + ATTN_BLOCK]
             for g in range(HEADS_PER_HALF)], axis=1)
        rq = jnp.concatenate([qg, jnp.zeros((LANES - HEAD_DIM, HALF_Q), BF16)], axis=0)
        ka = k_band(i, kh)
        s_ref[slot, :, half * HALF_Q:(half + 1) * HALF_Q] = _dot(ka, rq)

    def softmax_head(u, g):
        i, kh = divmod(u, N_KV_HEADS)
        slot = u % 2
        gblk = j * nblk + i
        ok_prev = diff >= jnp.where(gblk > 0, 0, never)
        ok_next = diff <= jnp.where(gblk < last_blk, 0, -never)
        cs = slice(g * ATTN_BLOCK, (g + 1) * ATTN_BLOCK)
        sink = sink_ref[kh * GQA_GROUP + g] * LOG2_E
        s0 = jnp.where(ok_prev, s_ref[slot, 0:ATTN_BLOCK, cs], NEG_INF)
        s1 = s_ref[slot, ATTN_BLOCK:2 * ATTN_BLOCK, cs]
        s2 = jnp.where(ok_next, s_ref[slot, 2 * ATTN_BLOCK:, cs], NEG_INF)
        m = jnp.max(jnp.maximum(jnp.maximum(s0, s1), s2), axis=0, keepdims=True)
        m = jnp.maximum(m, sink)
        p0 = jnp.exp2(s0 - m)
        p1 = jnp.exp2(s1 - m)
        p2 = jnp.exp2(s2 - m)
        p_ref[slot, 0:ATTN_BLOCK, cs] = p0.astype(BF16)
        p_ref[slot, ATTN_BLOCK:2 * ATTN_BLOCK, cs] = p1.astype(BF16)
        p_ref[slot, 2 * ATTN_BLOCK:, cs] = p2.astype(BF16)
        return jnp.exp2(sink - m)

    def weighted(u, half, sink_share):
        i, kh = divmod(u, N_KV_HEADS)
        slot = u % 2
        c0 = i * ATTN_BLOCK
        vt = jnp.concatenate([jnp.concatenate(vt_band(i, kh), axis=1), ones_rows],
                             axis=0)
        ot = _dot(vt, p_ref[slot, :, half * HALF_Q:(half + 1) * HALF_Q])
        parts = []
        for g in range(HEADS_PER_HALF):
            cq = slice(g * ATTN_BLOCK, (g + 1) * ATTN_BLOCK)
            l = ot[HEAD_DIM:HEAD_DIM + 1, cq] + sink_share[g]
            parts.append(ot[0:HEAD_DIM, cq] * (1.0 / l))
        yt = jnp.concatenate(parts, axis=0)
        f0 = (kh * GQA_GROUP + half * HEADS_PER_HALF) * HEAD_DIM
        cs = slice(f0, f0 + HEADS_PER_HALF * HEAD_DIM)
        y = yt.T * sg_ref[c0:c0 + ATTN_BLOCK, cs]
        y_ref[c0:c0 + ATTN_BLOCK, cs] = y.astype(BF16)

    def gate_chunk(kh):
        cs = slice(kh * GQA_GROUP * HEAD_DIM, (kh + 1) * GQA_GROUP * HEAD_DIM)
        sg_ref[:, cs] = _silu(_dot(h_ref[0], wgate_ref[:, cs]))

    def finish_rows(r0, n):
        rows = slice(r0, r0 + n)
        out = x_ref[0, rows, :] + _dot(y_ref[rows, :], wout_ref[...])
        o_ref[0, rows, :] = _rms(out, fn_ref[...])

    units = nblk * N_KV_HEADS
    inv = {}
    for step in range(units + 2):
        sc = step if step < units else None
        sm = step - 1 if 1 <= step <= units else None
        wt = step - 2 if step >= 2 else None
        if step < N_KV_HEADS:
            gate_chunk(step)
        for half in range(2):
            if sc is not None:
                scores(sc, half)
            if sm is not None:
                inv[sm, 2 * half] = softmax_head(sm, 2 * half)
            if wt is not None:
                weighted(wt, half, [inv.pop((wt, half * HEADS_PER_HALF + g))
                                    for g in range(HEADS_PER_HALF)])
            if sm is not None:
                inv[sm, 2 * half + 1] = softmax_head(sm, 2 * half + 1)
        if wt is not None and (wt + 1) % (units // 2) == 0:
            finish_rows((wt + 1 - units // 2) // N_KV_HEADS * ATTN_BLOCK, tile // 2)


def _attention(x, h, qt, ka, vt, w_gate, sink_1, w_out_1, final_norm):
    bn, seq, d = x.shape
    tile = TILE_ATTN
    nt = seq // tile
    per = tile // ATTN_BLOCK
    nb = seq // ATTN_BLOCK

    def const(shape):
        return pl.BlockSpec(shape, lambda b, j: (0,) * len(shape), pipeline_mode=pl.Buffered(1))

    def prev_blk(j):
        return jnp.maximum(j * per - 1, 0)

    def next_blk(j):
        return jnp.minimum((j + 1) * per, nb - 1)

    kern = functools.partial(_attn_kernel, tile=tile, seq=seq)
    return pl.pallas_call(
        kern,
        grid=(bn, nt),
        in_specs=[
            pl.BlockSpec(memory_space=pltpu.SMEM),
            pl.BlockSpec((1, tile, d), lambda b, j: (b, j, 0)),
            pl.BlockSpec((1, tile, d), lambda b, j: (b, j, 0)),
            pl.BlockSpec((1, C_WIDTH, tile), lambda b, j: (b, 0, j)),
            pl.BlockSpec((1, ATTN_BLOCK, KV_PAD_WIDTH), lambda b, j: (b, prev_blk(j), 0)),
            pl.BlockSpec((1, tile, KV_PAD_WIDTH), lambda b, j: (b, j, 0)),
            pl.BlockSpec((1, ATTN_BLOCK, KV_PAD_WIDTH), lambda b, j: (b, next_blk(j), 0)),
            pl.BlockSpec((1, KV_WIDTH, ATTN_BLOCK), lambda b, j: (b, 0, prev_blk(j))),
            pl.BlockSpec((1, KV_WIDTH, tile), lambda b, j: (b, 0, j)),
            pl.BlockSpec((1, KV_WIDTH, ATTN_BLOCK), lambda b, j: (b, 0, next_blk(j))),
            const(w_gate.shape),
            const(w_out_1.shape),
            const((1, d)),
        ],
        out_specs=pl.BlockSpec((1, tile, d), lambda b, j: (b, j, 0)),
        out_shape=jax.ShapeDtypeStruct(x.shape, F32),
        scratch_shapes=[
            pltpu.VMEM((2, BAND, GROUP_Q), F32),
            pltpu.VMEM((2, BAND, GROUP_Q), BF16),
            pltpu.VMEM((tile, C_WIDTH), F32),
            pltpu.VMEM((tile, C_WIDTH), BF16),
        ],
        compiler_params=pltpu.CompilerParams(
            dimension_semantics=("arbitrary", "arbitrary"), vmem_limit_bytes=VMEM_LIMIT),
        name="layer1_attention",
    )(sink_1, x, h, qt, ka, ka, ka, vt, vt, vt, w_gate, w_out_1, final_norm.reshape(1, d))


def kernel(x, norm_0, w_in_0, a_v_norm_0, a_spatial_w_0, a_spatial_b_0, b_group_w_0, b_scale_0,
           w_out_0, norm_1, w_in_1, sink_1, w_out_1, final_norm):
    x1, w_gate, w_out_1b, h1, qt, ka, vt = _layer0(
        x, norm_0, w_in_0, a_v_norm_0, a_spatial_w_0, a_spatial_b_0, b_group_w_0, b_scale_0,
        w_out_0, norm_1, w_in_1, w_out_1)
    return _attention(x1, h1, qt, ka, vt, w_gate, sink_1, w_out_1b, final_norm)
```

```python
import functools
import math

import jax
import jax.numpy as jnp
import numpy as np
from jax import lax
from jax.experimental import pallas as pl
from jax.experimental.pallas import tpu as pltpu

F32 = jnp.float32
BF16 = jnp.bfloat16

D_MODEL = 1024
EPS = 1e-6
NEG_INF = -1e30
LOG2_E = math.log2(math.e)
CHUNK = 128
A_GROUPS = 4
A_WIDTH = D_MODEL
A_GROUP_DIM = A_WIDTH // A_GROUPS
B_WIDTH = D_MODEL
POOL_WINDOWS = (2, 4, 8, 16)
B_GROUP_DIM = B_WIDTH // len(POOL_WINDOWS)
POOL_HALO = max(POOL_WINDOWS) // 2
N_HEADS = 16
N_KV_HEADS = 4
HEAD_DIM = 64
GQA_GROUP = N_HEADS // N_KV_HEADS
C_WIDTH = N_HEADS * HEAD_DIM
KV_WIDTH = N_KV_HEADS * HEAD_DIM
WINDOW = 128
ATTN_BLOCK = 128
BAND = 3 * ATTN_BLOCK
ROPE_THETA = 500000.0
ROT_DIM = HEAD_DIM // 4
LANES = 128
KV_PAD_WIDTH = N_KV_HEADS * LANES
GROUP_Q = GQA_GROUP * ATTN_BLOCK
HEADS_PER_HALF = GQA_GROUP // 2
HALF_Q = HEADS_PER_HALF * ATTN_BLOCK

QKV_COLS = C_WIDTH + 2 * KV_WIDTH
W1_CHUNK_ROWS = 128
W1_CHUNKS = D_MODEL // W1_CHUNK_ROWS

TILE_L0 = 512
TILE_ATTN = 1024
VMEM_LIMIT = 56 * 1024 * 1024


def _rms(x, g):
    return x * lax.rsqrt(jnp.mean(x * x, axis=-1, keepdims=True) + EPS) * g


def _silu(x):
    return x / (1.0 + jnp.exp(-x))


def _dot(a, b):
    return jnp.dot(a, b, preferred_element_type=F32)


def _dot_nt(a, b):
    return lax.dot_general(a, b, (((1,), (1,)), ((), ())), preferred_element_type=F32)


def _layer0_kernel(x_ref, xp_ref, xn_ref, n0_ref, win_ref, gv_ref, ws_ref, bsb_ref, wg_ref,
                   sc_ref, wout_ref, w1_ref, wo1_ref, n1_ref, wqkv_ref, cost_ref, sint_ref,
                   o_ref, wgate_ref, wo1b_ref, h1_ref, qt_ref, ka_ref, vt_ref,
                   hext_ref, v_ref, cat_ref, wqkvt_ref, *, tile, seq):
    j = pl.program_id(1)
    nj = pl.num_programs(1)
    step = pl.program_id(0) * nj + j

    @pl.when(step == 0)
    def _():
        for c in range(W1_CHUNKS):
            rows = slice(c * W1_CHUNK_ROWS, (c + 1) * W1_CHUNK_ROWS)
            wqkvt_ref[:, rows] = wqkv_ref[rows, :].T.astype(BF16)

    @pl.when(step < W1_CHUNKS)
    def _():
        wgate_ref[...] = w1_ref[:, QKV_COLS:].astype(BF16)
        wo1b_ref[...] = wo1_ref[...].astype(BF16)

    n0 = n0_ref[...]
    x = x_ref[0]
    h = _rms(x, n0).astype(BF16)
    hext_ref[0:tile, :] = h
    hp = jnp.where(j > 0, _rms(xp_ref[0], n0), 0.0)
    hn = jnp.where(j < nj - 1, _rms(xn_ref[0], n0), 0.0)
    hext_ref[tile:tile + 2 * POOL_HALO, :] = jnp.concatenate([hp, hn], axis=0).astype(BF16)

    def col(k):
        return k * D_MODEL

    a_v = jax.nn.gelu(_dot(h, win_ref[:, col(1):col(2)]))
    v_ref[...] = _rms(a_v, gv_ref[...]).astype(BF16)
    for g in range(A_GROUPS):
        c0 = g * A_GROUP_DIM
        c1 = c0 + A_GROUP_DIM
        u = jax.nn.gelu(_dot(h, win_ref[:, col(0) + c0:col(0) + c1]))
        gate = _silu(_dot(h, win_ref[:, col(2) + c0:col(2) + c1]))
        ws_g = ws_ref[g].astype(BF16)
        for c in range(tile // CHUNK):
            r0 = c * CHUNK
            r1 = r0 + CHUNK
            mixed = _dot(ws_g, v_ref[r0:r1, c0:c1]) + bsb_ref[:, c0:c1]
            cat_ref[r0:r1, c0:c1] = (u[r0:r1] * mixed * gate[r0:r1]).astype(BF16)

    bxe = _dot(hext_ref[...], win_ref[:, col(3):col(4)])
    n_ext = tile + 2 * POOL_HALO
    edge_row = lax.broadcasted_iota(jnp.int32, (POOL_HALO, 1), 0)
    for gi, w in enumerate(POOL_WINDOWS):
        c0 = gi * B_GROUP_DIM
        c1 = c0 + B_GROUP_DIM
        half = w // 2
        e = jnp.concatenate([bxe[tile:tile + POOL_HALO, c0:c1], bxe[0:tile, c0:c1],
                             bxe[tile + POOL_HALO:, c0:c1]], axis=0)
        a = e
        span = 1
        while span < half:
            a = a + pltpu.roll(a, n_ext - span, 0)
            span *= 2
        win_sum = (a + pltpu.roll(a, half, 0))[POOL_HALO:POOL_HALO + tile]

        def edge_mean(slab, row0, half=half):
            t = j * tile + row0 + edge_row
            cnt = (jnp.minimum(t + half, seq) - jnp.maximum(t - half, 0)).astype(F32)
            return slab / cnt

        mean = jnp.concatenate(
            [edge_mean(win_sum[0:POOL_HALO], 0),
             win_sum[POOL_HALO:tile - POOL_HALO] * (1.0 / w),
             edge_mean(win_sum[tile - POOL_HALO:], tile - POOL_HALO)], axis=0)
        p = (mean - bxe[0:tile, c0:c1]).astype(BF16)
        gate = _silu(_dot(h, win_ref[:, col(4) + c0:col(4) + c1]))
        yb = _dot(p, wg_ref[gi].astype(BF16)) * sc_ref[:, c0:c1] * gate
        cat_ref[:, A_WIDTH + c0:A_WIDTH + c1] = yb.astype(BF16)

    n1 = n1_ref[...]
    hrows = tile // 2
    for r0 in range(0, tile, hrows):
        rows = slice(r0, r0 + hrows)
        out = x_ref[0, rows, :] + _dot(cat_ref[rows, :], wout_ref[...])
        o_ref[0, rows, :] = out
        h1_ref[0, rows, :] = _rms(out, n1).astype(BF16)
    _qkv_project(h1_ref[0], wqkvt_ref, cost_ref[...], sint_ref[...], qt_ref, ka_ref, vt_ref,
                 slice(0, tile))


def _layer0(x, norm_0, w_in_0, a_v_norm_0, a_spatial_w_0, a_spatial_b_0, b_group_w_0, b_scale_0,
            w_out_0, norm_1, w_in_1, w_out_1):
    bn, seq, d = x.shape
    tile = TILE_L0
    nt = seq // tile
    assert bn * nt >= W1_CHUNKS

    def w1_chunk(b, j):
        return jnp.minimum(b * nt + j, W1_CHUNKS - 1)
    halo_blocks = seq // POOL_HALO
    win = w_in_0.astype(BF16)
    wout = w_out_0.astype(BF16)
    ws = a_spatial_w_0
    wg = b_group_w_0
    cos_fm, sin_fm = _rope_tables(seq)
    bsb = jnp.repeat(a_spatial_b_0.T, A_GROUP_DIM, axis=1)

    def const(shape):
        return pl.BlockSpec(shape, lambda b, j: (0,) * len(shape), pipeline_mode=pl.Buffered(1))

    per_halo = tile // POOL_HALO
    kern = functools.partial(_layer0_kernel, tile=tile, seq=seq)
    return pl.pallas_call(
        kern,
        grid=(bn, nt),
        in_specs=[
            pl.BlockSpec((1, tile, d), lambda b, j: (b, j, 0)),
            pl.BlockSpec((1, POOL_HALO, d), lambda b, j: (b, jnp.maximum(j * per_halo - 1, 0), 0)),
            pl.BlockSpec((1, POOL_HALO, d),
                         lambda b, j: (b, jnp.minimum((j + 1) * per_halo, halo_blocks - 1), 0)),
            const((1, d)),
            const(win.shape),
            const((1, A_WIDTH)),
            const(ws.shape),
            const(bsb.shape),
            const(wg.shape),
            const((1, B_WIDTH)),
            const(wout.shape),
            pl.BlockSpec((W1_CHUNK_ROWS, w_in_1.shape[1]), lambda b, j: (w1_chunk(b, j), 0)),
            pl.BlockSpec((W1_CHUNK_ROWS, d), lambda b, j: (w1_chunk(b, j), 0)),
            const((1, d)),
            const((d, QKV_COLS)),
            pl.BlockSpec((ROT_DIM // 2, tile), lambda b, j: (0, j)),
            pl.BlockSpec((ROT_DIM // 2, tile), lambda b, j: (0, j)),
        ],
        out_specs=[
            pl.BlockSpec((1, tile, d), lambda b, j: (b, j, 0)),
            pl.BlockSpec((W1_CHUNK_ROWS, C_WIDTH), lambda b, j: (w1_chunk(b, j), 0)),
            pl.BlockSpec((W1_CHUNK_ROWS, d), lambda b, j: (w1_chunk(b, j), 0)),
            pl.BlockSpec((1, tile, d), lambda b, j: (b, j, 0)),
            pl.BlockSpec((1, C_WIDTH, tile), lambda b, j: (b, 0, j)),
            pl.BlockSpec((1, tile, KV_PAD_WIDTH), lambda b, j: (b, j, 0)),
            pl.BlockSpec((1, KV_WIDTH, tile), lambda b, j: (b, 0, j)),
        ],
        out_shape=[
            jax.ShapeDtypeStruct(x.shape, F32),
            jax.ShapeDtypeStruct((d, C_WIDTH), BF16),
            jax.ShapeDtypeStruct((C_WIDTH, d), BF16),
            jax.ShapeDtypeStruct((bn, seq, d), BF16),
            jax.ShapeDtypeStruct((bn, C_WIDTH, seq), BF16),
            jax.ShapeDtypeStruct((bn, seq, KV_PAD_WIDTH), BF16),
            jax.ShapeDtypeStruct((bn, KV_WIDTH, seq), BF16),
        ],
        scratch_shapes=[
            pltpu.VMEM((tile + 2 * POOL_HALO, d), BF16),
            pltpu.VMEM((tile, A_WIDTH), BF16),
            pltpu.VMEM((tile, A_WIDTH + B_WIDTH), BF16),
            pltpu.VMEM((QKV_COLS, d), BF16),
        ],
        compiler_params=pltpu.CompilerParams(
            dimension_semantics=("arbitrary", "arbitrary"), vmem_limit_bytes=VMEM_LIMIT),
        name="layer0_mixer",
    )(x, x, x, norm_0.reshape(1, d), win, a_v_norm_0.reshape(1, A_WIDTH), ws, bsb, wg,
      b_scale_0.reshape(1, B_WIDTH), wout, w_in_1, w_out_1, norm_1.reshape(1, d), w_in_1,
      cos_fm, sin_fm)


def _qkv_project(h, w_ref, ct, st, qt_ref, ka_ref, vt_ref, pos):
    half = ROT_DIM // 2
    n = h.shape[0]
    scale = HEAD_DIM ** -0.5 * LOG2_E

    qkv = _dot_nt(w_ref[...], h)

    def rope_rows(r0):
        x1 = qkv[r0:r0 + half]
        x2 = qkv[r0 + half:r0 + ROT_DIM]
        return jnp.concatenate([x1 * ct - x2 * st, x2 * ct + x1 * st], axis=0)

    for hd in range(N_HEADS):
        r0 = hd * HEAD_DIM
        qt_ref[0, r0:r0 + ROT_DIM, pos] = (rope_rows(r0) * scale).astype(BF16)
        qt_ref[0, r0 + ROT_DIM:r0 + HEAD_DIM, pos] = (
            qkv[r0 + ROT_DIM:r0 + HEAD_DIM] * scale).astype(BF16)
    vt_ref[0, :, pos] = qkv[C_WIDTH + KV_WIDTH:].astype(BF16)

    pad = jnp.zeros((LANES - HEAD_DIM, n), F32)
    for kh in range(N_KV_HEADS):
        r0 = C_WIDTH + kh * HEAD_DIM
        kt = jnp.concatenate([rope_rows(r0), qkv[r0 + ROT_DIM:r0 + HEAD_DIM], pad], axis=0)
        ka_ref[0, pos, kh * LANES:(kh + 1) * LANES] = kt.T.astype(BF16)


def _rope_tables(seq):
    inv = ROPE_THETA ** (-np.arange(0, ROT_DIM, 2, dtype=np.float64) / ROT_DIM)
    ang = np.arange(seq, dtype=np.float64)[None, :] * inv[:, None]
    return jnp.asarray(np.cos(ang), F32), jnp.asarray(np.sin(ang), F32)


def _attn_kernel(sink_ref, x_ref, h_ref, qt_ref, kap_ref, kat_ref, kan_ref, vtp_ref, vtt_ref,
                 vtn_ref, wgate_ref, wout_ref, fn_ref, o_ref,
                 s_ref, p_ref, sg_ref, y_ref, *, tile, seq):
    j = pl.program_id(1)
    nblk = tile // ATTN_BLOCK
    last_blk = seq // ATTN_BLOCK - 1

    def k_band(i, kh):
        cols = slice(kh * LANES, (kh + 1) * LANES)
        lo = max(i - 1, 0) * ATTN_BLOCK
        hi = min(i + 2, nblk) * ATTN_BLOCK
        parts = [kat_ref[0, lo:hi, cols]]
        if i == 0:
            parts.insert(0, kap_ref[0, :, cols])
        if i == nblk - 1:
            parts.append(kan_ref[0, :, cols])
        return jnp.concatenate(parts, axis=0)

    def vt_band(i, kh):
        rows = slice(kh * HEAD_DIM, (kh + 1) * HEAD_DIM)
        lo = max(i - 1, 0) * ATTN_BLOCK
        hi = min(i + 2, nblk) * ATTN_BLOCK
        parts = [vtt_ref[0, rows, lo:hi]]
        if i == 0:
            parts.insert(0, vtp_ref[0, rows, :])
        if i == nblk - 1:
            parts.append(vtn_ref[0, rows, :])
        return parts
    diff = (lax.broadcasted_iota(jnp.int32, (ATTN_BLOCK, ATTN_BLOCK), 0)
            - lax.broadcasted_iota(jnp.int32, (ATTN_BLOCK, ATTN_BLOCK), 1))
    never = 2 * ATTN_BLOCK
    ones_rows = jnp.ones((HEAD_DIM, BAND), BF16)

    def scores(u, half):
        i, kh = divmod(u, N_KV_HEADS)
        slot = u % 2
        c0 = i * ATTN_BLOCK
        h0 = kh * GQA_GROUP + half * HEADS_PER_HALF
        qg = jnp.concatenate(
            [qt_ref[0, (h0 + g) * HEAD_DIM:(h0 + g + 1) * HEAD_DIM, c0:c0 + ATTN_BLOCK]
             for g in range(HEADS_PER_HALF)], axis=1)
        rq = jnp.concatenate([qg, jnp.zeros((LANES - HEAD_DIM, HALF_Q), BF16)], axis=0)
        ka = k_band(i, kh)
        s_ref[slot, :, half * HALF_Q:(half + 1) * HALF_Q] = _dot(ka, rq)

    def softmax_head(u, g):
        i, kh = divmod(u, N_KV_HEADS)
        slot = u % 2
        gblk = j * nblk + i
        ok_prev = diff >= jnp.where(gblk > 0, 0, never)
        ok_next = diff <= jnp.where(gblk < last_blk, 0, -never)
        cs = slice(g * ATTN_BLOCK, (g + 1) * ATTN_BLOCK)
        sink = sink_ref[kh * GQA_GROUP + g] * LOG2_E
        s0 = jnp.where(ok_prev, s_ref[slot, 0:ATTN_BLOCK, cs], NEG_INF)
        s1 = s_ref[slot, ATTN_BLOCK:2 * ATTN_BLOCK, cs]
        s2 = jnp.where(ok_next, s_ref[slot, 2 * ATTN_BLOCK:, cs], NEG_INF)
        m = jnp.max(jnp.maximum(jnp.maximum(s0, s1), s2), axis=0, keepdims=True)
        m = jnp.maximum(m, sink)
        p0 = jnp.exp2(s0 - m)
        p1 = jnp.exp2(s1 - m)
        p2 = jnp.exp2(s2 - m)
        p_ref[slot, 0:ATTN_BLOCK, cs] = p0.astype(BF16)
        p_ref[slot, ATTN_BLOCK:2 * ATTN_BLOCK, cs] = p1.astype(BF16)
        p_ref[slot, 2 * ATTN_BLOCK:, cs] = p2.astype(BF16)
        return jnp.exp2(sink - m)

    def weighted(u, half, sink_share):
        i, kh = divmod(u, N_KV_HEADS)
        slot = u % 2
        c0 = i * ATTN_BLOCK
        vt = jnp.concatenate([jnp.concatenate(vt_band(i, kh), axis=1), ones_rows],
                             axis=0)
        ot = _dot(vt, p_ref[slot, :, half * HALF_Q:(half + 1) * HALF_Q])
        parts = []
        for g in range(HEADS_PER_HALF):
            cq = slice(g * ATTN_BLOCK, (g + 1) * ATTN_BLOCK)
            l = ot[HEAD_DIM:HEAD_DIM + 1, cq] + sink_share[g]
            parts.append(ot[0:HEAD_DIM, cq] * (1.0 / l))
        yt = jnp.concatenate(parts, axis=0)
        f0 = (kh * GQA_GROUP + half * HEADS_PER_HALF) * HEAD_DIM
        cs = slice(f0, f0 + HEADS_PER_HALF * HEAD_DIM)
        y = yt.T * sg_ref[c0:c0 + ATTN_BLOCK, cs]
        y_ref[c0:c0 + ATTN_BLOCK, cs] = y.astype(BF16)

    def gate_chunk(kh):
        cs = slice(kh * GQA_GROUP * HEAD_DIM, (kh + 1) * GQA_GROUP * HEAD_DIM)
        sg_ref[:, cs] = _silu(_dot(h_ref[0], wgate_ref[:, cs]))

    def finish_rows(r0, n):
        rows = slice(r0, r0 + n)
        out = x_ref[0, rows, :] + _dot(y_ref[rows, :], wout_ref[...])
        o_ref[0, rows, :] = _rms(out, fn_ref[...])

    units = nblk * N_KV_HEADS
    inv = {}
    for step in range(units + 2):
        sc = step if step < units else None
        sm = step - 1 if 1 <= step <= units else None
        wt = step - 2 if step >= 2 else None
        if step < N_KV_HEADS:
            gate_chunk(step)
        for half in range(2):
            if sc is not None:
                scores(sc, half)
            if sm is not None:
                inv[sm, 2 * half] = softmax_head(sm, 2 * half)
            if wt is not None:
                weighted(wt, half, [inv.pop((wt, half * HEADS_PER_HALF + g))
                                    for g in range(HEADS_PER_HALF)])
            if sm is not None:
                inv[sm, 2 * half + 1] = softmax_head(sm, 2 * half + 1)
        if wt is not None and (wt + 1) % (units // 2) == 0:
            finish_rows((wt + 1 - units // 2) // N_KV_HEADS * ATTN_BLOCK, tile // 2)


def _attention(x, h, qt, ka, vt, w_gate, sink_1, w_out_1, final_norm):
    bn, seq, d = x.shape
    tile = TILE_ATTN
    nt = seq // tile
    per = tile // ATTN_BLOCK
    nb = seq // ATTN_BLOCK

    def const(shape):
        return pl.BlockSpec(shape, lambda b, j: (0,) * len(shape), pipeline_mode=pl.Buffered(1))

    def prev_blk(j):
        return jnp.maximum(j * per - 1, 0)

    def next_blk(j):
        return jnp.minimum((j + 1) * per, nb - 1)

    kern = functools.partial(_attn_kernel, tile=tile, seq=seq)
    return pl.pallas_call(
        kern,
        grid=(bn, nt),
        in_specs=[
            pl.BlockSpec(memory_space=pltpu.SMEM),
            pl.BlockSpec((1, tile, d), lambda b, j: (b, j, 0)),
            pl.BlockSpec((1, tile, d), lambda b, j: (b, j, 0)),
            pl.BlockSpec((1, C_WIDTH, tile), lambda b, j: (b, 0, j)),
            pl.BlockSpec((1, ATTN_BLOCK, KV_PAD_WIDTH), lambda b, j: (b, prev_blk(j), 0)),
            pl.BlockSpec((1, tile, KV_PAD_WIDTH), lambda b, j: (b, j, 0)),
            pl.BlockSpec((1, ATTN_BLOCK, KV_PAD_WIDTH), lambda b, j: (b, next_blk(j), 0)),
            pl.BlockSpec((1, KV_WIDTH, ATTN_BLOCK), lambda b, j: (b, 0, prev_blk(j))),
            pl.BlockSpec((1, KV_WIDTH, tile), lambda b, j: (b, 0, j)),
            pl.BlockSpec((1, KV_WIDTH, ATTN_BLOCK), lambda b, j: (b, 0, next_blk(j))),
            const(w_gate.shape),
            const(w_out_1.shape),
            const((1, d)),
        ],
        out_specs=pl.BlockSpec((1, tile, d), lambda b, j: (b, j, 0)),
        out_shape=jax.ShapeDtypeStruct(x.shape, F32),
        scratch_shapes=[
            pltpu.VMEM((2, BAND, GROUP_Q), F32),
            pltpu.VMEM((2, BAND, GROUP_Q), BF16),
            pltpu.VMEM((tile, C_WIDTH), F32),
            pltpu.VMEM((tile, C_WIDTH), BF16),
        ],
        compiler_params=pltpu.CompilerParams(
            dimension_semantics=("arbitrary", "arbitrary"), vmem_limit_bytes=VMEM_LIMIT),
        name="layer1_attention",
    )(sink_1, x, h, qt, ka, ka, ka, vt, vt, vt, w_gate, w_out_1, final_norm.reshape(1, d))


def kernel(x, norm_0, w_in_0, a_v_norm_0, a_spatial_w_0, a_spatial_b_0, b_group_w_0, b_scale_0,
           w_out_0, norm_1, w_in_1, sink_1, w_out_1, final_norm):
    x1, w_gate, w_out_1b, h1, qt, ka, vt = _layer0(
        x, norm_0, w_in_0, a_v_norm_0, a_spatial_w_0, a_spatial_b_0, b_group_w_0, b_scale_0,
        w_out_0, norm_1, w_in_1, w_out_1)
    return _attention(x1, h1, qt, ka, vt, w_gate, sink_1, w_out_1b, final_norm)
```

```python
import functools
import math

import jax
import jax.numpy as jnp
import numpy as np
from jax import lax
from jax.experimental import pallas as pl
from jax.experimental.pallas import tpu as pltpu

F32 = jnp.float32
BF16 = jnp.bfloat16

D_MODEL = 1024
EPS = 1e-6
NEG_INF = -1e30
LOG2_E = math.log2(math.e)
CHUNK = 128
A_GROUPS = 4
A_WIDTH = D_MODEL
A_GROUP_DIM = A_WIDTH // A_GROUPS
B_WIDTH = D_MODEL
POOL_WINDOWS = (2, 4, 8, 16)
B_GROUP_DIM = B_WIDTH // len(POOL_WINDOWS)
POOL_HALO = max(POOL_WINDOWS) // 2
N_HEADS = 16
N_KV_HEADS = 4
HEAD_DIM = 64
GQA_GROUP = N_HEADS // N_KV_HEADS
C_WIDTH = N_HEADS * HEAD_DIM
KV_WIDTH = N_KV_HEADS * HEAD_DIM
WINDOW = 128
ATTN_BLOCK = 128
BAND = 3 * ATTN_BLOCK
ROPE_THETA = 500000.0
ROT_DIM = HEAD_DIM // 4
LANES = 128
KV_PAD_WIDTH = N_KV_HEADS * LANES
GROUP_Q = GQA_GROUP * ATTN_BLOCK
HEADS_PER_HALF = GQA_GROUP // 2
HALF_Q = HEADS_PER_HALF * ATTN_BLOCK

QKV_COLS = C_WIDTH + 2 * KV_WIDTH
W1_CHUNK_ROWS = 128
W1_CHUNKS = D_MODEL // W1_CHUNK_ROWS
W0_STAGE_CHUNK = (1024, 512)

TILE_L0 = 512
TILE_ATTN = 512
VMEM_LIMIT = 56 * 1024 * 1024


def _rms(x, g):
    return x * lax.rsqrt(jnp.mean(x * x, axis=-1, keepdims=True) + EPS) * g


def _silu(x):
    return x / (1.0 + jnp.exp(-x))


def _dot(a, b):
    return jnp.dot(a, b, preferred_element_type=F32)


def _dot_nt(a, b):
    return lax.dot_general(a, b, (((1,), (1,)), ((), ())), preferred_element_type=F32)


def _layer0_kernel(x_ref, xp_ref, xn_ref, n0_ref, win_hbm, gv_ref, ws_ref, bsb_ref, wg_ref,
                   sc_ref, wout_hbm, w1_ref, wo1_ref, n1_ref, wqkv_ref, cost_ref, sint_ref,
                   o_ref, wgate_ref, wo1b_ref, h1_ref, qt_ref, ka_ref, vt_ref,
                   hext_ref, v_ref, cat_ref, wqkvt_ref, win_ref, wout_ref, stage_ref, wsem,
                   *, tile, seq):
    j = pl.program_id(1)
    nj = pl.num_programs(1)
    step = pl.program_id(0) * nj + j

    @pl.when(step == 0)
    def _():
        cr, cc = stage_ref.shape[1:]
        chunks = [(win_hbm, win_ref, r0, c0)
                  for r0 in range(0, win_ref.shape[0], cr) for c0 in range(0, win_ref.shape[1], cc)]
        chunks += [(wout_hbm, wout_ref, r0, c0)
                   for r0 in range(0, wout_ref.shape[0], cr)
                   for c0 in range(0, wout_ref.shape[1], cc)]

        def chunk_copy(k):
            src, _, r0, c0 = chunks[k]
            return pltpu.make_async_copy(src.at[r0:r0 + cr, c0:c0 + cc], stage_ref.at[k % 2],
                                         wsem.at[k % 2])

        chunk_copy(0).start()
        for k, (_, dst, r0, c0) in enumerate(chunks):
            if k + 1 < len(chunks):
                chunk_copy(k + 1).start()
            chunk_copy(k).wait()
            dst[r0:r0 + cr, c0:c0 + cc] = stage_ref[k % 2].astype(BF16)

    @pl.when(step == 0)
    def _():
        for c in range(W1_CHUNKS):
            rows = slice(c * W1_CHUNK_ROWS, (c + 1) * W1_CHUNK_ROWS)
            wqkvt_ref[:, rows] = wqkv_ref[rows, :].T.astype(BF16)

    @pl.when(step < W1_CHUNKS)
    def _():
        wgate_ref[...] = w1_ref[:, QKV_COLS:].astype(BF16)
        wo1b_ref[...] = wo1_ref[...].astype(BF16)

    n0 = n0_ref[...]
    x = x_ref[0]
    h = _rms(x, n0).astype(BF16)
    hext_ref[0:tile, :] = h
    hp = jnp.where(j > 0, _rms(xp_ref[0], n0), 0.0)
    hn = jnp.where(j < nj - 1, _rms(xn_ref[0], n0), 0.0)
    hext_ref[tile:tile + 2 * POOL_HALO, :] = jnp.concatenate([hp, hn], axis=0).astype(BF16)

    def col(k):
        return k * D_MODEL

    a_v = jax.nn.gelu(_dot(h, win_ref[:, col(1):col(2)]))
    v_ref[...] = _rms(a_v, gv_ref[...]).astype(BF16)
    for g in range(A_GROUPS):
        c0 = g * A_GROUP_DIM
        c1 = c0 + A_GROUP_DIM
        u = jax.nn.gelu(_dot(h, win_ref[:, col(0) + c0:col(0) + c1]))
        gate = _silu(_dot(h, win_ref[:, col(2) + c0:col(2) + c1]))
        ws_g = ws_ref[g].astype(BF16)
        for c in range(tile // CHUNK):
            r0 = c * CHUNK
            r1 = r0 + CHUNK
            mixed = _dot(ws_g, v_ref[r0:r1, c0:c1]) + bsb_ref[:, c0:c1]
            cat_ref[r0:r1, c0:c1] = (u[r0:r1] * mixed * gate[r0:r1]).astype(BF16)

    bxe = _dot(hext_ref[...], win_ref[:, col(3):col(4)])
    n_ext = tile + 2 * POOL_HALO
    edge_row = lax.broadcasted_iota(jnp.int32, (POOL_HALO, 1), 0)
    for gi, w in enumerate(POOL_WINDOWS):
        c0 = gi * B_GROUP_DIM
        c1 = c0 + B_GROUP_DIM
        half = w // 2
        e = jnp.concatenate([bxe[tile:tile + POOL_HALO, c0:c1], bxe[0:tile, c0:c1],
                             bxe[tile + POOL_HALO:, c0:c1]], axis=0)
        a = e
        span = 1
        while span < half:
            a = a + pltpu.roll(a, n_ext - span, 0)
            span *= 2
        win_sum = (a + pltpu.roll(a, half, 0))[POOL_HALO:POOL_HALO + tile]

        def edge_mean(slab, row0, half=half):
            t = j * tile + row0 + edge_row
            cnt = (jnp.minimum(t + half, seq) - jnp.maximum(t - half, 0)).astype(F32)
            return slab / cnt

        mean = jnp.concatenate(
            [edge_mean(win_sum[0:POOL_HALO], 0),
             win_sum[POOL_HALO:tile - POOL_HALO] * (1.0 / w),
             edge_mean(win_sum[tile - POOL_HALO:], tile - POOL_HALO)], axis=0)
        p = (mean - bxe[0:tile, c0:c1]).astype(BF16)
        gate = _silu(_dot(h, win_ref[:, col(4) + c0:col(4) + c1]))
        yb = _dot(p, wg_ref[gi].astype(BF16)) * sc_ref[:, c0:c1] * gate
        cat_ref[:, A_WIDTH + c0:A_WIDTH + c1] = yb.astype(BF16)

    n1 = n1_ref[...]
    hrows = tile // 2
    for r0 in range(0, tile, hrows):
        rows = slice(r0, r0 + hrows)
        out = x_ref[0, rows, :] + _dot(cat_ref[rows, :], wout_ref[...])
        o_ref[0, rows, :] = out
        h1_ref[0, rows, :] = _rms(out, n1).astype(BF16)
    _qkv_project(h1_ref[0], wqkvt_ref, cost_ref[...], sint_ref[...], qt_ref, ka_ref, vt_ref,
                 slice(0, tile))


def _layer0(x, norm_0, w_in_0, a_v_norm_0, a_spatial_w_0, a_spatial_b_0, b_group_w_0, b_scale_0,
            w_out_0, norm_1, w_in_1, w_out_1):
    bn, seq, d = x.shape
    tile = TILE_L0
    nt = seq // tile
    assert bn * nt >= W1_CHUNKS

    def w1_chunk(b, j):
        return jnp.minimum(b * nt + j, W1_CHUNKS - 1)
    halo_blocks = seq // POOL_HALO
    ws = a_spatial_w_0
    wg = b_group_w_0
    cos_fm, sin_fm = _rope_tables(seq)
    bsb = jnp.repeat(a_spatial_b_0.T, A_GROUP_DIM, axis=1)

    def const(shape):
        return pl.BlockSpec(shape, lambda b, j: (0,) * len(shape), pipeline_mode=pl.Buffered(1))

    per_halo = tile // POOL_HALO
    kern = functools.partial(_layer0_kernel, tile=tile, seq=seq)
    return pl.pallas_call(
        kern,
        grid=(bn, nt),
        in_specs=[
            pl.BlockSpec((1, tile, d), lambda b, j: (b, j, 0)),
            pl.BlockSpec((1, POOL_HALO, d), lambda b, j: (b, jnp.maximum(j * per_halo - 1, 0), 0)),
            pl.BlockSpec((1, POOL_HALO, d),
                         lambda b, j: (b, jnp.minimum((j + 1) * per_halo, halo_blocks - 1), 0)),
            const((1, d)),
            pl.BlockSpec(memory_space=pl.ANY),
            const((1, A_WIDTH)),
            const(ws.shape),
            const(bsb.shape),
            const(wg.shape),
            const((1, B_WIDTH)),
            pl.BlockSpec(memory_space=pl.ANY),
            pl.BlockSpec((W1_CHUNK_ROWS, w_in_1.shape[1]), lambda b, j: (w1_chunk(b, j), 0)),
            pl.BlockSpec((W1_CHUNK_ROWS, d), lambda b, j: (w1_chunk(b, j), 0)),
            const((1, d)),
            const((d, QKV_COLS)),
            pl.BlockSpec((ROT_DIM // 2, tile), lambda b, j: (0, j)),
            pl.BlockSpec((ROT_DIM // 2, tile), lambda b, j: (0, j)),
        ],
        out_specs=[
            pl.BlockSpec((1, tile, d), lambda b, j: (b, j, 0)),
            pl.BlockSpec((W1_CHUNK_ROWS, C_WIDTH), lambda b, j: (w1_chunk(b, j), 0)),
            pl.BlockSpec((W1_CHUNK_ROWS, d), lambda b, j: (w1_chunk(b, j), 0)),
            pl.BlockSpec((1, tile, d), lambda b, j: (b, j, 0)),
            pl.BlockSpec((1, C_WIDTH, tile), lambda b, j: (b, 0, j)),
            pl.BlockSpec((1, tile, KV_PAD_WIDTH), lambda b, j: (b, j, 0)),
            pl.BlockSpec((1, KV_WIDTH, tile), lambda b, j: (b, 0, j)),
        ],
        out_shape=[
            jax.ShapeDtypeStruct(x.shape, F32),
            jax.ShapeDtypeStruct((d, C_WIDTH), BF16),
            jax.ShapeDtypeStruct((C_WIDTH, d), BF16),
            jax.ShapeDtypeStruct((bn, seq, d), BF16),
            jax.ShapeDtypeStruct((bn, C_WIDTH, seq), BF16),
            jax.ShapeDtypeStruct((bn, seq, KV_PAD_WIDTH), BF16),
            jax.ShapeDtypeStruct((bn, KV_WIDTH, seq), BF16),
        ],
        scratch_shapes=[
            pltpu.VMEM((tile + 2 * POOL_HALO, d), BF16),
            pltpu.VMEM((tile, A_WIDTH), BF16),
            pltpu.VMEM((tile, A_WIDTH + B_WIDTH), BF16),
            pltpu.VMEM((QKV_COLS, d), BF16),
            pltpu.VMEM(w_in_0.shape, BF16),
            pltpu.VMEM(w_out_0.shape, BF16),
            pltpu.VMEM((2,) + W0_STAGE_CHUNK, F32),
            pltpu.SemaphoreType.DMA((2,)),
        ],
        compiler_params=pltpu.CompilerParams(
            dimension_semantics=("arbitrary", "arbitrary"), vmem_limit_bytes=VMEM_LIMIT),
        name="layer0_mixer",
    )(x, x, x, norm_0.reshape(1, d), w_in_0, a_v_norm_0.reshape(1, A_WIDTH), ws, bsb, wg,
      b_scale_0.reshape(1, B_WIDTH), w_out_0, w_in_1, w_out_1, norm_1.reshape(1, d), w_in_1,
      cos_fm, sin_fm)


def _qkv_project(h, w_ref, ct, st, qt_ref, ka_ref, vt_ref, pos):
    half = ROT_DIM // 2
    n = h.shape[0]
    scale = HEAD_DIM ** -0.5 * LOG2_E

    qkv = _dot_nt(w_ref[...], h)

    def rope_rows(r0):
        x1 = qkv[r0:r0 + half]
        x2 = qkv[r0 + half:r0 + ROT_DIM]
        return jnp.concatenate([x1 * ct - x2 * st, x2 * ct + x1 * st], axis=0)

    for hd in range(N_HEADS):
        r0 = hd * HEAD_DIM
        qt_ref[0, r0:r0 + ROT_DIM, pos] = (rope_rows(r0) * scale).astype(BF16)
        qt_ref[0, r0 + ROT_DIM:r0 + HEAD_DIM, pos] = (
            qkv[r0 + ROT_DIM:r0 + HEAD_DIM] * scale).astype(BF16)
    vt_ref[0, :, pos] = qkv[C_WIDTH + KV_WIDTH:].astype(BF16)

    pad = jnp.zeros((LANES - HEAD_DIM, n), F32)
    for kh in range(N_KV_HEADS):
        r0 = C_WIDTH + kh * HEAD_DIM
        kt = jnp.concatenate([rope_rows(r0), qkv[r0 + ROT_DIM:r0 + HEAD_DIM], pad], axis=0)
        ka_ref[0, pos, kh * LANES:(kh + 1) * LANES] = kt.T.astype(BF16)


def _rope_tables(seq):
    inv = ROPE_THETA ** (-np.arange(0, ROT_DIM, 2, dtype=np.float64) / ROT_DIM)
    ang = np.arange(seq, dtype=np.float64)[None, :] * inv[:, None]
    return jnp.asarray(np.cos(ang), F32), jnp.asarray(np.sin(ang), F32)


def _attn_kernel(sink_ref, x_ref, h_ref, qt_ref, kap_ref, kat_ref, kan_ref, vtp_ref, vtt_ref,
                 vtn_ref, wgate_ref, wout_ref, fn_ref, o_ref,
                 s_ref, p_ref, sg_ref, y_ref, *, tile, seq):
    j = pl.program_id(1)
    nblk = tile // ATTN_BLOCK
    last_blk = seq // ATTN_BLOCK - 1

    def k_band(i, kh):
        cols = slice(kh * LANES, (kh + 1) * LANES)
        lo = max(i - 1, 0) * ATTN_BLOCK
        hi = min(i + 2, nblk) * ATTN_BLOCK
        parts = [kat_ref[0, lo:hi, cols]]
        if i == 0:
            parts.insert(0, kap_ref[0, :, cols])
        if i == nblk - 1:
            parts.append(kan_ref[0, :, cols])
        return jnp.concatenate(parts, axis=0)

    def vt_band(i, kh):
        rows = slice(kh * HEAD_DIM, (kh + 1) * HEAD_DIM)
        lo = max(i - 1, 0) * ATTN_BLOCK
        hi = min(i + 2, nblk) * ATTN_BLOCK
        parts = [vtt_ref[0, rows, lo:hi]]
        if i == 0:
            parts.insert(0, vtp_ref[0, rows, :])
        if i == nblk - 1:
            parts.append(vtn_ref[0, rows, :])
        return parts
    diff = (lax.broadcasted_iota(jnp.int32, (ATTN_BLOCK, ATTN_BLOCK), 0)
            - lax.broadcasted_iota(jnp.int32, (ATTN_BLOCK, ATTN_BLOCK), 1))
    never = 2 * ATTN_BLOCK
    ones_rows = jnp.ones((HEAD_DIM, BAND), BF16)

    def scores(u, half):
        i, kh = divmod(u, N_KV_HEADS)
        slot = u % 2
        c0 = i * ATTN_BLOCK
        h0 = kh * GQA_GROUP + half * HEADS_PER_HALF
        qg = jnp.concatenate(
            [qt_ref[0, (h0 + g) * HEAD_DIM:(h0 + g + 1) * HEAD_DIM, c0:c0 + ATTN_BLOCK]
             for g in range(HEADS_PER_HALF)], axis=1)
        rq = jnp.concatenate([qg, jnp.zeros((LANES - HEAD_DIM, HALF_Q), BF16)], axis=0)
        ka = k_band(i, kh)
        s_ref[slot, :, half * HALF_Q:(half + 1) * HALF_Q] = _dot(ka, rq)

    def softmax_head(u, g):
        i, kh = divmod(u, N_KV_HEADS)
        slot = u % 2
        gblk = j * nblk + i
        ok_prev = diff >= jnp.where(gblk > 0, 0, never)
        ok_next = diff <= jnp.where(gblk < last_blk, 0, -never)
        cs = slice(g * ATTN_BLOCK, (g + 1) * ATTN_BLOCK)
        sink = sink_ref[kh * GQA_GROUP + g] * LOG2_E
        s0 = jnp.where(ok_prev, s_ref[slot, 0:ATTN_BLOCK, cs], NEG_INF)
        s1 = s_ref[slot, ATTN_BLOCK:2 * ATTN_BLOCK, cs]
        s2 = jnp.where(ok_next, s_ref[slot, 2 * ATTN_BLOCK:, cs], NEG_INF)
        m = jnp.max(jnp.maximum(jnp.maximum(s0, s1), s2), axis=0, keepdims=True)
        m = jnp.maximum(m, sink)
        p0 = jnp.exp2(s0 - m)
        p1 = jnp.exp2(s1 - m)
        p2 = jnp.exp2(s2 - m)
        p_ref[slot, 0:ATTN_BLOCK, cs] = p0.astype(BF16)
        p_ref[slot, ATTN_BLOCK:2 * ATTN_BLOCK, cs] = p1.astype(BF16)
        p_ref[slot, 2 * ATTN_BLOCK:, cs] = p2.astype(BF16)
        return jnp.exp2(sink - m)

    def weighted(u, half, sink_share):
        i, kh = divmod(u, N_KV_HEADS)
        slot = u % 2
        c0 = i * ATTN_BLOCK
        vt = jnp.concatenate([jnp.concatenate(vt_band(i, kh), axis=1), ones_rows],
                             axis=0)
        ot = _dot(vt, p_ref[slot, :, half * HALF_Q:(half + 1) * HALF_Q])
        parts = []
        for g in range(HEADS_PER_HALF):
            cq = slice(g * ATTN_BLOCK, (g + 1) * ATTN_BLOCK)
            l = ot[HEAD_DIM:HEAD_DIM + 1, cq] + sink_share[g]
            parts.append(ot[0:HEAD_DIM, cq] * (1.0 / l))
        yt = jnp.concatenate(parts, axis=0)
        f0 = (kh * GQA_GROUP + half * HEADS_PER_HALF) * HEAD_DIM
        cs = slice(f0, f0 + HEADS_PER_HALF * HEAD_DIM)
        y = yt.T * sg_ref[c0:c0 + ATTN_BLOCK, cs]
        y_ref[c0:c0 + ATTN_BLOCK, cs] = y.astype(BF16)

    def gate_chunk(kh):
        cs = slice(kh * GQA_GROUP * HEAD_DIM, (kh + 1) * GQA_GROUP * HEAD_DIM)
        sg_ref[:, cs] = _silu(_dot(h_ref[0], wgate_ref[:, cs]))

    def finish_rows(r0, n):
        rows = slice(r0, r0 + n)
        out = x_ref[0, rows, :] + _dot(y_ref[rows, :], wout_ref[...])
        o_ref[0, rows, :] = _rms(out, fn_ref[...])

    units = nblk * N_KV_HEADS
    inv = {}
    for step in range(units + 2):
        sc = step if step < units else None
        sm = step - 1 if 1 <= step <= units else None
        wt = step - 2 if step >= 2 else None
        if step < N_KV_HEADS:
            gate_chunk(step)
        for half in range(2):
            if sc is not None:
                scores(sc, half)
            if sm is not None:
                inv[sm, 2 * half] = softmax_head(sm, 2 * half)
            if wt is not None:
                weighted(wt, half, [inv.pop((wt, half * HEADS_PER_HALF + g))
                                    for g in range(HEADS_PER_HALF)])
            if sm is not None:
                inv[sm, 2 * half + 1] = softmax_head(sm, 2 * half + 1)
        if wt is not None and (wt + 1) % (units // 2) == 0:
            finish_rows((wt + 1 - units // 2) // N_KV_HEADS * ATTN_BLOCK, tile // 2)


def _attention(x, h, qt, ka, vt, w_gate, sink_1, w_out_1, final_norm):
    bn, seq, d = x.shape
    tile = TILE_ATTN
    nt = seq // tile
    per = tile // ATTN_BLOCK
    nb = seq // ATTN_BLOCK

    def const(shape):
        return pl.BlockSpec(shape, lambda b, j: (0,) * len(shape), pipeline_mode=pl.Buffered(1))

    def prev_blk(j):
        return jnp.maximum(j * per - 1, 0)

    def next_blk(j):
        return jnp.minimum((j + 1) * per, nb - 1)

    kern = functools.partial(_attn_kernel, tile=tile, seq=seq)
    return pl.pallas_call(
        kern,
        grid=(bn, nt),
        in_specs=[
            pl.BlockSpec(memory_space=pltpu.SMEM),
            pl.BlockSpec((1, tile, d), lambda b, j: (b, j, 0)),
            pl.BlockSpec((1, tile, d), lambda b, j: (b, j, 0)),
            pl.BlockSpec((1, C_WIDTH, tile), lambda b, j: (b, 0, j)),
            pl.BlockSpec((1, ATTN_BLOCK, KV_PAD_WIDTH), lambda b, j: (b, prev_blk(j), 0)),
            pl.BlockSpec((1, tile, KV_PAD_WIDTH), lambda b, j: (b, j, 0)),
            pl.BlockSpec((1, ATTN_BLOCK, KV_PAD_WIDTH), lambda b, j: (b, next_blk(j), 0)),
            pl.BlockSpec((1, KV_WIDTH, ATTN_BLOCK), lambda b, j: (b, 0, prev_blk(j))),
            pl.BlockSpec((1, KV_WIDTH, tile), lambda b, j: (b, 0, j)),
            pl.BlockSpec((1, KV_WIDTH, ATTN_BLOCK), lambda b, j: (b, 0, next_blk(j))),
            const(w_gate.shape),
            const(w_out_1.shape),
            const((1, d)),
        ],
        out_specs=pl.BlockSpec((1, tile, d), lambda b, j: (b, j, 0)),
        out_shape=jax.ShapeDtypeStruct(x.shape, F32),
        scratch_shapes=[
            pltpu.VMEM((2, BAND, GROUP_Q), F32),
            pltpu.VMEM((2, BAND, GROUP_Q), BF16),
            pltpu.VMEM((tile, C_WIDTH), F32),
            pltpu.VMEM((tile, C_WIDTH), BF16),
        ],
        compiler_params=pltpu.CompilerParams(
            dimension_semantics=("arbitrary", "arbitrary"), vmem_limit_bytes=VMEM_LIMIT),
        name="layer1_attention",
    )(sink_1, x, h, qt, ka, ka, ka, vt, vt, vt, w_gate, w_out_1, final_norm.reshape(1, d))


def kernel(x, norm_0, w_in_0, a_v_norm_0, a_spatial_w_0, a_spatial_b_0, b_group_w_0, b_scale_0,
           w_out_0, norm_1, w_in_1, sink_1, w_out_1, final_norm):
    x1, w_gate, w_out_1b, h1, qt, ka, vt = _layer0(
        x, norm_0, w_in_0, a_v_norm_0, a_spatial_w_0, a_spatial_b_0, b_group_w_0, b_scale_0,
        w_out_0, norm_1, w_in_1, w_out_1)
    return _attention(x1, h1, qt, ka, vt, w_gate, sink_1, w_out_1b, final_norm)
```

```python
import functools
import math

import jax
import jax.numpy as jnp
import numpy as np
from jax import lax
from jax.experimental import pallas as pl
from jax.experimental.pallas import tpu as pltpu

F32 = jnp.float32
BF16 = jnp.bfloat16

D_MODEL = 1024
EPS = 1e-6
NEG_INF = -1e30
LOG2_E = math.log2(math.e)
CHUNK = 128
A_GROUPS = 4
A_WIDTH = D_MODEL
A_GROUP_DIM = A_WIDTH // A_GROUPS
B_WIDTH = D_MODEL
POOL_WINDOWS = (2, 4, 8, 16)
B_GROUP_DIM = B_WIDTH // len(POOL_WINDOWS)
POOL_HALO = max(POOL_WINDOWS) // 2
N_HEADS = 16
N_KV_HEADS = 4
HEAD_DIM = 64
GQA_GROUP = N_HEADS // N_KV_HEADS
C_WIDTH = N_HEADS * HEAD_DIM
KV_WIDTH = N_KV_HEADS * HEAD_DIM
WINDOW = 128
ATTN_BLOCK = 128
BAND = 3 * ATTN_BLOCK
ROPE_THETA = 500000.0
ROT_DIM = HEAD_DIM // 4
LANES = 128
KV_PAD_WIDTH = N_KV_HEADS * LANES
GROUP_Q = GQA_GROUP * ATTN_BLOCK
HEADS_PER_HALF = GQA_GROUP // 2
HALF_Q = HEADS_PER_HALF * ATTN_BLOCK

QKV_COLS = C_WIDTH + 2 * KV_WIDTH
W1_CHUNK_ROWS = 128
W1_CHUNKS = D_MODEL // W1_CHUNK_ROWS
W0_STAGE_CHUNK = (1024, 512)

TILE_L0 = 512
TILE_ATTN = 512
VMEM_LIMIT = 56 * 1024 * 1024


def _rms(x, g):
    return x * lax.rsqrt(jnp.mean(x * x, axis=-1, keepdims=True) + EPS) * g


def _silu(x):
    return x / (1.0 + jnp.exp(-x))


def _dot(a, b):
    return jnp.dot(a, b, preferred_element_type=F32)


def _dot_nt(a, b):
    return lax.dot_general(a, b, (((1,), (1,)), ((), ())), preferred_element_type=F32)


def _layer0_kernel(x_ref, xnt_ref, xnp_ref, xnn_ref, n0_ref, win_hbm, gv_ref, ws_ref, bsb_ref,
                   wg_ref, sc_ref, wout_hbm, w1_ref, wo1_ref, n1_ref, w1_hbm, cost_ref, sint_ref,
                   o_ref, wgate_ref, wo1b_ref, h1_ref, qt_ref, ka_ref, vt_ref,
                   hext_ref, v_ref, cat_ref, wqkvt_ref, win_ref, wout_ref, stage_ref, wsem,
                   *, tile, seq):
    j = pl.program_id(1)
    nj = pl.num_programs(1)
    step = pl.program_id(0) * nj + j

    @pl.when(step == 0)
    def _():
        cr, cc = stage_ref.shape[1:]
        chunks = [(win_hbm, win_ref, r0, c0, False)
                  for r0 in range(0, win_ref.shape[0], cr) for c0 in range(0, win_ref.shape[1], cc)]
        chunks += [(wout_hbm, wout_ref, r0, c0, False)
                   for r0 in range(0, wout_ref.shape[0], cr)
                   for c0 in range(0, wout_ref.shape[1], cc)]
        chunks += [(w1_hbm, wqkvt_ref, r0, c0, True)
                   for r0 in range(0, wqkvt_ref.shape[1], cr)
                   for c0 in range(0, wqkvt_ref.shape[0], cc)]

        def chunk_copy(k):
            src, _, r0, c0, _ = chunks[k]
            return pltpu.make_async_copy(src.at[r0:r0 + cr, c0:c0 + cc], stage_ref.at[k % 2],
                                         wsem.at[k % 2])

        chunk_copy(0).start()
        for k, (_, dst, r0, c0, transposed) in enumerate(chunks):
            if k + 1 < len(chunks):
                chunk_copy(k + 1).start()
            chunk_copy(k).wait()
            if transposed:
                dst[c0:c0 + cc, r0:r0 + cr] = stage_ref[k % 2].T.astype(BF16)
            else:
                dst[r0:r0 + cr, c0:c0 + cc] = stage_ref[k % 2].astype(BF16)

    @pl.when(step < W1_CHUNKS)
    def _():
        wgate_ref[...] = w1_ref[:, QKV_COLS:].astype(BF16)
        wo1b_ref[...] = wo1_ref[...].astype(BF16)

    n0 = n0_ref[...]

    def normalise_into_hext(xt, xp, xn, jt):
        hext_ref[0:tile, :] = _rms(xt, n0).astype(BF16)
        hp = jnp.where(jt > 0, _rms(xp, n0), 0.0)
        hn = jnp.where(jt < nj - 1, _rms(xn, n0), 0.0)
        hext_ref[tile:tile + 2 * POOL_HALO, :] = jnp.concatenate([hp, hn], axis=0).astype(BF16)

    @pl.when(step == 0)
    def _():
        first_next_rows = xnt_ref[0, 0:POOL_HALO, :]
        normalise_into_hext(x_ref[0], first_next_rows, first_next_rows, 0)

    h = hext_ref[0:tile, :]

    def col(k):
        return k * D_MODEL

    a_v = jax.nn.gelu(_dot(h, win_ref[:, col(1):col(2)]))
    v_ref[...] = _rms(a_v, gv_ref[...]).astype(BF16)
    for g in range(A_GROUPS):
        c0 = g * A_GROUP_DIM
        c1 = c0 + A_GROUP_DIM
        u = jax.nn.gelu(_dot(h, win_ref[:, col(0) + c0:col(0) + c1]))
        gate = _silu(_dot(h, win_ref[:, col(2) + c0:col(2) + c1]))
        ws_g = ws_ref[g].astype(BF16)
        for c in range(tile // CHUNK):
            r0 = c * CHUNK
            r1 = r0 + CHUNK
            mixed = _dot(ws_g, v_ref[r0:r1, c0:c1]) + bsb_ref[:, c0:c1]
            cat_ref[r0:r1, c0:c1] = (u[r0:r1] * mixed * gate[r0:r1]).astype(BF16)

    bxe = _dot(hext_ref[...], win_ref[:, col(3):col(4)])
    n_ext = tile + 2 * POOL_HALO
    edge_row = lax.broadcasted_iota(jnp.int32, (POOL_HALO, 1), 0)
    for gi, w in enumerate(POOL_WINDOWS):
        c0 = gi * B_GROUP_DIM
        c1 = c0 + B_GROUP_DIM
        half = w // 2
        e = jnp.concatenate([bxe[tile:tile + POOL_HALO, c0:c1], bxe[0:tile, c0:c1],
                             bxe[tile + POOL_HALO:, c0:c1]], axis=0)
        a = e
        span = 1
        while span < half:
            a = a + pltpu.roll(a, n_ext - span, 0)
            span *= 2
        win_sum = (a + pltpu.roll(a, half, 0))[POOL_HALO:POOL_HALO + tile]

        def edge_mean(slab, row0, half=half):
            t = j * tile + row0 + edge_row
            cnt = (jnp.minimum(t + half, seq) - jnp.maximum(t - half, 0)).astype(F32)
            return slab / cnt

        mean = jnp.concatenate(
            [edge_mean(win_sum[0:POOL_HALO], 0),
             win_sum[POOL_HALO:tile - POOL_HALO] * (1.0 / w),
             edge_mean(win_sum[tile - POOL_HALO:], tile - POOL_HALO)], axis=0)
        p = (mean - bxe[0:tile, c0:c1]).astype(BF16)
        gate = _silu(_dot(h, win_ref[:, col(4) + c0:col(4) + c1]))
        yb = _dot(p, wg_ref[gi].astype(BF16)) * sc_ref[:, c0:c1] * gate
        cat_ref[:, A_WIDTH + c0:A_WIDTH + c1] = yb.astype(BF16)

    next_step = jnp.minimum(step + 1, pl.num_programs(0) * nj - 1)
    normalise_into_hext(xnt_ref[0], xnp_ref[0], xnn_ref[0], next_step % nj)

    n1 = n1_ref[...]
    hrows = tile // 2
    for r0 in range(0, tile, hrows):
        rows = slice(r0, r0 + hrows)
        out = x_ref[0, rows, :] + _dot(cat_ref[rows, :], wout_ref[...])
        o_ref[0, rows, :] = out
        h1_ref[0, rows, :] = _rms(out, n1).astype(BF16)
    _qkv_project(h1_ref[0], wqkvt_ref, cost_ref[...], sint_ref[...], qt_ref, ka_ref, vt_ref,
                 slice(0, tile))


def _layer0(x, norm_0, w_in_0, a_v_norm_0, a_spatial_w_0, a_spatial_b_0, b_group_w_0, b_scale_0,
            w_out_0, norm_1, w_in_1, w_out_1):
    bn, seq, d = x.shape
    tile = TILE_L0
    nt = seq // tile
    assert bn * nt >= W1_CHUNKS

    def w1_chunk(b, j):
        return jnp.minimum(b * nt + j, W1_CHUNKS - 1)

    def next_tile(b, j):
        s1 = jnp.minimum(b * nt + j + 1, bn * nt - 1)
        return s1 // nt, s1 % nt
    halo_blocks = seq // POOL_HALO
    ws = a_spatial_w_0
    wg = b_group_w_0
    cos_fm, sin_fm = _rope_tables(seq)
    bsb = jnp.repeat(a_spatial_b_0.T, A_GROUP_DIM, axis=1)

    def const(shape):
        return pl.BlockSpec(shape, lambda b, j: (0,) * len(shape), pipeline_mode=pl.Buffered(1))

    per_halo = tile // POOL_HALO
    kern = functools.partial(_layer0_kernel, tile=tile, seq=seq)
    return pl.pallas_call(
        kern,
        grid=(bn, nt),
        in_specs=[
            pl.BlockSpec((1, tile, d), lambda b, j: (b, j, 0)),
            pl.BlockSpec((1, tile, d), lambda b, j: next_tile(b, j) + (0,)),
            pl.BlockSpec((1, POOL_HALO, d), lambda b, j: (
                next_tile(b, j)[0], jnp.maximum(next_tile(b, j)[1] * per_halo - 1, 0), 0)),
            pl.BlockSpec((1, POOL_HALO, d), lambda b, j: (
                next_tile(b, j)[0],
                jnp.minimum((next_tile(b, j)[1] + 1) * per_halo, halo_blocks - 1), 0)),
            const((1, d)),
            pl.BlockSpec(memory_space=pl.ANY),
            const((1, A_WIDTH)),
            const(ws.shape),
            const(bsb.shape),
            const(wg.shape),
            const((1, B_WIDTH)),
            pl.BlockSpec(memory_space=pl.ANY),
            pl.BlockSpec((W1_CHUNK_ROWS, w_in_1.shape[1]), lambda b, j: (w1_chunk(b, j), 0)),
            pl.BlockSpec((W1_CHUNK_ROWS, d), lambda b, j: (w1_chunk(b, j), 0)),
            const((1, d)),
            pl.BlockSpec(memory_space=pl.ANY),
            pl.BlockSpec((ROT_DIM // 2, tile), lambda b, j: (0, j)),
            pl.BlockSpec((ROT_DIM // 2, tile), lambda b, j: (0, j)),
        ],
        out_specs=[
            pl.BlockSpec((1, tile, d), lambda b, j: (b, j, 0)),
            pl.BlockSpec((W1_CHUNK_ROWS, C_WIDTH), lambda b, j: (w1_chunk(b, j), 0)),
            pl.BlockSpec((W1_CHUNK_ROWS, d), lambda b, j: (w1_chunk(b, j), 0)),
            pl.BlockSpec((1, tile, d), lambda b, j: (b, j, 0)),
            pl.BlockSpec((1, C_WIDTH, tile), lambda b, j: (b, 0, j)),
            pl.BlockSpec((1, tile, KV_PAD_WIDTH), lambda b, j: (b, j, 0)),
            pl.BlockSpec((1, KV_WIDTH, tile), lambda b, j: (b, 0, j)),
        ],
        out_shape=[
            jax.ShapeDtypeStruct(x.shape, F32),
            jax.ShapeDtypeStruct((d, C_WIDTH), BF16),
            jax.ShapeDtypeStruct((C_WIDTH, d), BF16),
            jax.ShapeDtypeStruct((bn, seq, d), BF16),
            jax.ShapeDtypeStruct((bn, C_WIDTH, seq), BF16),
            jax.ShapeDtypeStruct((bn, seq, KV_PAD_WIDTH), BF16),
            jax.ShapeDtypeStruct((bn, KV_WIDTH, seq), BF16),
        ],
        scratch_shapes=[
            pltpu.VMEM((tile + 2 * POOL_HALO, d), BF16),
            pltpu.VMEM((tile, A_WIDTH), BF16),
            pltpu.VMEM((tile, A_WIDTH + B_WIDTH), BF16),
            pltpu.VMEM((QKV_COLS, d), BF16),
            pltpu.VMEM(w_in_0.shape, BF16),
            pltpu.VMEM(w_out_0.shape, BF16),
            pltpu.VMEM((2,) + W0_STAGE_CHUNK, F32),
            pltpu.SemaphoreType.DMA((2,)),
        ],
        compiler_params=pltpu.CompilerParams(
            dimension_semantics=("arbitrary", "arbitrary"), vmem_limit_bytes=VMEM_LIMIT),
        name="layer0_mixer",
    )(x, x, x, x, norm_0.reshape(1, d), w_in_0, a_v_norm_0.reshape(1, A_WIDTH), ws, bsb, wg,
      b_scale_0.reshape(1, B_WIDTH), w_out_0, w_in_1, w_out_1, norm_1.reshape(1, d), w_in_1,
      cos_fm, sin_fm)


def _qkv_project(h, w_ref, ct, st, qt_ref, ka_ref, vt_ref, pos):
    half = ROT_DIM // 2
    n = h.shape[0]
    scale = HEAD_DIM ** -0.5 * LOG2_E

    qkv = _dot_nt(w_ref[...], h)

    def rope_rows(r0):
        x1 = qkv[r0:r0 + half]
        x2 = qkv[r0 + half:r0 + ROT_DIM]
        return jnp.concatenate([x1 * ct - x2 * st, x2 * ct + x1 * st], axis=0)

    for hd in range(N_HEADS):
        r0 = hd * HEAD_DIM
        qt_ref[0, r0:r0 + ROT_DIM, pos] = (rope_rows(r0) * scale).astype(BF16)
        qt_ref[0, r0 + ROT_DIM:r0 + HEAD_DIM, pos] = (
            qkv[r0 + ROT_DIM:r0 + HEAD_DIM] * scale).astype(BF16)
    vt_ref[0, :, pos] = qkv[C_WIDTH + KV_WIDTH:].astype(BF16)

    pad = jnp.zeros((LANES - HEAD_DIM, n), F32)
    for kh in range(N_KV_HEADS):
        r0 = C_WIDTH + kh * HEAD_DIM
        kt = jnp.concatenate([rope_rows(r0), qkv[r0 + ROT_DIM:r0 + HEAD_DIM], pad], axis=0)
        ka_ref[0, pos, kh * LANES:(kh + 1) * LANES] = kt.T.astype(BF16)


def _rope_tables(seq):
    inv = ROPE_THETA ** (-np.arange(0, ROT_DIM, 2, dtype=np.float64) / ROT_DIM)
    ang = np.arange(seq, dtype=np.float64)[None, :] * inv[:, None]
    return jnp.asarray(np.cos(ang), F32), jnp.asarray(np.sin(ang), F32)


def _attn_kernel(sink_ref, x_ref, h_ref, qt_ref, kap_ref, kat_ref, kan_ref, vtp_ref, vtt_ref,
                 vtn_ref, wgate_ref, wout_ref, fn_ref, o_ref,
                 s_ref, p_ref, sg_ref, y_ref, *, tile, seq):
    j = pl.program_id(1)
    nblk = tile // ATTN_BLOCK
    last_blk = seq // ATTN_BLOCK - 1

    def k_band(i, kh):
        cols = slice(kh * LANES, (kh + 1) * LANES)
        lo = max(i - 1, 0) * ATTN_BLOCK
        hi = min(i + 2, nblk) * ATTN_BLOCK
        parts = [kat_ref[0, lo:hi, cols]]
        if i == 0:
            parts.insert(0, kap_ref[0, :, cols])
        if i == nblk - 1:
            parts.append(kan_ref[0, :, cols])
        return jnp.concatenate(parts, axis=0)

    def vt_band(i, kh):
        rows = slice(kh * HEAD_DIM, (kh + 1) * HEAD_DIM)
        lo = max(i - 1, 0) * ATTN_BLOCK
        hi = min(i + 2, nblk) * ATTN_BLOCK
        parts = [vtt_ref[0, rows, lo:hi]]
        if i == 0:
            parts.insert(0, vtp_ref[0, rows, :])
        if i == nblk - 1:
            parts.append(vtn_ref[0, rows, :])
        return parts
    diff = (lax.broadcasted_iota(jnp.int32, (ATTN_BLOCK, ATTN_BLOCK), 0)
            - lax.broadcasted_iota(jnp.int32, (ATTN_BLOCK, ATTN_BLOCK), 1))
    never = 2 * ATTN_BLOCK
    ones_rows = jnp.ones((HEAD_DIM, BAND), BF16)

    def scores(u, half):
        i, kh = divmod(u, N_KV_HEADS)
        slot = u % 2
        c0 = i * ATTN_BLOCK
        h0 = kh * GQA_GROUP + half * HEADS_PER_HALF
        qg = jnp.concatenate(
            [qt_ref[0, (h0 + g) * HEAD_DIM:(h0 + g + 1) * HEAD_DIM, c0:c0 + ATTN_BLOCK]
             for g in range(HEADS_PER_HALF)], axis=1)
        rq = jnp.concatenate([qg, jnp.zeros((LANES - HEAD_DIM, HALF_Q), BF16)], axis=0)
        ka = k_band(i, kh)
        s_ref[slot, :, half * HALF_Q:(half + 1) * HALF_Q] = _dot(ka, rq)

    def softmax_head(u, g):
        i, kh = divmod(u, N_KV_HEADS)
        slot = u % 2
        gblk = j * nblk + i
        ok_prev = diff >= jnp.where(gblk > 0, 0, never)
        ok_next = diff <= jnp.where(gblk < last_blk, 0, -never)
        cs = slice(g * ATTN_BLOCK, (g + 1) * ATTN_BLOCK)
        sink = sink_ref[kh * GQA_GROUP + g] * LOG2_E
        s0 = jnp.where(ok_prev, s_ref[slot, 0:ATTN_BLOCK, cs], NEG_INF)
        s1 = s_ref[slot, ATTN_BLOCK:2 * ATTN_BLOCK, cs]
        s2 = jnp.where(ok_next, s_ref[slot, 2 * ATTN_BLOCK:, cs], NEG_INF)
        m = jnp.max(jnp.maximum(jnp.maximum(s0, s1), s2), axis=0, keepdims=True)
        m = jnp.maximum(m, sink)
        p0 = jnp.exp2(s0 - m)
        p1 = jnp.exp2(s1 - m)
        p2 = jnp.exp2(s2 - m)
        p_ref[slot, 0:ATTN_BLOCK, cs] = p0.astype(BF16)
        p_ref[slot, ATTN_BLOCK:2 * ATTN_BLOCK, cs] = p1.astype(BF16)
        p_ref[slot, 2 * ATTN_BLOCK:, cs] = p2.astype(BF16)
        return jnp.exp2(sink - m)

    def weighted(u, half, sink_share):
        i, kh = divmod(u, N_KV_HEADS)
        slot = u % 2
        c0 = i * ATTN_BLOCK
        vt = jnp.concatenate([jnp.concatenate(vt_band(i, kh), axis=1), ones_rows],
                             axis=0)
        ot = _dot(vt, p_ref[slot, :, half * HALF_Q:(half + 1) * HALF_Q])
        parts = []
        for g in range(HEADS_PER_HALF):
            cq = slice(g * ATTN_BLOCK, (g + 1) * ATTN_BLOCK)
            l = ot[HEAD_DIM:HEAD_DIM + 1, cq] + sink_share[g]
            parts.append(ot[0:HEAD_DIM, cq] * (1.0 / l))
        yt = jnp.concatenate(parts, axis=0)
        f0 = (kh * GQA_GROUP + half * HEADS_PER_HALF) * HEAD_DIM
        cs = slice(f0, f0 + HEADS_PER_HALF * HEAD_DIM)
        y = yt.T * sg_ref[c0:c0 + ATTN_BLOCK, cs]
        y_ref[c0:c0 + ATTN_BLOCK, cs] = y.astype(BF16)

    def gate_chunk(kh):
        cs = slice(kh * GQA_GROUP * HEAD_DIM, (kh + 1) * GQA_GROUP * HEAD_DIM)
        sg_ref[:, cs] = _silu(_dot(h_ref[0], wgate_ref[:, cs]))

    def finish_rows(r0, n):
        rows = slice(r0, r0 + n)
        out = x_ref[0, rows, :] + _dot(y_ref[rows, :], wout_ref[...])
        o_ref[0, rows, :] = _rms(out, fn_ref[...])

    units = nblk * N_KV_HEADS
    inv = {}
    for step in range(units + 2):
        sc = step if step < units else None
        sm = step - 1 if 1 <= step <= units else None
        wt = step - 2 if step >= 2 else None
        if step < N_KV_HEADS:
            gate_chunk(step)
        for half in range(2):
            if sc is not None:
                scores(sc, half)
            if sm is not None:
                inv[sm, 2 * half] = softmax_head(sm, 2 * half)
            if wt is not None:
                weighted(wt, half, [inv.pop((wt, half * HEADS_PER_HALF + g))
                                    for g in range(HEADS_PER_HALF)])
            if sm is not None:
                inv[sm, 2 * half + 1] = softmax_head(sm, 2 * half + 1)
        if wt is not None and (wt + 1) % (units // 2) == 0:
            finish_rows((wt + 1 - units // 2) // N_KV_HEADS * ATTN_BLOCK, tile // 2)


def _attention(x, h, qt, ka, vt, w_gate, sink_1, w_out_1, final_norm):
    bn, seq, d = x.shape
    tile = TILE_ATTN
    nt = seq // tile
    per = tile // ATTN_BLOCK
    nb = seq // ATTN_BLOCK

    def const(shape):
        return pl.BlockSpec(shape, lambda b, j: (0,) * len(shape), pipeline_mode=pl.Buffered(1))

    def prev_blk(j):
        return jnp.maximum(j * per - 1, 0)

    def next_blk(j):
        return jnp.minimum((j + 1) * per, nb - 1)

    kern = functools.partial(_attn_kernel, tile=tile, seq=seq)
    return pl.pallas_call(
        kern,
        grid=(bn, nt),
        in_specs=[
            pl.BlockSpec(memory_space=pltpu.SMEM),
            pl.BlockSpec((1, tile, d), lambda b, j: (b, j, 0)),
            pl.BlockSpec((1, tile, d), lambda b, j: (b, j, 0)),
            pl.BlockSpec((1, C_WIDTH, tile), lambda b, j: (b, 0, j)),
            pl.BlockSpec((1, ATTN_BLOCK, KV_PAD_WIDTH), lambda b, j: (b, prev_blk(j), 0)),
            pl.BlockSpec((1, tile, KV_PAD_WIDTH), lambda b, j: (b, j, 0)),
            pl.BlockSpec((1, ATTN_BLOCK, KV_PAD_WIDTH), lambda b, j: (b, next_blk(j), 0)),
            pl.BlockSpec((1, KV_WIDTH, ATTN_BLOCK), lambda b, j: (b, 0, prev_blk(j))),
            pl.BlockSpec((1, KV_WIDTH, tile), lambda b, j: (b, 0, j)),
            pl.BlockSpec((1, KV_WIDTH, ATTN_BLOCK), lambda b, j: (b, 0, next_blk(j))),
            const(w_gate.shape),
            const(w_out_1.shape),
            const((1, d)),
        ],
        out_specs=pl.BlockSpec((1, tile, d), lambda b, j: (b, j, 0)),
        out_shape=jax.ShapeDtypeStruct(x.shape, F32),
        scratch_shapes=[
            pltpu.VMEM((2, BAND, GROUP_Q), F32),
            pltpu.VMEM((2, BAND, GROUP_Q), BF16),
            pltpu.VMEM((tile, C_WIDTH), F32),
            pltpu.VMEM((tile, C_WIDTH), BF16),
        ],
        compiler_params=pltpu.CompilerParams(
            dimension_semantics=("arbitrary", "arbitrary"), vmem_limit_bytes=VMEM_LIMIT),
        name="layer1_attention",
    )(sink_1, x, h, qt, ka, ka, ka, vt, vt, vt, w_gate, w_out_1, final_norm.reshape(1, d))


def kernel(x, norm_0, w_in_0, a_v_norm_0, a_spatial_w_0, a_spatial_b_0, b_group_w_0, b_scale_0,
           w_out_0, norm_1, w_in_1, sink_1, w_out_1, final_norm):
    x1, w_gate, w_out_1b, h1, qt, ka, vt = _layer0(
        x, norm_0, w_in_0, a_v_norm_0, a_spatial_w_0, a_spatial_b_0, b_group_w_0, b_scale_0,
        w_out_0, norm_1, w_in_1, w_out_1)
    return _attention(x1, h1, qt, ka, vt, w_gate, sink_1, w_out_1b, final_norm)
```

```python
import functools
import math

import jax
import jax.numpy as jnp
import numpy as np
from jax import lax
from jax.experimental import pallas as pl
from jax.experimental.pallas import tpu as pltpu

F32 = jnp.float32
BF16 = jnp.bfloat16

D_MODEL = 1024
EPS = 1e-6
NEG_INF = -1e30
LOG2_E = math.log2(math.e)
CHUNK = 128
A_GROUPS = 4
A_WIDTH = D_MODEL
A_GROUP_DIM = A_WIDTH // A_GROUPS
B_WIDTH = D_MODEL
POOL_WINDOWS = (2, 4, 8, 16)
B_GROUP_DIM = B_WIDTH // len(POOL_WINDOWS)
POOL_HALO = max(POOL_WINDOWS) // 2
N_HEADS = 16
N_KV_HEADS = 4
HEAD_DIM = 64
GQA_GROUP = N_HEADS // N_KV_HEADS
C_WIDTH = N_HEADS * HEAD_DIM
KV_WIDTH = N_KV_HEADS * HEAD_DIM
WINDOW = 128
ATTN_BLOCK = 128
BAND = 3 * ATTN_BLOCK
ROPE_THETA = 500000.0
ROT_DIM = HEAD_DIM // 4
LANES = 128
KV_PAD_WIDTH = N_KV_HEADS * LANES
GROUP_Q = GQA_GROUP * ATTN_BLOCK
HEADS_PER_HALF = GQA_GROUP // 2
HALF_Q = HEADS_PER_HALF * ATTN_BLOCK

QKV_COLS = C_WIDTH + 2 * KV_WIDTH
W1_CHUNK_ROWS = 128
W1_CHUNKS = D_MODEL // W1_CHUNK_ROWS
W0_STAGE_CHUNK = (1024, 512)

TILE_L0 = 512
TILE_ATTN = 512
VMEM_LIMIT = 56 * 1024 * 1024


def _rms(x, g):
    return x * lax.rsqrt(jnp.mean(x * x, axis=-1, keepdims=True) + EPS) * g


def _silu(x):
    return x / (1.0 + jnp.exp(-x))


def _dot(a, b):
    return jnp.dot(a, b, preferred_element_type=F32)


def _dot_nt(a, b):
    return lax.dot_general(a, b, (((1,), (1,)), ((), ())), preferred_element_type=F32)


def _layer0_kernel(x_ref, xp_ref, xn_ref, n0_ref, win_hbm, gv_ref, ws_ref, bsb_ref, wg_ref,
                   sc_ref, wout_hbm, w1_ref, wo1_ref, n1_ref, w1_hbm, cost_ref, sint_ref,
                   o_ref, wgate_ref, wo1b_ref, h1_ref, qt_ref, ka_ref, vt_ref,
                   hext_ref, v_ref, cat_ref, wqkvt_ref, win_ref, wout_ref, stage_ref, wsem,
                   *, tile, seq):
    j = pl.program_id(1)
    nj = pl.num_programs(1)
    step = pl.program_id(0) * nj + j

    @pl.when(step == 0)
    def _():
        cr, cc = stage_ref.shape[1:]
        chunks = [(win_hbm, win_ref, r0, c0, False)
                  for r0 in range(0, win_ref.shape[0], cr) for c0 in range(0, win_ref.shape[1], cc)]
        chunks += [(wout_hbm, wout_ref, r0, c0, False)
                   for r0 in range(0, wout_ref.shape[0], cr)
                   for c0 in range(0, wout_ref.shape[1], cc)]
        chunks += [(w1_hbm, wqkvt_ref, r0, c0, True)
                   for r0 in range(0, wqkvt_ref.shape[1], cr)
                   for c0 in range(0, wqkvt_ref.shape[0], cc)]

        def chunk_copy(k):
            src, _, r0, c0, _ = chunks[k]
            return pltpu.make_async_copy(src.at[r0:r0 + cr, c0:c0 + cc], stage_ref.at[k % 2],
                                         wsem.at[k % 2])

        chunk_copy(0).start()
        for k, (_, dst, r0, c0, transposed) in enumerate(chunks):
            if k + 1 < len(chunks):
                chunk_copy(k + 1).start()
            chunk_copy(k).wait()
            if transposed:
                dst[c0:c0 + cc, r0:r0 + cr] = stage_ref[k % 2].T.astype(BF16)
            else:
                dst[r0:r0 + cr, c0:c0 + cc] = stage_ref[k % 2].astype(BF16)

    @pl.when(step < W1_CHUNKS)
    def _():
        wgate_ref[...] = w1_ref[:, QKV_COLS:].astype(BF16)
        wo1b_ref[...] = wo1_ref[...].astype(BF16)

    n0 = n0_ref[...]
    x = x_ref[0]
    h = _rms(x, n0).astype(BF16)
    hext_ref[0:tile, :] = h
    hp = jnp.where(j > 0, _rms(xp_ref[0], n0), 0.0)
    hn = jnp.where(j < nj - 1, _rms(xn_ref[0], n0), 0.0)
    hext_ref[tile:tile + 2 * POOL_HALO, :] = jnp.concatenate([hp, hn], axis=0).astype(BF16)

    def col(k):
        return k * D_MODEL

    a_v = jax.nn.gelu(_dot(h, win_ref[:, col(1):col(2)]))
    v_ref[...] = _rms(a_v, gv_ref[...]).astype(BF16)
    for g in range(A_GROUPS):
        c0 = g * A_GROUP_DIM
        c1 = c0 + A_GROUP_DIM
        u = jax.nn.gelu(_dot(h, win_ref[:, col(0) + c0:col(0) + c1]))
        gate = _silu(_dot(h, win_ref[:, col(2) + c0:col(2) + c1]))
        ws_g = ws_ref[g].astype(BF16)
        for c in range(tile // CHUNK):
            r0 = c * CHUNK
            r1 = r0 + CHUNK
            mixed = _dot(ws_g, v_ref[r0:r1, c0:c1]) + bsb_ref[:, c0:c1]
            cat_ref[r0:r1, c0:c1] = (u[r0:r1] * mixed * gate[r0:r1]).astype(BF16)

    bxe = _dot(hext_ref[...], win_ref[:, col(3):col(4)])
    n_ext = tile + 2 * POOL_HALO
    edge_row = lax.broadcasted_iota(jnp.int32, (POOL_HALO, 1), 0)
    for gi, w in enumerate(POOL_WINDOWS):
        c0 = gi * B_GROUP_DIM
        c1 = c0 + B_GROUP_DIM
        half = w // 2
        e = jnp.concatenate([bxe[tile:tile + POOL_HALO, c0:c1], bxe[0:tile, c0:c1],
                             bxe[tile + POOL_HALO:, c0:c1]], axis=0)
        a = e
        span = 1
        while span < half:
            a = a + pltpu.roll(a, n_ext - span, 0)
            span *= 2
        win_sum = (a + pltpu.roll(a, half, 0))[POOL_HALO:POOL_HALO + tile]

        def edge_mean(slab, row0, half=half):
            t = j * tile + row0 + edge_row
            cnt = (jnp.minimum(t + half, seq) - jnp.maximum(t - half, 0)).astype(F32)
            return slab / cnt

        mean = jnp.concatenate(
            [edge_mean(win_sum[0:POOL_HALO], 0),
             win_sum[POOL_HALO:tile - POOL_HALO] * (1.0 / w),
             edge_mean(win_sum[tile - POOL_HALO:], tile - POOL_HALO)], axis=0)
        p = (mean - bxe[0:tile, c0:c1]).astype(BF16)
        gate = _silu(_dot(h, win_ref[:, col(4) + c0:col(4) + c1]))
        yb = _dot(p, wg_ref[gi].astype(BF16)) * sc_ref[:, c0:c1] * gate
        cat_ref[:, A_WIDTH + c0:A_WIDTH + c1] = yb.astype(BF16)

    n1 = n1_ref[...]
    hrows = tile // 2
    for r0 in range(0, tile, hrows):
        rows = slice(r0, r0 + hrows)
        out = x_ref[0, rows, :] + _dot(cat_ref[rows, :], wout_ref[...])
        o_ref[0, rows, :] = out
        h1_ref[0, rows, :] = _rms(out, n1).astype(BF16)
    _qkv_project(h1_ref[0], wqkvt_ref, cost_ref[...], sint_ref[...], qt_ref, ka_ref, vt_ref,
                 slice(0, tile))


def _layer0(x, norm_0, w_in_0, a_v_norm_0, a_spatial_w_0, a_spatial_b_0, b_group_w_0, b_scale_0,
            w_out_0, norm_1, w_in_1, w_out_1):
    bn, seq, d = x.shape
    tile = TILE_L0
    nt = seq // tile
    assert bn * nt >= W1_CHUNKS

    def w1_chunk(b, j):
        return jnp.minimum(b * nt + j, W1_CHUNKS - 1)
    halo_blocks = seq // POOL_HALO
    ws = a_spatial_w_0
    wg = b_group_w_0
    cos_fm, sin_fm = _rope_tables(seq)
    bsb = jnp.repeat(a_spatial_b_0.T, A_GROUP_DIM, axis=1)

    def const(shape):
        return pl.BlockSpec(shape, lambda b, j: (0,) * len(shape), pipeline_mode=pl.Buffered(1))

    per_halo = tile // POOL_HALO
    kern = functools.partial(_layer0_kernel, tile=tile, seq=seq)
    return pl.pallas_call(
        kern,
        grid=(bn, nt),
        in_specs=[
            pl.BlockSpec((1, tile, d), lambda b, j: (b, j, 0)),
            pl.BlockSpec((1, POOL_HALO, d), lambda b, j: (b, jnp.maximum(j * per_halo - 1, 0), 0)),
            pl.BlockSpec((1, POOL_HALO, d),
                         lambda b, j: (b, jnp.minimum((j + 1) * per_halo, halo_blocks - 1), 0)),
            const((1, d)),
            pl.BlockSpec(memory_space=pl.ANY),
            const((1, A_WIDTH)),
            const(ws.shape),
            const(bsb.shape),
            const(wg.shape),
            const((1, B_WIDTH)),
            pl.BlockSpec(memory_space=pl.ANY),
            pl.BlockSpec((W1_CHUNK_ROWS, w_in_1.shape[1]), lambda b, j: (w1_chunk(b, j), 0)),
            pl.BlockSpec((W1_CHUNK_ROWS, d), lambda b, j: (w1_chunk(b, j), 0)),
            const((1, d)),
            pl.BlockSpec(memory_space=pl.ANY),
            pl.BlockSpec((ROT_DIM // 2, tile), lambda b, j: (0, j)),
            pl.BlockSpec((ROT_DIM // 2, tile), lambda b, j: (0, j)),
        ],
        out_specs=[
            pl.BlockSpec((1, tile, d), lambda b, j: (b, j, 0)),
            pl.BlockSpec((W1_CHUNK_ROWS, C_WIDTH), lambda b, j: (w1_chunk(b, j), 0)),
            pl.BlockSpec((W1_CHUNK_ROWS, d), lambda b, j: (w1_chunk(b, j), 0)),
            pl.BlockSpec((1, tile, d), lambda b, j: (b, j, 0)),
            pl.BlockSpec((1, C_WIDTH, tile), lambda b, j: (b, 0, j)),
            pl.BlockSpec((1, tile, KV_PAD_WIDTH), lambda b, j: (b, j, 0)),
            pl.BlockSpec((1, KV_WIDTH, tile), lambda b, j: (b, 0, j)),
        ],
        out_shape=[
            jax.ShapeDtypeStruct(x.shape, F32),
            jax.ShapeDtypeStruct((d, C_WIDTH), BF16),
            jax.ShapeDtypeStruct((C_WIDTH, d), BF16),
            jax.ShapeDtypeStruct((bn, seq, d), BF16),
            jax.ShapeDtypeStruct((bn, C_WIDTH, seq), BF16),
            jax.ShapeDtypeStruct((bn, seq, KV_PAD_WIDTH), BF16),
            jax.ShapeDtypeStruct((bn, KV_WIDTH, seq), BF16),
        ],
        scratch_shapes=[
            pltpu.VMEM((tile + 2 * POOL_HALO, d), BF16),
            pltpu.VMEM((tile, A_WIDTH), BF16),
            pltpu.VMEM((tile, A_WIDTH + B_WIDTH), BF16),
            pltpu.VMEM((QKV_COLS, d), BF16),
            pltpu.VMEM(w_in_0.shape, BF16),
            pltpu.VMEM(w_out_0.shape, BF16),
            pltpu.VMEM((2,) + W0_STAGE_CHUNK, F32),
            pltpu.SemaphoreType.DMA((2,)),
        ],
        compiler_params=pltpu.CompilerParams(
            dimension_semantics=("arbitrary", "arbitrary"), vmem_limit_bytes=VMEM_LIMIT),
        name="layer0_mixer",
    )(x, x, x, norm_0.reshape(1, d), w_in_0, a_v_norm_0.reshape(1, A_WIDTH), ws, bsb, wg,
      b_scale_0.reshape(1, B_WIDTH), w_out_0, w_in_1, w_out_1, norm_1.reshape(1, d), w_in_1,
      cos_fm, sin_fm)


def _qkv_project(h, w_ref, ct, st, qt_ref, ka_ref, vt_ref, pos):
    half = ROT_DIM // 2
    n = h.shape[0]
    scale = HEAD_DIM ** -0.5 * LOG2_E

    qkv = _dot_nt(w_ref[...], h)

    def rope_rows(r0):
        x1 = qkv[r0:r0 + half]
        x2 = qkv[r0 + half:r0 + ROT_DIM]
        return jnp.concatenate([x1 * ct - x2 * st, x2 * ct + x1 * st], axis=0)

    for hd in range(N_HEADS):
        r0 = hd * HEAD_DIM
        qt_ref[0, r0:r0 + ROT_DIM, pos] = (rope_rows(r0) * scale).astype(BF16)
        qt_ref[0, r0 + ROT_DIM:r0 + HEAD_DIM, pos] = (
            qkv[r0 + ROT_DIM:r0 + HEAD_DIM] * scale).astype(BF16)
    vt_ref[0, :, pos] = qkv[C_WIDTH + KV_WIDTH:].astype(BF16)

    pad = jnp.zeros((LANES - HEAD_DIM, n), F32)
    for kh in range(N_KV_HEADS):
        r0 = C_WIDTH + kh * HEAD_DIM
        kt = jnp.concatenate([rope_rows(r0), qkv[r0 + ROT_DIM:r0 + HEAD_DIM], pad], axis=0)
        ka_ref[0, pos, kh * LANES:(kh + 1) * LANES] = kt.T.astype(BF16)


def _rope_tables(seq):
    inv = ROPE_THETA ** (-np.arange(0, ROT_DIM, 2, dtype=np.float64) / ROT_DIM)
    ang = np.arange(seq, dtype=np.float64)[None, :] * inv[:, None]
    return jnp.asarray(np.cos(ang), F32), jnp.asarray(np.sin(ang), F32)


def _attn_kernel(sink_ref, x_ref, h_ref, qt_ref, kap_ref, kat_ref, kan_ref, vtp_ref, vtt_ref,
                 vtn_ref, wgate_ref, wout_ref, fn_ref, o_ref,
                 s_ref, p_ref, sg_ref, y_ref, *, tile, seq):
    j = pl.program_id(1)
    nblk = tile // ATTN_BLOCK
    last_blk = seq // ATTN_BLOCK - 1

    def k_band(i, kh):
        cols = slice(kh * LANES, (kh + 1) * LANES)
        lo = max(i - 1, 0) * ATTN_BLOCK
        hi = min(i + 2, nblk) * ATTN_BLOCK
        parts = [kat_ref[0, lo:hi, cols]]
        if i == 0:
            parts.insert(0, kap_ref[0, :, cols])
        if i == nblk - 1:
            parts.append(kan_ref[0, :, cols])
        return jnp.concatenate(parts, axis=0)

    def vt_band(i, kh):
        rows = slice(kh * HEAD_DIM, (kh + 1) * HEAD_DIM)
        lo = max(i - 1, 0) * ATTN_BLOCK
        hi = min(i + 2, nblk) * ATTN_BLOCK
        parts = [vtt_ref[0, rows, lo:hi]]
        if i == 0:
            parts.insert(0, vtp_ref[0, rows, :])
        if i == nblk - 1:
            parts.append(vtn_ref[0, rows, :])
        return parts
    diff = (lax.broadcasted_iota(jnp.int32, (ATTN_BLOCK, ATTN_BLOCK), 0)
            - lax.broadcasted_iota(jnp.int32, (ATTN_BLOCK, ATTN_BLOCK), 1))
    never = 2 * ATTN_BLOCK

    def scores(u, half):
        i, kh = divmod(u, N_KV_HEADS)
        slot = u % 2
        c0 = i * ATTN_BLOCK
        h0 = kh * GQA_GROUP + half * HEADS_PER_HALF
        qg = jnp.concatenate(
            [qt_ref[0, (h0 + g) * HEAD_DIM:(h0 + g + 1) * HEAD_DIM, c0:c0 + ATTN_BLOCK]
             for g in range(HEADS_PER_HALF)], axis=1)
        rq = jnp.concatenate([qg, jnp.zeros((LANES - HEAD_DIM, HALF_Q), BF16)], axis=0)
        ka = k_band(i, kh)
        s_ref[slot, :, half * HALF_Q:(half + 1) * HALF_Q] = _dot(ka, rq)

    def softmax_head(u, g):
        i, kh = divmod(u, N_KV_HEADS)
        slot = u % 2
        gblk = j * nblk + i
        ok_prev = diff >= jnp.where(gblk > 0, 0, never)
        ok_next = diff <= jnp.where(gblk < last_blk, 0, -never)
        cs = slice(g * ATTN_BLOCK, (g + 1) * ATTN_BLOCK)
        sink = sink_ref[kh * GQA_GROUP + g] * LOG2_E
        s0 = jnp.where(ok_prev, s_ref[slot, 0:ATTN_BLOCK, cs], NEG_INF)
        s1 = s_ref[slot, ATTN_BLOCK:2 * ATTN_BLOCK, cs]
        s2 = jnp.where(ok_next, s_ref[slot, 2 * ATTN_BLOCK:, cs], NEG_INF)
        m = jnp.max(jnp.maximum(jnp.maximum(s0, s1), s2), axis=0, keepdims=True)
        m = jnp.maximum(m, sink)
        p0 = jnp.exp2(s0 - m)
        p1 = jnp.exp2(s1 - m)
        p2 = jnp.exp2(s2 - m)
        p_ref[slot, 0:ATTN_BLOCK, cs] = p0.astype(BF16)
        p_ref[slot, ATTN_BLOCK:2 * ATTN_BLOCK, cs] = p1.astype(BF16)
        p_ref[slot, 2 * ATTN_BLOCK:, cs] = p2.astype(BF16)
        l = jnp.sum(p0 + p1 + p2, axis=0, keepdims=True) + jnp.exp2(sink - m)
        return 1.0 / l

    def weighted(u, half, inv):
        i, kh = divmod(u, N_KV_HEADS)
        slot = u % 2
        c0 = i * ATTN_BLOCK
        vt = jnp.concatenate(vt_band(i, kh), axis=1)
        ot = _dot(vt, p_ref[slot, :, half * HALF_Q:(half + 1) * HALF_Q])
        yt = jnp.concatenate(
            [ot[:, g * ATTN_BLOCK:(g + 1) * ATTN_BLOCK] * inv[g] for g in range(HEADS_PER_HALF)],
            axis=0)
        f0 = (kh * GQA_GROUP + half * HEADS_PER_HALF) * HEAD_DIM
        cs = slice(f0, f0 + HEADS_PER_HALF * HEAD_DIM)
        y = yt.T * sg_ref[c0:c0 + ATTN_BLOCK, cs]
        y_ref[c0:c0 + ATTN_BLOCK, cs] = y.astype(BF16)

    def gate_chunk(kh):
        cs = slice(kh * GQA_GROUP * HEAD_DIM, (kh + 1) * GQA_GROUP * HEAD_DIM)
        sg_ref[:, cs] = _silu(_dot(h_ref[0], wgate_ref[:, cs]))

    def finish_rows(r0, n):
        rows = slice(r0, r0 + n)
        out = x_ref[0, rows, :] + _dot(y_ref[rows, :], wout_ref[...])
        o_ref[0, rows, :] = _rms(out, fn_ref[...])

    units = nblk * N_KV_HEADS
    inv = {}
    for step in range(units + 2):
        sc = step if step < units else None
        sm = step - 1 if 1 <= step <= units else None
        wt = step - 2 if step >= 2 else None
        if step < N_KV_HEADS:
            gate_chunk(step)
        for half in range(2):
            if sc is not None:
                scores(sc, half)
            if sm is not None:
                inv[sm, 2 * half] = softmax_head(sm, 2 * half)
            if wt is not None:
                weighted(wt, half, [inv.pop((wt, half * HEADS_PER_HALF + g))
                                    for g in range(HEADS_PER_HALF)])
            if sm is not None:
                inv[sm, 2 * half + 1] = softmax_head(sm, 2 * half + 1)
        if wt is not None and (wt + 1) % (units // 2) == 0:
            finish_rows((wt + 1 - units // 2) // N_KV_HEADS * ATTN_BLOCK, tile // 2)


def _attention(x, h, qt, ka, vt, w_gate, sink_1, w_out_1, final_norm):
    bn, seq, d = x.shape
    tile = TILE_ATTN
    nt = seq // tile
    per = tile // ATTN_BLOCK
    nb = seq // ATTN_BLOCK

    def const(shape):
        return pl.BlockSpec(shape, lambda b, j: (0,) * len(shape), pipeline_mode=pl.Buffered(1))

    def prev_blk(j):
        return jnp.maximum(j * per - 1, 0)

    def next_blk(j):
        return jnp.minimum((j + 1) * per, nb - 1)

    kern = functools.partial(_attn_kernel, tile=tile, seq=seq)
    return pl.pallas_call(
        kern,
        grid=(bn, nt),
        in_specs=[
            pl.BlockSpec(memory_space=pltpu.SMEM),
            pl.BlockSpec((1, tile, d), lambda b, j: (b, j, 0)),
            pl.BlockSpec((1, tile, d), lambda b, j: (b, j, 0)),
            pl.BlockSpec((1, C_WIDTH, tile), lambda b, j: (b, 0, j)),
            pl.BlockSpec((1, ATTN_BLOCK, KV_PAD_WIDTH), lambda b, j: (b, prev_blk(j), 0)),
            pl.BlockSpec((1, tile, KV_PAD_WIDTH), lambda b, j: (b, j, 0)),
            pl.BlockSpec((1, ATTN_BLOCK, KV_PAD_WIDTH), lambda b, j: (b, next_blk(j), 0)),
            pl.BlockSpec((1, KV_WIDTH, ATTN_BLOCK), lambda b, j: (b, 0, prev_blk(j))),
            pl.BlockSpec((1, KV_WIDTH, tile), lambda b, j: (b, 0, j)),
            pl.BlockSpec((1, KV_WIDTH, ATTN_BLOCK), lambda b, j: (b, 0, next_blk(j))),
            const(w_gate.shape),
            const(w_out_1.shape),
            const((1, d)),
        ],
        out_specs=pl.BlockSpec((1, tile, d), lambda b, j: (b, j, 0)),
        out_shape=jax.ShapeDtypeStruct(x.shape, F32),
        scratch_shapes=[
            pltpu.VMEM((2, BAND, GROUP_Q), F32),
            pltpu.VMEM((2, BAND, GROUP_Q), BF16),
            pltpu.VMEM((tile, C_WIDTH), F32),
            pltpu.VMEM((tile, C_WIDTH), BF16),
        ],
        compiler_params=pltpu.CompilerParams(
            dimension_semantics=("arbitrary", "arbitrary"), vmem_limit_bytes=VMEM_LIMIT),
        name="layer1_attention",
    )(sink_1, x, h, qt, ka, ka, ka, vt, vt, vt, w_gate, w_out_1, final_norm.reshape(1, d))


def kernel(x, norm_0, w_in_0, a_v_norm_0, a_spatial_w_0, a_spatial_b_0, b_group_w_0, b_scale_0,
           w_out_0, norm_1, w_in_1, sink_1, w_out_1, final_norm):
    x1, w_gate, w_out_1b, h1, qt, ka, vt = _layer0(
        x, norm_0, w_in_0, a_v_norm_0, a_spatial_w_0, a_spatial_b_0, b_group_w_0, b_scale_0,
        w_out_0, norm_1, w_in_1, w_out_1)
    return _attention(x1, h1, qt, ka, vt, w_gate, sink_1, w_out_1b, final_norm)
```

```python
import functools
import math

import jax
import jax.numpy as jnp
import numpy as np
from jax import lax
from jax.experimental import pallas as pl
from jax.experimental.pallas import tpu as pltpu

F32 = jnp.float32
BF16 = jnp.bfloat16

D_MODEL = 1024
EPS = 1e-6
NEG_INF = -1e30
LOG2_E = math.log2(math.e)
CHUNK = 128
A_GROUPS = 4
A_WIDTH = D_MODEL
A_GROUP_DIM = A_WIDTH // A_GROUPS
B_WIDTH = D_MODEL
POOL_WINDOWS = (2, 4, 8, 16)
B_GROUP_DIM = B_WIDTH // len(POOL_WINDOWS)
POOL_HALO = max(POOL_WINDOWS) // 2
N_HEADS = 16
N_KV_HEADS = 4
HEAD_DIM = 64
GQA_GROUP = N_HEADS // N_KV_HEADS
C_WIDTH = N_HEADS * HEAD_DIM
KV_WIDTH = N_KV_HEADS * HEAD_DIM
WINDOW = 128
ATTN_BLOCK = 128
BAND = 3 * ATTN_BLOCK
ROPE_THETA = 500000.0
ROT_DIM = HEAD_DIM // 4
LANES = 128
KV_PAD_WIDTH = N_KV_HEADS * LANES
GROUP_Q = GQA_GROUP * ATTN_BLOCK
HEADS_PER_HALF = GQA_GROUP // 2
HALF_Q = HEADS_PER_HALF * ATTN_BLOCK

QKV_COLS = C_WIDTH + 2 * KV_WIDTH
W1_CHUNK_ROWS = 128
W1_CHUNKS = D_MODEL // W1_CHUNK_ROWS
W0_STAGE_CHUNK = (1024, 512)

TILE_L0 = 512
TILE_ATTN = 512
VMEM_LIMIT = 56 * 1024 * 1024


def _rms(x, g):
    return x * lax.rsqrt(jnp.mean(x * x, axis=-1, keepdims=True) + EPS) * g


def _silu(x):
    hx = 0.5 * x
    return hx + hx * jnp.tanh(hx)


def _dot(a, b):
    return jnp.dot(a, b, preferred_element_type=F32)


def _dot_nt(a, b):
    return lax.dot_general(a, b, (((1,), (1,)), ((), ())), preferred_element_type=F32)


def _layer0_kernel(x_ref, xp_ref, xn_ref, n0_ref, win_hbm, gv_ref, ws_ref, bsb_ref, wg_ref,
                   sc_ref, wout_hbm, w1_ref, wo1_ref, n1_ref, w1_hbm, cost_ref, sint_ref,
                   o_ref, wgate_ref, wo1b_ref, h1_ref, qt_ref, ka_ref, vt_ref,
                   hext_ref, v_ref, cat_ref, wqkvt_ref, win_ref, wout_ref, stage_ref, wsem,
                   *, tile, seq):
    j = pl.program_id(1)
    nj = pl.num_programs(1)
    step = pl.program_id(0) * nj + j

    @pl.when(step == 0)
    def _():
        cr, cc = stage_ref.shape[1:]
        chunks = [(win_hbm, win_ref, r0, c0, False)
                  for r0 in range(0, win_ref.shape[0], cr) for c0 in range(0, win_ref.shape[1], cc)]
        chunks += [(wout_hbm, wout_ref, r0, c0, False)
                   for r0 in range(0, wout_ref.shape[0], cr)
                   for c0 in range(0, wout_ref.shape[1], cc)]
        chunks += [(w1_hbm, wqkvt_ref, r0, c0, True)
                   for r0 in range(0, wqkvt_ref.shape[1], cr)
                   for c0 in range(0, wqkvt_ref.shape[0], cc)]

        def chunk_copy(k):
            src, _, r0, c0, _ = chunks[k]
            return pltpu.make_async_copy(src.at[r0:r0 + cr, c0:c0 + cc], stage_ref.at[k % 2],
                                         wsem.at[k % 2])

        chunk_copy(0).start()
        for k, (_, dst, r0, c0, transposed) in enumerate(chunks):
            if k + 1 < len(chunks):
                chunk_copy(k + 1).start()
            chunk_copy(k).wait()
            if transposed:
                dst[c0:c0 + cc, r0:r0 + cr] = stage_ref[k % 2].T.astype(BF16)
            else:
                dst[r0:r0 + cr, c0:c0 + cc] = stage_ref[k % 2].astype(BF16)

    @pl.when(step < W1_CHUNKS)
    def _():
        wgate_ref[...] = w1_ref[:, QKV_COLS:].astype(BF16)
        wo1b_ref[...] = wo1_ref[...].astype(BF16)

    n0 = n0_ref[...]
    x = x_ref[0]
    h = _rms(x, n0).astype(BF16)
    hext_ref[0:tile, :] = h
    hp = jnp.where(j > 0, _rms(xp_ref[0], n0), 0.0)
    hn = jnp.where(j < nj - 1, _rms(xn_ref[0], n0), 0.0)
    hext_ref[tile:tile + 2 * POOL_HALO, :] = jnp.concatenate([hp, hn], axis=0).astype(BF16)

    def col(k):
        return k * D_MODEL

    a_v = jax.nn.gelu(_dot(h, win_ref[:, col(1):col(2)]))
    v_ref[...] = _rms(a_v, gv_ref[...]).astype(BF16)
    for g in range(A_GROUPS):
        c0 = g * A_GROUP_DIM
        c1 = c0 + A_GROUP_DIM
        u = jax.nn.gelu(_dot(h, win_ref[:, col(0) + c0:col(0) + c1]))
        gate = _silu(_dot(h, win_ref[:, col(2) + c0:col(2) + c1]))
        ws_g = ws_ref[g].astype(BF16)
        for c in range(tile // CHUNK):
            r0 = c * CHUNK
            r1 = r0 + CHUNK
            mixed = _dot(ws_g, v_ref[r0:r1, c0:c1]) + bsb_ref[:, c0:c1]
            cat_ref[r0:r1, c0:c1] = (u[r0:r1] * mixed * gate[r0:r1]).astype(BF16)

    bxe = _dot(hext_ref[...], win_ref[:, col(3):col(4)])
    n_ext = tile + 2 * POOL_HALO
    edge_row = lax.broadcasted_iota(jnp.int32, (POOL_HALO, 1), 0)
    for gi, w in enumerate(POOL_WINDOWS):
        c0 = gi * B_GROUP_DIM
        c1 = c0 + B_GROUP_DIM
        half = w // 2
        e = jnp.concatenate([bxe[tile:tile + POOL_HALO, c0:c1], bxe[0:tile, c0:c1],
                             bxe[tile + POOL_HALO:, c0:c1]], axis=0)
        a = e
        span = 1
        while span < half:
            a = a + pltpu.roll(a, n_ext - span, 0)
            span *= 2
        win_sum = (a + pltpu.roll(a, half, 0))[POOL_HALO:POOL_HALO + tile]

        def edge_mean(slab, row0, half=half):
            t = j * tile + row0 + edge_row
            cnt = (jnp.minimum(t + half, seq) - jnp.maximum(t - half, 0)).astype(F32)
            return slab / cnt

        mean = jnp.concatenate(
            [edge_mean(win_sum[0:POOL_HALO], 0),
             win_sum[POOL_HALO:tile - POOL_HALO] * (1.0 / w),
             edge_mean(win_sum[tile - POOL_HALO:], tile - POOL_HALO)], axis=0)
        p = (mean - bxe[0:tile, c0:c1]).astype(BF16)
        gate = _silu(_dot(h, win_ref[:, col(4) + c0:col(4) + c1]))
        yb = _dot(p, wg_ref[gi].astype(BF16)) * sc_ref[:, c0:c1] * gate
        cat_ref[:, A_WIDTH + c0:A_WIDTH + c1] = yb.astype(BF16)

    n1 = n1_ref[...]
    hrows = tile // 2
    for r0 in range(0, tile, hrows):
        rows = slice(r0, r0 + hrows)
        out = x_ref[0, rows, :] + _dot(cat_ref[rows, :], wout_ref[...])
        o_ref[0, rows, :] = out
        h1_ref[0, rows, :] = _rms(out, n1).astype(BF16)
    _qkv_project(h1_ref[0], wqkvt_ref, cost_ref[...], sint_ref[...], qt_ref, ka_ref, vt_ref,
                 slice(0, tile))


def _layer0(x, norm_0, w_in_0, a_v_norm_0, a_spatial_w_0, a_spatial_b_0, b_group_w_0, b_scale_0,
            w_out_0, norm_1, w_in_1, w_out_1):
    bn, seq, d = x.shape
    tile = TILE_L0
    nt = seq // tile
    assert bn * nt >= W1_CHUNKS

    def w1_chunk(b, j):
        return jnp.minimum(b * nt + j, W1_CHUNKS - 1)
    halo_blocks = seq // POOL_HALO
    ws = a_spatial_w_0
    wg = b_group_w_0
    cos_fm, sin_fm = _rope_tables(seq)
    bsb = jnp.repeat(a_spatial_b_0.T, A_GROUP_DIM, axis=1)

    def const(shape):
        return pl.BlockSpec(shape, lambda b, j: (0,) * len(shape), pipeline_mode=pl.Buffered(1))

    per_halo = tile // POOL_HALO
    kern = functools.partial(_layer0_kernel, tile=tile, seq=seq)
    return pl.pallas_call(
        kern,
        grid=(bn, nt),
        in_specs=[
            pl.BlockSpec((1, tile, d), lambda b, j: (b, j, 0)),
            pl.BlockSpec((1, POOL_HALO, d), lambda b, j: (b, jnp.maximum(j * per_halo - 1, 0), 0)),
            pl.BlockSpec((1, POOL_HALO, d),
                         lambda b, j: (b, jnp.minimum((j + 1) * per_halo, halo_blocks - 1), 0)),
            const((1, d)),
            pl.BlockSpec(memory_space=pl.ANY),
            const((1, A_WIDTH)),
            const(ws.shape),
            const(bsb.shape),
            const(wg.shape),
            const((1, B_WIDTH)),
            pl.BlockSpec(memory_space=pl.ANY),
            pl.BlockSpec((W1_CHUNK_ROWS, w_in_1.shape[1]), lambda b, j: (w1_chunk(b, j), 0)),
            pl.BlockSpec((W1_CHUNK_ROWS, d), lambda b, j: (w1_chunk(b, j), 0)),
            const((1, d)),
            pl.BlockSpec(memory_space=pl.ANY),
            pl.BlockSpec((ROT_DIM // 2, tile), lambda b, j: (0, j)),
            pl.BlockSpec((ROT_DIM // 2, tile), lambda b, j: (0, j)),
        ],
        out_specs=[
            pl.BlockSpec((1, tile, d), lambda b, j: (b, j, 0)),
            pl.BlockSpec((W1_CHUNK_ROWS, C_WIDTH), lambda b, j: (w1_chunk(b, j), 0)),
            pl.BlockSpec((W1_CHUNK_ROWS, d), lambda b, j: (w1_chunk(b, j), 0)),
            pl.BlockSpec((1, tile, d), lambda b, j: (b, j, 0)),
            pl.BlockSpec((1, C_WIDTH, tile), lambda b, j: (b, 0, j)),
            pl.BlockSpec((1, tile, KV_PAD_WIDTH), lambda b, j: (b, j, 0)),
            pl.BlockSpec((1, KV_WIDTH, tile), lambda b, j: (b, 0, j)),
        ],
        out_shape=[
            jax.ShapeDtypeStruct(x.shape, F32),
            jax.ShapeDtypeStruct((d, C_WIDTH), BF16),
            jax.ShapeDtypeStruct((C_WIDTH, d), BF16),
            jax.ShapeDtypeStruct((bn, seq, d), BF16),
            jax.ShapeDtypeStruct((bn, C_WIDTH, seq), BF16),
            jax.ShapeDtypeStruct((bn, seq, KV_PAD_WIDTH), BF16),
            jax.ShapeDtypeStruct((bn, KV_WIDTH, seq), BF16),
        ],
        scratch_shapes=[
            pltpu.VMEM((tile + 2 * POOL_HALO, d), BF16),
            pltpu.VMEM((tile, A_WIDTH), BF16),
            pltpu.VMEM((tile, A_WIDTH + B_WIDTH), BF16),
            pltpu.VMEM((QKV_COLS, d), BF16),
            pltpu.VMEM(w_in_0.shape, BF16),
            pltpu.VMEM(w_out_0.shape, BF16),
            pltpu.VMEM((2,) + W0_STAGE_CHUNK, F32),
            pltpu.SemaphoreType.DMA((2,)),
        ],
        compiler_params=pltpu.CompilerParams(
            dimension_semantics=("arbitrary", "arbitrary"), vmem_limit_bytes=VMEM_LIMIT),
        name="layer0_mixer",
    )(x, x, x, norm_0.reshape(1, d), w_in_0, a_v_norm_0.reshape(1, A_WIDTH), ws, bsb, wg,
      b_scale_0.reshape(1, B_WIDTH), w_out_0, w_in_1, w_out_1, norm_1.reshape(1, d), w_in_1,
      cos_fm, sin_fm)


def _qkv_project(h, w_ref, ct, st, qt_ref, ka_ref, vt_ref, pos):
    half = ROT_DIM // 2
    n = h.shape[0]
    scale = HEAD_DIM ** -0.5 * LOG2_E

    qkv = _dot_nt(w_ref[...], h)

    def rope_rows(r0):
        x1 = qkv[r0:r0 + half]
        x2 = qkv[r0 + half:r0 + ROT_DIM]
        return jnp.concatenate([x1 * ct - x2 * st, x2 * ct + x1 * st], axis=0)

    for hd in range(N_HEADS):
        r0 = hd * HEAD_DIM
        qt_ref[0, r0:r0 + ROT_DIM, pos] = (rope_rows(r0) * scale).astype(BF16)
        qt_ref[0, r0 + ROT_DIM:r0 + HEAD_DIM, pos] = (
            qkv[r0 + ROT_DIM:r0 + HEAD_DIM] * scale).astype(BF16)
    vt_ref[0, :, pos] = qkv[C_WIDTH + KV_WIDTH:].astype(BF16)

    pad = jnp.zeros((LANES - HEAD_DIM, n), F32)
    for kh in range(N_KV_HEADS):
        r0 = C_WIDTH + kh * HEAD_DIM
        kt = jnp.concatenate([rope_rows(r0), qkv[r0 + ROT_DIM:r0 + HEAD_DIM], pad], axis=0)
        ka_ref[0, pos, kh * LANES:(kh + 1) * LANES] = kt.T.astype(BF16)


def _rope_tables(seq):
    inv = ROPE_THETA ** (-np.arange(0, ROT_DIM, 2, dtype=np.float64) / ROT_DIM)
    ang = np.arange(seq, dtype=np.float64)[None, :] * inv[:, None]
    return jnp.asarray(np.cos(ang), F32), jnp.asarray(np.sin(ang), F32)


def _attn_kernel(sink_ref, x_ref, h_ref, qt_ref, kap_ref, kat_ref, kan_ref, vtp_ref, vtt_ref,
                 vtn_ref, wgate_ref, wout_ref, fn_ref, o_ref,
                 s_ref, p_ref, sg_ref, y_ref, *, tile, seq):
    j = pl.program_id(1)
    nblk = tile // ATTN_BLOCK
    last_blk = seq // ATTN_BLOCK - 1

    def k_band(i, kh):
        cols = slice(kh * LANES, (kh + 1) * LANES)
        lo = max(i - 1, 0) * ATTN_BLOCK
        hi = min(i + 2, nblk) * ATTN_BLOCK
        parts = [kat_ref[0, lo:hi, cols]]
        if i == 0:
            parts.insert(0, kap_ref[0, :, cols])
        if i == nblk - 1:
            parts.append(kan_ref[0, :, cols])
        return jnp.concatenate(parts, axis=0)

    def vt_band(i, kh):
        rows = slice(kh * HEAD_DIM, (kh + 1) * HEAD_DIM)
        lo = max(i - 1, 0) * ATTN_BLOCK
        hi = min(i + 2, nblk) * ATTN_BLOCK
        parts = [vtt_ref[0, rows, lo:hi]]
        if i == 0:
            parts.insert(0, vtp_ref[0, rows, :])
        if i == nblk - 1:
            parts.append(vtn_ref[0, rows, :])
        return parts
    diff = (lax.broadcasted_iota(jnp.int32, (ATTN_BLOCK, ATTN_BLOCK), 0)
            - lax.broadcasted_iota(jnp.int32, (ATTN_BLOCK, ATTN_BLOCK), 1))
    never = 2 * ATTN_BLOCK
    ones_rows = jnp.ones((HEAD_DIM, BAND), BF16)

    def scores(u, half):
        i, kh = divmod(u, N_KV_HEADS)
        slot = u % 2
        c0 = i * ATTN_BLOCK
        h0 = kh * GQA_GROUP + half * HEADS_PER_HALF
        qg = jnp.concatenate(
            [qt_ref[0, (h0 + g) * HEAD_DIM:(h0 + g + 1) * HEAD_DIM, c0:c0 + ATTN_BLOCK]
             for g in range(HEADS_PER_HALF)], axis=1)
        rq = jnp.concatenate([qg, jnp.zeros((LANES - HEAD_DIM, HALF_Q), BF16)], axis=0)
        ka = k_band(i, kh)
        s_ref[slot, :, half * HALF_Q:(half + 1) * HALF_Q] = _dot(ka, rq)

    def softmax_head(u, g):
        i, kh = divmod(u, N_KV_HEADS)
        slot = u % 2
        gblk = j * nblk + i
        ok_prev = diff >= jnp.where(gblk > 0, 0, never)
        ok_next = diff <= jnp.where(gblk < last_blk, 0, -never)
        cs = slice(g * ATTN_BLOCK, (g + 1) * ATTN_BLOCK)
        sink = sink_ref[kh * GQA_GROUP + g] * LOG2_E
        s0 = jnp.where(ok_prev, s_ref[slot, 0:ATTN_BLOCK, cs], NEG_INF)
        s1 = s_ref[slot, ATTN_BLOCK:2 * ATTN_BLOCK, cs]
        s2 = jnp.where(ok_next, s_ref[slot, 2 * ATTN_BLOCK:, cs], NEG_INF)
        m = jnp.max(jnp.maximum(jnp.maximum(s0, s1), s2), axis=0, keepdims=True)
        m = jnp.maximum(m, sink)
        p0 = jnp.exp2(s0 - m)
        p1 = jnp.exp2(s1 - m)
        p2 = jnp.exp2(s2 - m)
        p_ref[slot, 0:ATTN_BLOCK, cs] = p0.astype(BF16)
        p_ref[slot, ATTN_BLOCK:2 * ATTN_BLOCK, cs] = p1.astype(BF16)
        p_ref[slot, 2 * ATTN_BLOCK:, cs] = p2.astype(BF16)
        return jnp.exp2(sink - m)

    def weighted(u, half, sink_share):
        i, kh = divmod(u, N_KV_HEADS)
        slot = u % 2
        c0 = i * ATTN_BLOCK
        vt = jnp.concatenate([jnp.concatenate(vt_band(i, kh), axis=1), ones_rows],
                             axis=0)
        ot = _dot(vt, p_ref[slot, :, half * HALF_Q:(half + 1) * HALF_Q])
        parts = []
        for g in range(HEADS_PER_HALF):
            cq = slice(g * ATTN_BLOCK, (g + 1) * ATTN_BLOCK)
            l = ot[HEAD_DIM:HEAD_DIM + 1, cq] + sink_share[g]
            parts.append(ot[0:HEAD_DIM, cq] * (1.0 / l))
        yt = jnp.concatenate(parts, axis=0)
        f0 = (kh * GQA_GROUP + half * HEADS_PER_HALF) * HEAD_DIM
        cs = slice(f0, f0 + HEADS_PER_HALF * HEAD_DIM)
        y = yt.T * sg_ref[c0:c0 + ATTN_BLOCK, cs]
        y_ref[c0:c0 + ATTN_BLOCK, cs] = y.astype(BF16)

    def gate_chunk(kh):
        cs = slice(kh * GQA_GROUP * HEAD_DIM, (kh + 1) * GQA_GROUP * HEAD_DIM)
        sg_ref[:, cs] = _silu(_dot(h_ref[0], wgate_ref[:, cs]))

    def finish_rows(r0, n):
        rows = slice(r0, r0 + n)
        out = x_ref[0, rows, :] + _dot(y_ref[rows, :], wout_ref[...])
        o_ref[0, rows, :] = _rms(out, fn_ref[...])

    units = nblk * N_KV_HEADS
    inv = {}
    for step in range(units + 2):
        sc = step if step < units else None
        sm = step - 1 if 1 <= step <= units else None
        wt = step - 2 if step >= 2 else None
        if step < N_KV_HEADS:
            gate_chunk(step)
        for half in range(2):
            if sc is not None:
                scores(sc, half)
            if sm is not None:
                inv[sm, 2 * half] = softmax_head(sm, 2 * half)
            if wt is not None:
                weighted(wt, half, [inv.pop((wt, half * HEADS_PER_HALF + g))
                                    for g in range(HEADS_PER_HALF)])
            if sm is not None:
                inv[sm, 2 * half + 1] = softmax_head(sm, 2 * half + 1)
        if wt is not None and (wt + 1) % (units // 2) == 0:
            finish_rows((wt + 1 - units // 2) // N_KV_HEADS * ATTN_BLOCK, tile // 2)


def _attention(x, h, qt, ka, vt, w_gate, sink_1, w_out_1, final_norm):
    bn, seq, d = x.shape
    tile = TILE_ATTN
    nt = seq // tile
    per = tile // ATTN_BLOCK
    nb = seq // ATTN_BLOCK

    def const(shape):
        return pl.BlockSpec(shape, lambda b, j: (0,) * len(shape), pipeline_mode=pl.Buffered(1))

    def prev_blk(j):
        return jnp.maximum(j * per - 1, 0)

    def next_blk(j):
        return jnp.minimum((j + 1) * per, nb - 1)

    kern = functools.partial(_attn_kernel, tile=tile, seq=seq)
    return pl.pallas_call(
        kern,
        grid=(bn, nt),
        in_specs=[
            pl.BlockSpec(memory_space=pltpu.SMEM),
            pl.BlockSpec((1, tile, d), lambda b, j: (b, j, 0)),
            pl.BlockSpec((1, tile, d), lambda b, j: (b, j, 0)),
            pl.BlockSpec((1, C_WIDTH, tile), lambda b, j: (b, 0, j)),
            pl.BlockSpec((1, ATTN_BLOCK, KV_PAD_WIDTH), lambda b, j: (b, prev_blk(j), 0)),
            pl.BlockSpec((1, tile, KV_PAD_WIDTH), lambda b, j: (b, j, 0)),
            pl.BlockSpec((1, ATTN_BLOCK, KV_PAD_WIDTH), lambda b, j: (b, next_blk(j), 0)),
            pl.BlockSpec((1, KV_WIDTH, ATTN_BLOCK), lambda b, j: (b, 0, prev_blk(j))),
            pl.BlockSpec((1, KV_WIDTH, tile), lambda b, j: (b, 0, j)),
            pl.BlockSpec((1, KV_WIDTH, ATTN_BLOCK), lambda b, j: (b, 0, next_blk(j))),
            const(w_gate.shape),
            const(w_out_1.shape),
            const((1, d)),
        ],
        out_specs=pl.BlockSpec((1, tile, d), lambda b, j: (b, j, 0)),
        out_shape=jax.ShapeDtypeStruct(x.shape, F32),
        scratch_shapes=[
            pltpu.VMEM((2, BAND, GROUP_Q), F32),
            pltpu.VMEM((2, BAND, GROUP_Q), BF16),
            pltpu.VMEM((tile, C_WIDTH), F32),
            pltpu.VMEM((tile, C_WIDTH), BF16),
        ],
        compiler_params=pltpu.CompilerParams(
            dimension_semantics=("arbitrary", "arbitrary"), vmem_limit_bytes=VMEM_LIMIT),
        name="layer1_attention",
    )(sink_1, x, h, qt, ka, ka, ka, vt, vt, vt, w_gate, w_out_1, final_norm.reshape(1, d))


def kernel(x, norm_0, w_in_0, a_v_norm_0, a_spatial_w_0, a_spatial_b_0, b_group_w_0, b_scale_0,
           w_out_0, norm_1, w_in_1, sink_1, w_out_1, final_norm):
    x1, w_gate, w_out_1b, h1, qt, ka, vt = _layer0(
        x, norm_0, w_in_0, a_v_norm_0, a_spatial_w_0, a_spatial_b_0, b_group_w_0, b_scale_0,
        w_out_0, norm_1, w_in_1, w_out_1)
    return _attention(x1, h1, qt, ka, vt, w_gate, sink_1, w_out_1b, final_norm)
```

```python
import functools
import math

import jax
import jax.numpy as jnp
import numpy as np
from jax import lax
from jax.experimental import pallas as pl
from jax.experimental.pallas import tpu as pltpu

F32 = jnp.float32
BF16 = jnp.bfloat16

D_MODEL = 1024
EPS = 1e-6
NEG_INF = -1e30
LOG2_E = math.log2(math.e)
CHUNK = 128
A_GROUPS = 4
A_WIDTH = D_MODEL
A_GROUP_DIM = A_WIDTH // A_GROUPS
B_WIDTH = D_MODEL
POOL_WINDOWS = (2, 4, 8, 16)
B_GROUP_DIM = B_WIDTH // len(POOL_WINDOWS)
POOL_HALO = max(POOL_WINDOWS) // 2
N_HEADS = 16
N_KV_HEADS = 4
HEAD_DIM = 64
GQA_GROUP = N_HEADS // N_KV_HEADS
C_WIDTH = N_HEADS * HEAD_DIM
KV_WIDTH = N_KV_HEADS * HEAD_DIM
WINDOW = 128
ATTN_BLOCK = 128
BAND = 3 * ATTN_BLOCK
ROPE_THETA = 500000.0
ROT_DIM = HEAD_DIM // 4
LANES = 128
KV_PAD_WIDTH = N_KV_HEADS * LANES
GROUP_Q = GQA_GROUP * ATTN_BLOCK
HEADS_PER_HALF = GQA_GROUP // 2
HALF_Q = HEADS_PER_HALF * ATTN_BLOCK

QKV_COLS = C_WIDTH + 2 * KV_WIDTH
W1_CHUNK_ROWS = 128
W1_CHUNKS = D_MODEL // W1_CHUNK_ROWS
W0_STAGE_CHUNK = (1024, 512)

TILE_L0 = 512
TILE_ATTN = 512
VMEM_LIMIT = 56 * 1024 * 1024


def _rms(x, g):
    return x * lax.rsqrt(jnp.mean(x * x, axis=-1, keepdims=True) + EPS) * g


def _silu(x):
    hx = 0.5 * x
    return hx + hx * jnp.tanh(hx)


def _dot(a, b):
    return jnp.dot(a, b, preferred_element_type=F32)


def _dot_nt(a, b):
    return lax.dot_general(a, b, (((1,), (1,)), ((), ())), preferred_element_type=F32)


def _layer0_kernel(x_ref, xp_ref, xn_ref, n0_ref, win_hbm, gv_ref, ws_ref, bsb_ref, wg_ref,
                   sc_ref, wout_hbm, w1_ref, wo1_ref, n1_ref, w1_hbm, cost_ref, sint_ref,
                   o_ref, wgate_ref, wo1b_ref, h1_ref, qt_ref, ka_ref, vt_ref,
                   hext_ref, v_ref, cat_ref, wqkvt_ref, win_ref, wout_ref, stage_ref, wsem,
                   *, tile, seq):
    j = pl.program_id(1)
    nj = pl.num_programs(1)
    step = pl.program_id(0) * nj + j

    @pl.when(step == 0)
    def _():
        cr, cc = stage_ref.shape[1:]
        chunks = [(win_hbm, win_ref, r0, c0, False)
                  for r0 in range(0, win_ref.shape[0], cr) for c0 in range(0, win_ref.shape[1], cc)]
        chunks += [(wout_hbm, wout_ref, r0, c0, False)
                   for r0 in range(0, wout_ref.shape[0], cr)
                   for c0 in range(0, wout_ref.shape[1], cc)]
        chunks += [(w1_hbm, wqkvt_ref, r0, c0, True)
                   for r0 in range(0, wqkvt_ref.shape[1], cr)
                   for c0 in range(0, wqkvt_ref.shape[0], cc)]

        def chunk_copy(k):
            src, _, r0, c0, _ = chunks[k]
            return pltpu.make_async_copy(src.at[r0:r0 + cr, c0:c0 + cc], stage_ref.at[k % 2],
                                         wsem.at[k % 2])

        chunk_copy(0).start()
        for k, (_, dst, r0, c0, transposed) in enumerate(chunks):
            if k + 1 < len(chunks):
                chunk_copy(k + 1).start()
            chunk_copy(k).wait()
            if transposed:
                dst[c0:c0 + cc, r0:r0 + cr] = stage_ref[k % 2].T.astype(BF16)
            else:
                dst[r0:r0 + cr, c0:c0 + cc] = stage_ref[k % 2].astype(BF16)

    @pl.when(step < W1_CHUNKS)
    def _():
        wgate_ref[...] = w1_ref[:, QKV_COLS:].astype(BF16)
        wo1b_ref[...] = wo1_ref[...].astype(BF16)

    n0 = n0_ref[...]
    x = x_ref[0]
    h = _rms(x, n0).astype(BF16)
    hext_ref[0:tile, :] = h
    hp = jnp.where(j > 0, _rms(xp_ref[0], n0), 0.0)
    hn = jnp.where(j < nj - 1, _rms(xn_ref[0], n0), 0.0)
    hext_ref[tile:tile + 2 * POOL_HALO, :] = jnp.concatenate([hp, hn], axis=0).astype(BF16)

    def col(k):
        return k * D_MODEL

    a_v = jax.nn.gelu(_dot(h, win_ref[:, col(1):col(2)]))
    v_ref[...] = _rms(a_v, gv_ref[...]).astype(BF16)
    for g in range(A_GROUPS):
        c0 = g * A_GROUP_DIM
        c1 = c0 + A_GROUP_DIM
        u = jax.nn.gelu(_dot(h, win_ref[:, col(0) + c0:col(0) + c1]))
        gate = _silu(_dot(h, win_ref[:, col(2) + c0:col(2) + c1]))
        ws_g = ws_ref[g].astype(BF16)
        for c in range(tile // CHUNK):
            r0 = c * CHUNK
            r1 = r0 + CHUNK
            mixed = _dot(ws_g, v_ref[r0:r1, c0:c1]) + bsb_ref[:, c0:c1]
            cat_ref[r0:r1, c0:c1] = (u[r0:r1] * mixed * gate[r0:r1]).astype(BF16)

    bxe = _dot(hext_ref[...], win_ref[:, col(3):col(4)])
    n_ext = tile + 2 * POOL_HALO
    edge_row = lax.broadcasted_iota(jnp.int32, (POOL_HALO, 1), 0)
    for gi, w in enumerate(POOL_WINDOWS):
        c0 = gi * B_GROUP_DIM
        c1 = c0 + B_GROUP_DIM
        half = w // 2
        e = jnp.concatenate([bxe[tile:tile + POOL_HALO, c0:c1], bxe[0:tile, c0:c1],
                             bxe[tile + POOL_HALO:, c0:c1]], axis=0)
        a = e
        span = 1
        while span < half:
            a = a + pltpu.roll(a, n_ext - span, 0)
            span *= 2
        win_sum = (a + pltpu.roll(a, half, 0))[POOL_HALO:POOL_HALO + tile]

        def edge_mean(slab, row0, half=half):
            t = j * tile + row0 + edge_row
            cnt = (jnp.minimum(t + half, seq) - jnp.maximum(t - half, 0)).astype(F32)
            return slab / cnt

        mean = jnp.concatenate(
            [edge_mean(win_sum[0:POOL_HALO], 0),
             win_sum[POOL_HALO:tile - POOL_HALO] * (1.0 / w),
             edge_mean(win_sum[tile - POOL_HALO:], tile - POOL_HALO)], axis=0)
        p = (mean - bxe[0:tile, c0:c1]).astype(BF16)
        gate = _silu(_dot(h, win_ref[:, col(4) + c0:col(4) + c1]))
        yb = _dot(p, wg_ref[gi].astype(BF16)) * sc_ref[:, c0:c1] * gate
        cat_ref[:, A_WIDTH + c0:A_WIDTH + c1] = yb.astype(BF16)

    n1 = n1_ref[...]
    hrows = tile // 2
    for r0 in range(0, tile, hrows):
        rows = slice(r0, r0 + hrows)
        out = x_ref[0, rows, :] + _dot(cat_ref[rows, :], wout_ref[...])
        o_ref[0, rows, :] = out
        h1_ref[0, rows, :] = _rms(out, n1).astype(BF16)
    _qkv_project(h1_ref[0], wqkvt_ref, cost_ref[...], sint_ref[...], qt_ref, ka_ref, vt_ref,
                 slice(0, tile))


def _layer0(x, norm_0, w_in_0, a_v_norm_0, a_spatial_w_0, a_spatial_b_0, b_group_w_0, b_scale_0,
            w_out_0, norm_1, w_in_1, w_out_1):
    bn, seq, d = x.shape
    tile = TILE_L0
    nt = seq // tile
    assert bn * nt >= W1_CHUNKS

    def w1_chunk(b, j):
        return jnp.minimum(b * nt + j, W1_CHUNKS - 1)
    halo_blocks = seq // POOL_HALO
    ws = a_spatial_w_0
    wg = b_group_w_0
    cos_fm, sin_fm = _rope_tables(seq)
    bsb = jnp.repeat(a_spatial_b_0.T, A_GROUP_DIM, axis=1)

    def const(shape):
        return pl.BlockSpec(shape, lambda b, j: (0,) * len(shape), pipeline_mode=pl.Buffered(1))

    per_halo = tile // POOL_HALO
    kern = functools.partial(_layer0_kernel, tile=tile, seq=seq)
    return pl.pallas_call(
        kern,
        grid=(bn, nt),
        in_specs=[
            pl.BlockSpec((1, tile, d), lambda b, j: (b, j, 0)),
            pl.BlockSpec((1, POOL_HALO, d), lambda b, j: (b, jnp.maximum(j * per_halo - 1, 0), 0)),
            pl.BlockSpec((1, POOL_HALO, d),
                         lambda b, j: (b, jnp.minimum((j + 1) * per_halo, halo_blocks - 1), 0)),
            const((1, d)),
            pl.BlockSpec(memory_space=pl.ANY),
            const((1, A_WIDTH)),
            const(ws.shape),
            const(bsb.shape),
            const(wg.shape),
            const((1, B_WIDTH)),
            pl.BlockSpec(memory_space=pl.ANY),
            pl.BlockSpec((W1_CHUNK_ROWS, w_in_1.shape[1]), lambda b, j: (w1_chunk(b, j), 0)),
            pl.BlockSpec((W1_CHUNK_ROWS, d), lambda b, j: (w1_chunk(b, j), 0)),
            const((1, d)),
            pl.BlockSpec(memory_space=pl.ANY),
            pl.BlockSpec((ROT_DIM // 2, tile), lambda b, j: (0, j)),
            pl.BlockSpec((ROT_DIM // 2, tile), lambda b, j: (0, j)),
        ],
        out_specs=[
            pl.BlockSpec((1, tile, d), lambda b, j: (b, j, 0)),
            pl.BlockSpec((W1_CHUNK_ROWS, C_WIDTH), lambda b, j: (w1_chunk(b, j), 0)),
            pl.BlockSpec((W1_CHUNK_ROWS, d), lambda b, j: (w1_chunk(b, j), 0)),
            pl.BlockSpec((1, tile, d), lambda b, j: (b, j, 0)),
            pl.BlockSpec((1, 1, C_WIDTH, tile), lambda b, j: (b, j, 0, 0)),
            pl.BlockSpec((1, tile, KV_PAD_WIDTH), lambda b, j: (b, j, 0)),
            pl.BlockSpec((1, 1, KV_WIDTH, tile), lambda b, j: (b, j, 0, 0)),
        ],
        out_shape=[
            jax.ShapeDtypeStruct(x.shape, F32),
            jax.ShapeDtypeStruct((d, C_WIDTH), BF16),
            jax.ShapeDtypeStruct((C_WIDTH, d), BF16),
            jax.ShapeDtypeStruct((bn, seq, d), BF16),
            jax.ShapeDtypeStruct((bn, nt, C_WIDTH, tile), BF16),
            jax.ShapeDtypeStruct((bn, seq, KV_PAD_WIDTH), BF16),
            jax.ShapeDtypeStruct((bn, nt, KV_WIDTH, tile), BF16),
        ],
        scratch_shapes=[
            pltpu.VMEM((tile + 2 * POOL_HALO, d), BF16),
            pltpu.VMEM((tile, A_WIDTH), BF16),
            pltpu.VMEM((tile, A_WIDTH + B_WIDTH), BF16),
            pltpu.VMEM((QKV_COLS, d), BF16),
            pltpu.VMEM(w_in_0.shape, BF16),
            pltpu.VMEM(w_out_0.shape, BF16),
            pltpu.VMEM((2,) + W0_STAGE_CHUNK, F32),
            pltpu.SemaphoreType.DMA((2,)),
        ],
        compiler_params=pltpu.CompilerParams(
            dimension_semantics=("arbitrary", "arbitrary"), vmem_limit_bytes=VMEM_LIMIT),
        name="layer0_mixer",
    )(x, x, x, norm_0.reshape(1, d), w_in_0, a_v_norm_0.reshape(1, A_WIDTH), ws, bsb, wg,
      b_scale_0.reshape(1, B_WIDTH), w_out_0, w_in_1, w_out_1, norm_1.reshape(1, d), w_in_1,
      cos_fm, sin_fm)


def _qkv_project(h, w_ref, ct, st, qt_ref, ka_ref, vt_ref, pos):
    half = ROT_DIM // 2
    n = h.shape[0]
    scale = HEAD_DIM ** -0.5 * LOG2_E

    qkv = _dot_nt(w_ref[...], h)

    def rope_rows(r0):
        x1 = qkv[r0:r0 + half]
        x2 = qkv[r0 + half:r0 + ROT_DIM]
        return jnp.concatenate([x1 * ct - x2 * st, x2 * ct + x1 * st], axis=0)

    for hd in range(N_HEADS):
        r0 = hd * HEAD_DIM
        qt_ref[0, 0, r0:r0 + ROT_DIM, pos] = (rope_rows(r0) * scale).astype(BF16)
        qt_ref[0, 0, r0 + ROT_DIM:r0 + HEAD_DIM, pos] = (
            qkv[r0 + ROT_DIM:r0 + HEAD_DIM] * scale).astype(BF16)
    vt_ref[0, 0, :, pos] = qkv[C_WIDTH + KV_WIDTH:].astype(BF16)

    pad = jnp.zeros((LANES - HEAD_DIM, n), F32)
    for kh in range(N_KV_HEADS):
        r0 = C_WIDTH + kh * HEAD_DIM
        kt = jnp.concatenate([rope_rows(r0), qkv[r0 + ROT_DIM:r0 + HEAD_DIM], pad], axis=0)
        ka_ref[0, pos, kh * LANES:(kh + 1) * LANES] = kt.T.astype(BF16)


def _rope_tables(seq):
    inv = ROPE_THETA ** (-np.arange(0, ROT_DIM, 2, dtype=np.float64) / ROT_DIM)
    ang = np.arange(seq, dtype=np.float64)[None, :] * inv[:, None]
    return jnp.asarray(np.cos(ang), F32), jnp.asarray(np.sin(ang), F32)


def _attn_kernel(sink_ref, x_ref, h_ref, qt_ref, kap_ref, kat_ref, kan_ref, vtp_ref, vtt_ref,
                 vtn_ref, wgate_ref, wout_ref, fn_ref, o_ref,
                 s_ref, p_ref, sg_ref, y_ref, *, tile, seq):
    j = pl.program_id(1)
    nblk = tile // ATTN_BLOCK
    last_blk = seq // ATTN_BLOCK - 1

    def k_band(i, kh):
        cols = slice(kh * LANES, (kh + 1) * LANES)
        lo = max(i - 1, 0) * ATTN_BLOCK
        hi = min(i + 2, nblk) * ATTN_BLOCK
        parts = [kat_ref[0, lo:hi, cols]]
        if i == 0:
            parts.insert(0, kap_ref[0, :, cols])
        if i == nblk - 1:
            parts.append(kan_ref[0, :, cols])
        return jnp.concatenate(parts, axis=0)

    def vt_band(i, kh):
        rows = slice(kh * HEAD_DIM, (kh + 1) * HEAD_DIM)
        lo = max(i - 1, 0) * ATTN_BLOCK
        hi = min(i + 2, nblk) * ATTN_BLOCK
        parts = [vtt_ref[0, 0, rows, lo:hi]]
        if i == 0:
            parts.insert(0, vtp_ref[0, 0, rows, :])
        if i == nblk - 1:
            parts.append(vtn_ref[0, 0, rows, :])
        return parts
    diff = (lax.broadcasted_iota(jnp.int32, (ATTN_BLOCK, ATTN_BLOCK), 0)
            - lax.broadcasted_iota(jnp.int32, (ATTN_BLOCK, ATTN_BLOCK), 1))
    never = 2 * ATTN_BLOCK
    ones_rows = jnp.ones((HEAD_DIM // 2, BAND), BF16)

    def scores(u, half):
        i, kh = divmod(u, N_KV_HEADS)
        slot = u % 2
        c0 = i * ATTN_BLOCK
        h0 = kh * GQA_GROUP + half * HEADS_PER_HALF
        qg = jnp.concatenate(
            [qt_ref[0, 0, (h0 + g) * HEAD_DIM:(h0 + g + 1) * HEAD_DIM, c0:c0 + ATTN_BLOCK]
             for g in range(HEADS_PER_HALF)], axis=1)
        rq = jnp.concatenate([qg, jnp.zeros((LANES - HEAD_DIM, HALF_Q), BF16)], axis=0)
        ka = k_band(i, kh)
        s_ref[slot, :, half * HALF_Q:(half + 1) * HALF_Q] = _dot(ka, rq)

    def softmax_head(u, g):
        i, kh = divmod(u, N_KV_HEADS)
        slot = u % 2
        gblk = j * nblk + i
        ok_prev = diff >= jnp.where(gblk > 0, 0, never)
        ok_next = diff <= jnp.where(gblk < last_blk, 0, -never)
        cs = slice(g * ATTN_BLOCK, (g + 1) * ATTN_BLOCK)
        sink = sink_ref[kh * GQA_GROUP + g] * LOG2_E
        s0 = jnp.where(ok_prev, s_ref[slot, 0:ATTN_BLOCK, cs], NEG_INF)
        s1 = s_ref[slot, ATTN_BLOCK:2 * ATTN_BLOCK, cs]
        s2 = jnp.where(ok_next, s_ref[slot, 2 * ATTN_BLOCK:, cs], NEG_INF)
        m = jnp.max(jnp.maximum(jnp.maximum(s0, s1), s2), axis=0, keepdims=True)
        m = jnp.maximum(m, sink)
        p0 = jnp.exp2(s0 - m)
        p1 = jnp.exp2(s1 - m)
        p2 = jnp.exp2(s2 - m)
        p_ref[slot, 0:ATTN_BLOCK, cs] = p0.astype(BF16)
        p_ref[slot, ATTN_BLOCK:2 * ATTN_BLOCK, cs] = p1.astype(BF16)
        p_ref[slot, 2 * ATTN_BLOCK:, cs] = p2.astype(BF16)
        return jnp.exp2(sink - m)

    def weighted(u, half, sink_share):
        i, kh = divmod(u, N_KV_HEADS)
        slot = u % 2
        c0 = i * ATTN_BLOCK
        vt = jnp.concatenate([jnp.concatenate(vt_band(i, kh), axis=1), ones_rows],
                             axis=0)
        ot = _dot(vt, p_ref[slot, :, half * HALF_Q:(half + 1) * HALF_Q])
        parts = []
        for g in range(HEADS_PER_HALF):
            cq = slice(g * ATTN_BLOCK, (g + 1) * ATTN_BLOCK)
            l = ot[HEAD_DIM:HEAD_DIM + 1, cq] + sink_share[g]
            parts.append(ot[0:HEAD_DIM, cq] * (1.0 / l))
        yt = jnp.concatenate(parts, axis=0)
        f0 = (kh * GQA_GROUP + half * HEADS_PER_HALF) * HEAD_DIM
        cs = slice(f0, f0 + HEADS_PER_HALF * HEAD_DIM)
        y = yt.T * sg_ref[c0:c0 + ATTN_BLOCK, cs]
        y_ref[c0:c0 + ATTN_BLOCK, cs] = y.astype(BF16)

    def gate_chunk(kh):
        cs = slice(kh * GQA_GROUP * HEAD_DIM, (kh + 1) * GQA_GROUP * HEAD_DIM)
        sg_ref[:, cs] = _silu(_dot(h_ref[0], wgate_ref[:, cs]))

    def finish_rows(r0, n):
        rows = slice(r0, r0 + n)
        out = x_ref[0, rows, :] + _dot(y_ref[rows, :], wout_ref[...])
        o_ref[0, rows, :] = _rms(out, fn_ref[...])

    units = nblk * N_KV_HEADS
    inv = {}
    for step in range(units + 2):
        sc = step if step < units else None
        sm = step - 1 if 1 <= step <= units else None
        wt = step - 2 if step >= 2 else None
        if step < N_KV_HEADS:
            gate_chunk(step)
        for half in range(2):
            if sc is not None:
                scores(sc, half)
            if sm is not None:
                inv[sm, 2 * half] = softmax_head(sm, 2 * half)
            if wt is not None:
                weighted(wt, half, [inv.pop((wt, half * HEADS_PER_HALF + g))
                                    for g in range(HEADS_PER_HALF)])
            if sm is not None:
                inv[sm, 2 * half + 1] = softmax_head(sm, 2 * half + 1)
        if wt is not None and (wt + 1) % (units // 2) == 0:
            finish_rows((wt + 1 - units // 2) // N_KV_HEADS * ATTN_BLOCK, tile // 2)


def _attention(x, h, qt, ka, vt, w_gate, sink_1, w_out_1, final_norm):
    bn, seq, d = x.shape
    tile = TILE_ATTN
    assert tile == TILE_L0
    nt = seq // tile
    per = tile // ATTN_BLOCK
    nb = seq // ATTN_BLOCK

    def const(shape):
        return pl.BlockSpec(shape, lambda b, j: (0,) * len(shape), pipeline_mode=pl.Buffered(1))

    def prev_blk(j):
        return jnp.maximum(j * per - 1, 0)

    def next_blk(j):
        return jnp.minimum((j + 1) * per, nb - 1)

    kern = functools.partial(_attn_kernel, tile=tile, seq=seq)
    return pl.pallas_call(
        kern,
        grid=(bn, nt),
        in_specs=[
            pl.BlockSpec(memory_space=pltpu.SMEM),
            pl.BlockSpec((1, tile, d), lambda b, j: (b, j, 0)),
            pl.BlockSpec((1, tile, d), lambda b, j: (b, j, 0)),
            pl.BlockSpec((1, 1, C_WIDTH, tile), lambda b, j: (b, j, 0, 0)),
            pl.BlockSpec((1, ATTN_BLOCK, KV_PAD_WIDTH), lambda b, j: (b, prev_blk(j), 0)),
            pl.BlockSpec((1, tile, KV_PAD_WIDTH), lambda b, j: (b, j, 0)),
            pl.BlockSpec((1, ATTN_BLOCK, KV_PAD_WIDTH), lambda b, j: (b, next_blk(j), 0)),
            pl.BlockSpec((1, 1, KV_WIDTH, ATTN_BLOCK),
                         lambda b, j: (b, jnp.maximum(j - 1, 0), 0, per - 1)),
            pl.BlockSpec((1, 1, KV_WIDTH, tile), lambda b, j: (b, j, 0, 0)),
            pl.BlockSpec((1, 1, KV_WIDTH, ATTN_BLOCK),
                         lambda b, j: (b, jnp.minimum(j + 1, nt - 1), 0, 0)),
            const(w_gate.shape),
            const(w_out_1.shape),
            const((1, d)),
        ],
        out_specs=pl.BlockSpec((1, tile, d), lambda b, j: (b, j, 0)),
        out_shape=jax.ShapeDtypeStruct(x.shape, F32),
        scratch_shapes=[
            pltpu.VMEM((2, BAND, GROUP_Q), F32),
            pltpu.VMEM((2, BAND, GROUP_Q), BF16),
            pltpu.VMEM((tile, C_WIDTH), F32),
            pltpu.VMEM((tile, C_WIDTH), BF16),
        ],
        compiler_params=pltpu.CompilerParams(
            dimension_semantics=("arbitrary", "arbitrary"), vmem_limit_bytes=VMEM_LIMIT),
        name="layer1_attention",
    )(sink_1, x, h, qt, ka, ka, ka, vt, vt, vt, w_gate, w_out_1, final_norm.reshape(1, d))


def kernel(x, norm_0, w_in_0, a_v_norm_0, a_spatial_w_0, a_spatial_b_0, b_group_w_0, b_scale_0,
           w_out_0, norm_1, w_in_1, sink_1, w_out_1, final_norm):
    x1, w_gate, w_out_1b, h1, qt, ka, vt = _layer0(
        x, norm_0, w_in_0, a_v_norm_0, a_spatial_w_0, a_spatial_b_0, b_group_w_0, b_scale_0,
        w_out_0, norm_1, w_in_1, w_out_1)
    return _attention(x1, h1, qt, ka, vt, w_gate, sink_1, w_out_1b, final_norm)
```

```python
import functools
import math

import jax
import jax.numpy as jnp
import numpy as np
from jax import lax
from jax.experimental import pallas as pl
from jax.experimental.pallas import tpu as pltpu

F32 = jnp.float32
BF16 = jnp.bfloat16

D_MODEL = 1024
EPS = 1e-6
NEG_INF = -1e30
LOG2_E = math.log2(math.e)
CHUNK = 128
A_GROUPS = 4
A_WIDTH = D_MODEL
A_GROUP_DIM = A_WIDTH // A_GROUPS
B_WIDTH = D_MODEL
POOL_WINDOWS = (2, 4, 8, 16)
B_GROUP_DIM = B_WIDTH // len(POOL_WINDOWS)
POOL_HALO = max(POOL_WINDOWS) // 2
N_HEADS = 16
N_KV_HEADS = 4
HEAD_DIM = 64
GQA_GROUP = N_HEADS // N_KV_HEADS
C_WIDTH = N_HEADS * HEAD_DIM
KV_WIDTH = N_KV_HEADS * HEAD_DIM
WINDOW = 128
ATTN_BLOCK = 128
BAND = 3 * ATTN_BLOCK
ROPE_THETA = 500000.0
ROT_DIM = HEAD_DIM // 4
LANES = 128
KV_PAD_WIDTH = KV_WIDTH
GROUP_Q = GQA_GROUP * ATTN_BLOCK
HEADS_PER_HALF = GQA_GROUP // 2
HALF_Q = HEADS_PER_HALF * ATTN_BLOCK

QKV_COLS = C_WIDTH + 2 * KV_WIDTH
W1_CHUNK_ROWS = 128
W1_CHUNKS = D_MODEL // W1_CHUNK_ROWS
W0_STAGE_CHUNK = (1024, 512)

TILE_L0 = 512
TILE_ATTN = 512
VMEM_LIMIT = 56 * 1024 * 1024


def _rms(x, g):
    return x * lax.rsqrt(jnp.mean(x * x, axis=-1, keepdims=True) + EPS) * g


def _silu(x):
    hx = 0.5 * x
    return hx + hx * jnp.tanh(hx)


def _dot(a, b):
    return jnp.dot(a, b, preferred_element_type=F32)


def _dot_nt(a, b):
    return lax.dot_general(a, b, (((1,), (1,)), ((), ())), preferred_element_type=F32)


def _layer0_kernel(x_ref, xp_ref, xn_ref, n0_ref, win_hbm, gv_ref, ws_ref, bsb_ref, wg_ref,
                   sc_ref, wout_hbm, w1_ref, wo1_ref, n1_ref, w1_hbm, cost_ref, sint_ref,
                   o_ref, wgate_ref, wo1b_ref, h1_ref, qt_ref, ka_ref, vt_ref,
                   hext_ref, v_ref, cat_ref, wqkvt_ref, win_ref, wout_ref, stage_ref, wsem,
                   *, tile, seq):
    j = pl.program_id(1)
    nj = pl.num_programs(1)
    step = pl.program_id(0) * nj + j

    @pl.when(step == 0)
    def _():
        cr, cc = stage_ref.shape[1:]
        chunks = [(win_hbm, win_ref, r0, c0, False)
                  for r0 in range(0, win_ref.shape[0], cr) for c0 in range(0, win_ref.shape[1], cc)]
        chunks += [(wout_hbm, wout_ref, r0, c0, False)
                   for r0 in range(0, wout_ref.shape[0], cr)
                   for c0 in range(0, wout_ref.shape[1], cc)]
        chunks += [(w1_hbm, wqkvt_ref, r0, c0, True)
                   for r0 in range(0, wqkvt_ref.shape[1], cr)
                   for c0 in range(0, wqkvt_ref.shape[0], cc)]

        def chunk_copy(k):
            src, _, r0, c0, _ = chunks[k]
            return pltpu.make_async_copy(src.at[r0:r0 + cr, c0:c0 + cc], stage_ref.at[k % 2],
                                         wsem.at[k % 2])

        chunk_copy(0).start()
        for k, (_, dst, r0, c0, transposed) in enumerate(chunks):
            if k + 1 < len(chunks):
                chunk_copy(k + 1).start()
            chunk_copy(k).wait()
            if transposed:
                dst[c0:c0 + cc, r0:r0 + cr] = stage_ref[k % 2].T.astype(BF16)
            else:
                dst[r0:r0 + cr, c0:c0 + cc] = stage_ref[k % 2].astype(BF16)

    @pl.when(step < W1_CHUNKS)
    def _():
        wgate_ref[...] = w1_ref[:, QKV_COLS:].astype(BF16)
        wo1b_ref[...] = wo1_ref[...].astype(BF16)

    n0 = n0_ref[...]
    x = x_ref[0]
    h = _rms(x, n0).astype(BF16)
    hext_ref[0:tile, :] = h
    hp = jnp.where(j > 0, _rms(xp_ref[0], n0), 0.0)
    hn = jnp.where(j < nj - 1, _rms(xn_ref[0], n0), 0.0)
    hext_ref[tile:tile + 2 * POOL_HALO, :] = jnp.concatenate([hp, hn], axis=0).astype(BF16)

    def col(k):
        return k * D_MODEL

    a_v = jax.nn.gelu(_dot(h, win_ref[:, col(1):col(2)]))
    v_ref[...] = _rms(a_v, gv_ref[...]).astype(BF16)
    for g in range(A_GROUPS):
        c0 = g * A_GROUP_DIM
        c1 = c0 + A_GROUP_DIM
        u = jax.nn.gelu(_dot(h, win_ref[:, col(0) + c0:col(0) + c1]))
        gate = _silu(_dot(h, win_ref[:, col(2) + c0:col(2) + c1]))
        ws_g = ws_ref[g].astype(BF16)
        for c in range(tile // CHUNK):
            r0 = c * CHUNK
            r1 = r0 + CHUNK
            mixed = _dot(ws_g, v_ref[r0:r1, c0:c1]) + bsb_ref[:, c0:c1]
            cat_ref[r0:r1, c0:c1] = (u[r0:r1] * mixed * gate[r0:r1]).astype(BF16)

    bxe = _dot(hext_ref[...], win_ref[:, col(3):col(4)])
    n_ext = tile + 2 * POOL_HALO
    edge_row = lax.broadcasted_iota(jnp.int32, (POOL_HALO, 1), 0)
    for gi, w in enumerate(POOL_WINDOWS):
        c0 = gi * B_GROUP_DIM
        c1 = c0 + B_GROUP_DIM
        half = w // 2
        e = jnp.concatenate([bxe[tile:tile + POOL_HALO, c0:c1], bxe[0:tile, c0:c1],
                             bxe[tile + POOL_HALO:, c0:c1]], axis=0)
        a = e
        span = 1
        while span < half:
            a = a + pltpu.roll(a, n_ext - span, 0)
            span *= 2
        win_sum = (a + pltpu.roll(a, half, 0))[POOL_HALO:POOL_HALO + tile]

        def edge_mean(slab, row0, half=half):
            t = j * tile + row0 + edge_row
            cnt = (jnp.minimum(t + half, seq) - jnp.maximum(t - half, 0)).astype(F32)
            return slab / cnt

        mean = jnp.concatenate(
            [edge_mean(win_sum[0:POOL_HALO], 0),
             win_sum[POOL_HALO:tile - POOL_HALO] * (1.0 / w),
             edge_mean(win_sum[tile - POOL_HALO:], tile - POOL_HALO)], axis=0)
        p = (mean - bxe[0:tile, c0:c1]).astype(BF16)
        gate = _silu(_dot(h, win_ref[:, col(4) + c0:col(4) + c1]))
        yb = _dot(p, wg_ref[gi].astype(BF16)) * sc_ref[:, c0:c1] * gate
        cat_ref[:, A_WIDTH + c0:A_WIDTH + c1] = yb.astype(BF16)

    n1 = n1_ref[...]
    hrows = tile // 2
    for r0 in range(0, tile, hrows):
        rows = slice(r0, r0 + hrows)
        out = x_ref[0, rows, :] + _dot(cat_ref[rows, :], wout_ref[...])
        o_ref[0, rows, :] = out
        h1_ref[0, rows, :] = _rms(out, n1).astype(BF16)
    _qkv_project(h1_ref[0], wqkvt_ref, cost_ref[...], sint_ref[...], qt_ref, ka_ref, vt_ref,
                 slice(0, tile))


def _layer0(x, norm_0, w_in_0, a_v_norm_0, a_spatial_w_0, a_spatial_b_0, b_group_w_0, b_scale_0,
            w_out_0, norm_1, w_in_1, w_out_1):
    bn, seq, d = x.shape
    tile = TILE_L0
    nt = seq // tile
    assert bn * nt >= W1_CHUNKS

    def w1_chunk(b, j):
        return jnp.minimum(b * nt + j, W1_CHUNKS - 1)
    halo_blocks = seq // POOL_HALO
    ws = a_spatial_w_0
    wg = b_group_w_0
    cos_fm, sin_fm = _rope_tables(seq)
    bsb = jnp.repeat(a_spatial_b_0.T, A_GROUP_DIM, axis=1)

    def const(shape):
        return pl.BlockSpec(shape, lambda b, j: (0,) * len(shape), pipeline_mode=pl.Buffered(1))

    per_halo = tile // POOL_HALO
    kern = functools.partial(_layer0_kernel, tile=tile, seq=seq)
    return pl.pallas_call(
        kern,
        grid=(bn, nt),
        in_specs=[
            pl.BlockSpec((1, tile, d), lambda b, j: (b, j, 0)),
            pl.BlockSpec((1, POOL_HALO, d), lambda b, j: (b, jnp.maximum(j * per_halo - 1, 0), 0)),
            pl.BlockSpec((1, POOL_HALO, d),
                         lambda b, j: (b, jnp.minimum((j + 1) * per_halo, halo_blocks - 1), 0)),
            const((1, d)),
            pl.BlockSpec(memory_space=pl.ANY),
            const((1, A_WIDTH)),
            const(ws.shape),
            const(bsb.shape),
            const(wg.shape),
            const((1, B_WIDTH)),
            pl.BlockSpec(memory_space=pl.ANY),
            pl.BlockSpec((W1_CHUNK_ROWS, w_in_1.shape[1]), lambda b, j: (w1_chunk(b, j), 0)),
            pl.BlockSpec((W1_CHUNK_ROWS, d), lambda b, j: (w1_chunk(b, j), 0)),
            const((1, d)),
            pl.BlockSpec(memory_space=pl.ANY),
            pl.BlockSpec((ROT_DIM // 2, tile), lambda b, j: (0, j)),
            pl.BlockSpec((ROT_DIM // 2, tile), lambda b, j: (0, j)),
        ],
        out_specs=[
            pl.BlockSpec((1, tile, d), lambda b, j: (b, j, 0)),
            pl.BlockSpec((W1_CHUNK_ROWS, C_WIDTH), lambda b, j: (w1_chunk(b, j), 0)),
            pl.BlockSpec((W1_CHUNK_ROWS, d), lambda b, j: (w1_chunk(b, j), 0)),
            pl.BlockSpec((1, tile, d), lambda b, j: (b, j, 0)),
            pl.BlockSpec((1, 1, C_WIDTH, tile), lambda b, j: (b, j, 0, 0)),
            pl.BlockSpec((1, tile, KV_PAD_WIDTH), lambda b, j: (b, j, 0)),
            pl.BlockSpec((1, 1, KV_WIDTH, tile), lambda b, j: (b, j, 0, 0)),
        ],
        out_shape=[
            jax.ShapeDtypeStruct(x.shape, F32),
            jax.ShapeDtypeStruct((d, C_WIDTH), BF16),
            jax.ShapeDtypeStruct((C_WIDTH, d), BF16),
            jax.ShapeDtypeStruct((bn, seq, d), BF16),
            jax.ShapeDtypeStruct((bn, nt, C_WIDTH, tile), BF16),
            jax.ShapeDtypeStruct((bn, seq, KV_PAD_WIDTH), BF16),
            jax.ShapeDtypeStruct((bn, nt, KV_WIDTH, tile), BF16),
        ],
        scratch_shapes=[
            pltpu.VMEM((tile + 2 * POOL_HALO, d), BF16),
            pltpu.VMEM((tile, A_WIDTH), BF16),
            pltpu.VMEM((tile, A_WIDTH + B_WIDTH), BF16),
            pltpu.VMEM((QKV_COLS, d), BF16),
            pltpu.VMEM(w_in_0.shape, BF16),
            pltpu.VMEM(w_out_0.shape, BF16),
            pltpu.VMEM((2,) + W0_STAGE_CHUNK, F32),
            pltpu.SemaphoreType.DMA((2,)),
        ],
        compiler_params=pltpu.CompilerParams(
            dimension_semantics=("arbitrary", "arbitrary"), vmem_limit_bytes=VMEM_LIMIT),
        name="layer0_mixer",
    )(x, x, x, norm_0.reshape(1, d), w_in_0, a_v_norm_0.reshape(1, A_WIDTH), ws, bsb, wg,
      b_scale_0.reshape(1, B_WIDTH), w_out_0, w_in_1, w_out_1, norm_1.reshape(1, d), w_in_1,
      cos_fm, sin_fm)


def _qkv_project(h, w_ref, ct, st, qt_ref, ka_ref, vt_ref, pos):
    half = ROT_DIM // 2
    n = h.shape[0]
    scale = HEAD_DIM ** -0.5 * LOG2_E

    qkv = _dot_nt(w_ref[...], h)

    def rope_rows(r0):
        x1 = qkv[r0:r0 + half]
        x2 = qkv[r0 + half:r0 + ROT_DIM]
        return jnp.concatenate([x1 * ct - x2 * st, x2 * ct + x1 * st], axis=0)

    for hd in range(N_HEADS):
        r0 = hd * HEAD_DIM
        qt_ref[0, 0, r0:r0 + ROT_DIM, pos] = (rope_rows(r0) * scale).astype(BF16)
        qt_ref[0, 0, r0 + ROT_DIM:r0 + HEAD_DIM, pos] = (
            qkv[r0 + ROT_DIM:r0 + HEAD_DIM] * scale).astype(BF16)
    vt_ref[0, 0, :, pos] = qkv[C_WIDTH + KV_WIDTH:].astype(BF16)

    for kp in range(N_KV_HEADS // 2):
        rows = []
        for kh in (2 * kp, 2 * kp + 1):
            r0 = C_WIDTH + kh * HEAD_DIM
            rows += [rope_rows(r0), qkv[r0 + ROT_DIM:r0 + HEAD_DIM]]
        kt = jnp.concatenate(rows, axis=0)
        ka_ref[0, pos, kp * LANES:(kp + 1) * LANES] = kt.T.astype(BF16)


def _rope_tables(seq):
    inv = ROPE_THETA ** (-np.arange(0, ROT_DIM, 2, dtype=np.float64) / ROT_DIM)
    ang = np.arange(seq, dtype=np.float64)[None, :] * inv[:, None]
    return jnp.asarray(np.cos(ang), F32), jnp.asarray(np.sin(ang), F32)


def _attn_kernel(sink_ref, x_ref, h_ref, qt_ref, kap_ref, kat_ref, kan_ref, vtp_ref, vtt_ref,
                 vtn_ref, wgate_ref, wout_ref, fn_ref, o_ref,
                 s_ref, p_ref, sg_ref, y_ref, *, tile, seq):
    j = pl.program_id(1)
    nblk = tile // ATTN_BLOCK
    last_blk = seq // ATTN_BLOCK - 1

    def k_band(i, kh):
        cols = slice(kh // 2 * LANES, (kh // 2 + 1) * LANES)
        lo = max(i - 1, 0) * ATTN_BLOCK
        hi = min(i + 2, nblk) * ATTN_BLOCK
        parts = [kat_ref[0, lo:hi, cols]]
        if i == 0:
            parts.insert(0, kap_ref[0, :, cols])
        if i == nblk - 1:
            parts.append(kan_ref[0, :, cols])
        return jnp.concatenate(parts, axis=0)

    def vt_band(i, kh):
        rows = slice(kh * HEAD_DIM, (kh + 1) * HEAD_DIM)
        lo = max(i - 1, 0) * ATTN_BLOCK
        hi = min(i + 2, nblk) * ATTN_BLOCK
        parts = [vtt_ref[0, 0, rows, lo:hi]]
        if i == 0:
            parts.insert(0, vtp_ref[0, 0, rows, :])
        if i == nblk - 1:
            parts.append(vtn_ref[0, 0, rows, :])
        return parts
    diff = (lax.broadcasted_iota(jnp.int32, (ATTN_BLOCK, ATTN_BLOCK), 0)
            - lax.broadcasted_iota(jnp.int32, (ATTN_BLOCK, ATTN_BLOCK), 1))
    never = 2 * ATTN_BLOCK
    ones_rows = jnp.ones((HEAD_DIM // 2, BAND), BF16)

    def scores(u, half):
        i, kh = divmod(u, N_KV_HEADS)
        slot = u % 2
        c0 = i * ATTN_BLOCK
        h0 = kh * GQA_GROUP + half * HEADS_PER_HALF
        qg = jnp.concatenate(
            [qt_ref[0, 0, (h0 + g) * HEAD_DIM:(h0 + g + 1) * HEAD_DIM, c0:c0 + ATTN_BLOCK]
             for g in range(HEADS_PER_HALF)], axis=1)
        zq = jnp.zeros((LANES - HEAD_DIM, HALF_Q), BF16)
        rq = jnp.concatenate([qg, zq] if kh % 2 == 0 else [zq, qg], axis=0)
        ka = k_band(i, kh)
        s_ref[slot, :, half * HALF_Q:(half + 1) * HALF_Q] = _dot(ka, rq)

    def softmax_head(u, g):
        i, kh = divmod(u, N_KV_HEADS)
        slot = u % 2
        gblk = j * nblk + i
        ok_prev = diff >= jnp.where(gblk > 0, 0, never)
        ok_next = diff <= jnp.where(gblk < last_blk, 0, -never)
        cs = slice(g * ATTN_BLOCK, (g + 1) * ATTN_BLOCK)
        sink = sink_ref[kh * GQA_GROUP + g] * LOG2_E
        s0 = jnp.where(ok_prev, s_ref[slot, 0:ATTN_BLOCK, cs], NEG_INF)
        s1 = s_ref[slot, ATTN_BLOCK:2 * ATTN_BLOCK, cs]
        s2 = jnp.where(ok_next, s_ref[slot, 2 * ATTN_BLOCK:, cs], NEG_INF)
        m = jnp.max(jnp.maximum(jnp.maximum(s0, s1), s2), axis=0, keepdims=True)
        m = jnp.maximum(m, sink)
        p0 = jnp.exp2(s0 - m)
        p1 = jnp.exp2(s1 - m)
        p2 = jnp.exp2(s2 - m)
        p_ref[slot, 0:ATTN_BLOCK, cs] = p0.astype(BF16)
        p_ref[slot, ATTN_BLOCK:2 * ATTN_BLOCK, cs] = p1.astype(BF16)
        p_ref[slot, 2 * ATTN_BLOCK:, cs] = p2.astype(BF16)
        return jnp.exp2(sink - m)

    def weighted(u, half, sink_share):
        i, kh = divmod(u, N_KV_HEADS)
        slot = u % 2
        c0 = i * ATTN_BLOCK
        vt = jnp.concatenate([jnp.concatenate(vt_band(i, kh), axis=1), ones_rows],
                             axis=0)
        ot = _dot(vt, p_ref[slot, :, half * HALF_Q:(half + 1) * HALF_Q])
        parts = []
        for g in range(HEADS_PER_HALF):
            cq = slice(g * ATTN_BLOCK, (g + 1) * ATTN_BLOCK)
            l = ot[HEAD_DIM:HEAD_DIM + 1, cq] + sink_share[g]
            parts.append(ot[0:HEAD_DIM, cq] * (1.0 / l))
        yt = jnp.concatenate(parts, axis=0)
        f0 = (kh * GQA_GROUP + half * HEADS_PER_HALF) * HEAD_DIM
        cs = slice(f0, f0 + HEADS_PER_HALF * HEAD_DIM)
        y = yt.T * sg_ref[c0:c0 + ATTN_BLOCK, cs]
        y_ref[c0:c0 + ATTN_BLOCK, cs] = y.astype(BF16)

    def gate_chunk(kh):
        cs = slice(kh * GQA_GROUP * HEAD_DIM, (kh + 1) * GQA_GROUP * HEAD_DIM)
        sg_ref[:, cs] = _silu(_dot(h_ref[0], wgate_ref[:, cs]))

    def finish_rows(r0, n):
        rows = slice(r0, r0 + n)
        out = x_ref[0, rows, :] + _dot(y_ref[rows, :], wout_ref[...])
        o_ref[0, rows, :] = _rms(out, fn_ref[...])

    units = nblk * N_KV_HEADS
    inv = {}
    for step in range(units + 2):
        sc = step if step < units else None
        sm = step - 1 if 1 <= step <= units else None
        wt = step - 2 if step >= 2 else None
        if step < N_KV_HEADS:
            gate_chunk(step)
        for half in range(2):
            if sc is not None:
                scores(sc, half)
            if sm is not None:
                inv[sm, 2 * half] = softmax_head(sm, 2 * half)
            if wt is not None:
                weighted(wt, half, [inv.pop((wt, half * HEADS_PER_HALF + g))
                                    for g in range(HEADS_PER_HALF)])
            if sm is not None:
                inv[sm, 2 * half + 1] = softmax_head(sm, 2 * half + 1)
        if wt is not None and (wt + 1) % (units // 2) == 0:
            finish_rows((wt + 1 - units // 2) // N_KV_HEADS * ATTN_BLOCK, tile // 2)


def _attention(x, h, qt, ka, vt, w_gate, sink_1, w_out_1, final_norm):
    bn, seq, d = x.shape
    tile = TILE_ATTN
    assert tile == TILE_L0
    nt = seq // tile
    per = tile // ATTN_BLOCK
    nb = seq // ATTN_BLOCK

    def const(shape):
        return pl.BlockSpec(shape, lambda b, j: (0,) * len(shape), pipeline_mode=pl.Buffered(1))

    def prev_blk(j):
        return jnp.maximum(j * per - 1, 0)

    def next_blk(j):
        return jnp.minimum((j + 1) * per, nb - 1)

    kern = functools.partial(_attn_kernel, tile=tile, seq=seq)
    return pl.pallas_call(
        kern,
        grid=(bn, nt),
        in_specs=[
            pl.BlockSpec(memory_space=pltpu.SMEM),
            pl.BlockSpec((1, tile, d), lambda b, j: (b, j, 0)),
            pl.BlockSpec((1, tile, d), lambda b, j: (b, j, 0)),
            pl.BlockSpec((1, 1, C_WIDTH, tile), lambda b, j: (b, j, 0, 0)),
            pl.BlockSpec((1, ATTN_BLOCK, KV_PAD_WIDTH), lambda b, j: (b, prev_blk(j), 0)),
            pl.BlockSpec((1, tile, KV_PAD_WIDTH), lambda b, j: (b, j, 0)),
            pl.BlockSpec((1, ATTN_BLOCK, KV_PAD_WIDTH), lambda b, j: (b, next_blk(j), 0)),
            pl.BlockSpec((1, 1, KV_WIDTH, ATTN_BLOCK),
                         lambda b, j: (b, jnp.maximum(j - 1, 0), 0, per - 1)),
            pl.BlockSpec((1, 1, KV_WIDTH, tile), lambda b, j: (b, j, 0, 0)),
            pl.BlockSpec((1, 1, KV_WIDTH, ATTN_BLOCK),
                         lambda b, j: (b, jnp.minimum(j + 1, nt - 1), 0, 0)),
            const(w_gate.shape),
            const(w_out_1.shape),
            const((1, d)),
        ],
        out_specs=pl.BlockSpec((1, tile, d), lambda b, j: (b, j, 0)),
        out_shape=jax.ShapeDtypeStruct(x.shape, F32),
        scratch_shapes=[
            pltpu.VMEM((2, BAND, GROUP_Q), F32),
            pltpu.VMEM((2, BAND, GROUP_Q), BF16),
            pltpu.VMEM((tile, C_WIDTH), F32),
            pltpu.VMEM((tile, C_WIDTH), BF16),
        ],
        compiler_params=pltpu.CompilerParams(
            dimension_semantics=("arbitrary", "arbitrary"), vmem_limit_bytes=VMEM_LIMIT),
        name="layer1_attention",
    )(sink_1, x, h, qt, ka, ka, ka, vt, vt, vt, w_gate, w_out_1, final_norm.reshape(1, d))


def kernel(x, norm_0, w_in_0, a_v_norm_0, a_spatial_w_0, a_spatial_b_0, b_group_w_0, b_scale_0,
           w_out_0, norm_1, w_in_1, sink_1, w_out_1, final_norm):
    x1, w_gate, w_out_1b, h1, qt, ka, vt = _layer0(
        x, norm_0, w_in_0, a_v_norm_0, a_spatial_w_0, a_spatial_b_0, b_group_w_0, b_scale_0,
        w_out_0, norm_1, w_in_1, w_out_1)
    return _attention(x1, h1, qt, ka, vt, w_gate, sink_1, w_out_1b, final_norm)
```

```python
import functools
import math

import jax
import jax.numpy as jnp
import numpy as np
from jax import lax
from jax.experimental import pallas as pl
from jax.experimental.pallas import tpu as pltpu

F32 = jnp.float32
BF16 = jnp.bfloat16

D_MODEL = 1024
EPS = 1e-6
NEG_INF = -1e30
LOG2_E = math.log2(math.e)
CHUNK = 128
A_GROUPS = 4
A_WIDTH = D_MODEL
A_GROUP_DIM = A_WIDTH // A_GROUPS
B_WIDTH = D_MODEL
POOL_WINDOWS = (2, 4, 8, 16)
B_GROUP_DIM = B_WIDTH // len(POOL_WINDOWS)
POOL_HALO = max(POOL_WINDOWS) // 2
N_HEADS = 16
N_KV_HEADS = 4
HEAD_DIM = 64
GQA_GROUP = N_HEADS // N_KV_HEADS
C_WIDTH = N_HEADS * HEAD_DIM
KV_WIDTH = N_KV_HEADS * HEAD_DIM
WINDOW = 128
ATTN_BLOCK = 128
BAND = 3 * ATTN_BLOCK
ROPE_THETA = 500000.0
ROT_DIM = HEAD_DIM // 4
LANES = 128
assert WINDOW == ATTN_BLOCK
GROUP_Q = GQA_GROUP * ATTN_BLOCK
HEADS_PER_HALF = GQA_GROUP // 2
HALF_Q = HEADS_PER_HALF * ATTN_BLOCK

QKV_COLS = C_WIDTH + 2 * KV_WIDTH
W1_CHUNK_ROWS = 128
W1_CHUNKS = D_MODEL // W1_CHUNK_ROWS
W0_STAGE_CHUNK = (1024, 512)

TILE_L0 = 512
TILE_ATTN = 512
VMEM_LIMIT = 56 * 1024 * 1024


def _rms(x, g):
    return x * lax.rsqrt(jnp.mean(x * x, axis=-1, keepdims=True) + EPS) * g


def _silu(x):
    hx = 0.5 * x
    return hx + hx * jnp.tanh(hx)


def _dot(a, b):
    return jnp.dot(a, b, preferred_element_type=F32)


def _dot_nt(a, b):
    return lax.dot_general(a, b, (((1,), (1,)), ((), ())), preferred_element_type=F32)


def _layer0_kernel(x_ref, xp_ref, xn_ref, n0_ref, win_hbm, gv_ref, ws_ref, bsb_ref, wg_ref,
                   sc_ref, wout_hbm, w1_ref, wo1_ref, n1_ref, w1_hbm, cost_ref, sint_ref,
                   o_ref, wgate_ref, wo1b_ref, h1_ref, qt_ref, ka_ref, vt_ref,
                   hext_ref, v_ref, cat_ref, wqkvt_ref, win_ref, wout_ref, stage_ref, wsem,
                   *, tile, seq):
    j = pl.program_id(1)
    nj = pl.num_programs(1)
    step = pl.program_id(0) * nj + j

    @pl.when(step == 0)
    def _():
        cr, cc = stage_ref.shape[1:]
        chunks = [(win_hbm, win_ref, r0, c0, False)
                  for r0 in range(0, win_ref.shape[0], cr) for c0 in range(0, win_ref.shape[1], cc)]
        chunks += [(wout_hbm, wout_ref, r0, c0, False)
                   for r0 in range(0, wout_ref.shape[0], cr)
                   for c0 in range(0, wout_ref.shape[1], cc)]
        chunks += [(w1_hbm, wqkvt_ref, r0, c0, True)
                   for r0 in range(0, wqkvt_ref.shape[1], cr)
                   for c0 in range(0, wqkvt_ref.shape[0], cc)]

        def chunk_copy(k):
            src, _, r0, c0, _ = chunks[k]
            return pltpu.make_async_copy(src.at[r0:r0 + cr, c0:c0 + cc], stage_ref.at[k % 2],
                                         wsem.at[k % 2])

        chunk_copy(0).start()
        for k, (_, dst, r0, c0, transposed) in enumerate(chunks):
            if k + 1 < len(chunks):
                chunk_copy(k + 1).start()
            chunk_copy(k).wait()
            if transposed:
                dst[c0:c0 + cc, r0:r0 + cr] = stage_ref[k % 2].T.astype(BF16)
            else:
                dst[r0:r0 + cr, c0:c0 + cc] = stage_ref[k % 2].astype(BF16)

    @pl.when(step < W1_CHUNKS)
    def _():
        wgate_ref[...] = w1_ref[:, QKV_COLS:].astype(BF16)
        wo1b_ref[...] = wo1_ref[...].astype(BF16)

    n0 = n0_ref[...]
    x = x_ref[0]
    h = _rms(x, n0).astype(BF16)
    hext_ref[0:tile, :] = h
    hp = jnp.where(j > 0, _rms(xp_ref[0], n0), 0.0)
    hn = jnp.where(j < nj - 1, _rms(xn_ref[0], n0), 0.0)
    hext_ref[tile:tile + 2 * POOL_HALO, :] = jnp.concatenate([hp, hn], axis=0).astype(BF16)

    def col(k):
        return k * D_MODEL

    a_v = jax.nn.gelu(_dot(h, win_ref[:, col(1):col(2)]))
    v_ref[...] = _rms(a_v, gv_ref[...]).astype(BF16)
    for g in range(A_GROUPS):
        c0 = g * A_GROUP_DIM
        c1 = c0 + A_GROUP_DIM
        u = jax.nn.gelu(_dot(h, win_ref[:, col(0) + c0:col(0) + c1]))
        gate = _silu(_dot(h, win_ref[:, col(2) + c0:col(2) + c1]))
        ws_g = ws_ref[g].astype(BF16)
        for c in range(tile // CHUNK):
            r0 = c * CHUNK
            r1 = r0 + CHUNK
            mixed = _dot(ws_g, v_ref[r0:r1, c0:c1]) + bsb_ref[:, c0:c1]
            cat_ref[r0:r1, c0:c1] = (u[r0:r1] * mixed * gate[r0:r1]).astype(BF16)

    bxe = _dot(hext_ref[...], win_ref[:, col(3):col(4)])
    n_ext = tile + 2 * POOL_HALO
    edge_row = lax.broadcasted_iota(jnp.int32, (POOL_HALO, 1), 0)
    for gi, w in enumerate(POOL_WINDOWS):
        c0 = gi * B_GROUP_DIM
        c1 = c0 + B_GROUP_DIM
        half = w // 2
        e = jnp.concatenate([bxe[tile:tile + POOL_HALO, c0:c1], bxe[0:tile, c0:c1],
                             bxe[tile + POOL_HALO:, c0:c1]], axis=0)
        a = e
        span = 1
        while span < half:
            a = a + pltpu.roll(a, n_ext - span, 0)
            span *= 2
        win_sum = (a + pltpu.roll(a, half, 0))[POOL_HALO:POOL_HALO + tile]

        def edge_mean(slab, row0, half=half):
            t = j * tile + row0 + edge_row
            cnt = (jnp.minimum(t + half, seq) - jnp.maximum(t - half, 0)).astype(F32)
            return slab / cnt

        mean = jnp.concatenate(
            [edge_mean(win_sum[0:POOL_HALO], 0),
             win_sum[POOL_HALO:tile - POOL_HALO] * (1.0 / w),
             edge_mean(win_sum[tile - POOL_HALO:], tile - POOL_HALO)], axis=0)
        p = (mean - bxe[0:tile, c0:c1]).astype(BF16)
        gate = _silu(_dot(h, win_ref[:, col(4) + c0:col(4) + c1]))
        yb = _dot(p, wg_ref[gi].astype(BF16)) * sc_ref[:, c0:c1] * gate
        cat_ref[:, A_WIDTH + c0:A_WIDTH + c1] = yb.astype(BF16)

    n1 = n1_ref[...]
    hrows = tile // 2
    for r0 in range(0, tile, hrows):
        rows = slice(r0, r0 + hrows)
        out = x_ref[0, rows, :] + _dot(cat_ref[rows, :], wout_ref[...])
        o_ref[0, rows, :] = out
        h1_ref[0, rows, :] = _rms(out, n1).astype(BF16)
    _qkv_project(h1_ref[0], wqkvt_ref, cost_ref[...], sint_ref[...], qt_ref, ka_ref, vt_ref,
                 slice(0, tile))


def _layer0(x, norm_0, w_in_0, a_v_norm_0, a_spatial_w_0, a_spatial_b_0, b_group_w_0, b_scale_0,
            w_out_0, norm_1, w_in_1, w_out_1):
    bn, seq, d = x.shape
    tile = TILE_L0
    nt = seq // tile
    assert bn * nt >= W1_CHUNKS

    def w1_chunk(b, j):
        return jnp.minimum(b * nt + j, W1_CHUNKS - 1)
    halo_blocks = seq // POOL_HALO
    ws = a_spatial_w_0
    wg = b_group_w_0
    cos_fm, sin_fm = _rope_tables(seq)
    bsb = jnp.repeat(a_spatial_b_0.T, A_GROUP_DIM, axis=1)

    def const(shape):
        return pl.BlockSpec(shape, lambda b, j: (0,) * len(shape), pipeline_mode=pl.Buffered(1))

    per_halo = tile // POOL_HALO
    kern = functools.partial(_layer0_kernel, tile=tile, seq=seq)
    return pl.pallas_call(
        kern,
        grid=(bn, nt),
        in_specs=[
            pl.BlockSpec((1, tile, d), lambda b, j: (b, j, 0)),
            pl.BlockSpec((1, POOL_HALO, d), lambda b, j: (b, jnp.maximum(j * per_halo - 1, 0), 0)),
            pl.BlockSpec((1, POOL_HALO, d),
                         lambda b, j: (b, jnp.minimum((j + 1) * per_halo, halo_blocks - 1), 0)),
            const((1, d)),
            pl.BlockSpec(memory_space=pl.ANY),
            const((1, A_WIDTH)),
            const(ws.shape),
            const(bsb.shape),
            const(wg.shape),
            const((1, B_WIDTH)),
            pl.BlockSpec(memory_space=pl.ANY),
            pl.BlockSpec((W1_CHUNK_ROWS, w_in_1.shape[1]), lambda b, j: (w1_chunk(b, j), 0)),
            pl.BlockSpec((W1_CHUNK_ROWS, d), lambda b, j: (w1_chunk(b, j), 0)),
            const((1, d)),
            pl.BlockSpec(memory_space=pl.ANY),
            pl.BlockSpec((ROT_DIM // 2, tile), lambda b, j: (0, j)),
            pl.BlockSpec((ROT_DIM // 2, tile), lambda b, j: (0, j)),
        ],
        out_specs=[
            pl.BlockSpec((1, tile, d), lambda b, j: (b, j, 0)),
            pl.BlockSpec((W1_CHUNK_ROWS, C_WIDTH), lambda b, j: (w1_chunk(b, j), 0)),
            pl.BlockSpec((W1_CHUNK_ROWS, d), lambda b, j: (w1_chunk(b, j), 0)),
            pl.BlockSpec((1, tile, d), lambda b, j: (b, j, 0)),
            pl.BlockSpec((1, 1, C_WIDTH, tile), lambda b, j: (b, j, 0, 0)),
            pl.BlockSpec((1, tile, KV_WIDTH), lambda b, j: (b, j, 0)),
            pl.BlockSpec((1, 1, KV_WIDTH, tile), lambda b, j: (b, j, 0, 0)),
        ],
        out_shape=[
            jax.ShapeDtypeStruct(x.shape, F32),
            jax.ShapeDtypeStruct((d, C_WIDTH), BF16),
            jax.ShapeDtypeStruct((C_WIDTH, d), BF16),
            jax.ShapeDtypeStruct((bn, seq, d), BF16),
            jax.ShapeDtypeStruct((bn, nt, C_WIDTH, tile), BF16),
            jax.ShapeDtypeStruct((bn, seq, KV_WIDTH), BF16),
            jax.ShapeDtypeStruct((bn, nt, KV_WIDTH, tile), BF16),
        ],
        scratch_shapes=[
            pltpu.VMEM((tile + 2 * POOL_HALO, d), BF16),
            pltpu.VMEM((tile, A_WIDTH), BF16),
            pltpu.VMEM((tile, A_WIDTH + B_WIDTH), BF16),
            pltpu.VMEM((QKV_COLS, d), BF16),
            pltpu.VMEM(w_in_0.shape, BF16),
            pltpu.VMEM(w_out_0.shape, BF16),
            pltpu.VMEM((2,) + W0_STAGE_CHUNK, F32),
            pltpu.SemaphoreType.DMA((2,)),
        ],
        compiler_params=pltpu.CompilerParams(
            dimension_semantics=("arbitrary", "arbitrary"), vmem_limit_bytes=VMEM_LIMIT),
        name="layer0_mixer",
    )(x, x, x, norm_0.reshape(1, d), w_in_0, a_v_norm_0.reshape(1, A_WIDTH), ws, bsb, wg,
      b_scale_0.reshape(1, B_WIDTH), w_out_0, w_in_1, w_out_1, norm_1.reshape(1, d), w_in_1,
      cos_fm, sin_fm)


def _qkv_project(h, w_ref, ct, st, qt_ref, ka_ref, vt_ref, pos):
    half = ROT_DIM // 2
    n = h.shape[0]
    scale = HEAD_DIM ** -0.5 * LOG2_E

    qkv = _dot_nt(w_ref[...], h)

    def rope_rows(r0):
        x1 = qkv[r0:r0 + half]
        x2 = qkv[r0 + half:r0 + ROT_DIM]
        return jnp.concatenate([x1 * ct - x2 * st, x2 * ct + x1 * st], axis=0)

    for hd in range(N_HEADS):
        r0 = hd * HEAD_DIM
        qt_ref[0, 0, r0:r0 + ROT_DIM, pos] = (rope_rows(r0) * scale).astype(BF16)
        qt_ref[0, 0, r0 + ROT_DIM:r0 + HEAD_DIM, pos] = (
            qkv[r0 + ROT_DIM:r0 + HEAD_DIM] * scale).astype(BF16)
    vt_ref[0, 0, :, pos] = qkv[C_WIDTH + KV_WIDTH:].astype(BF16)

    for kp in range(N_KV_HEADS // 2):
        rows = []
        for kh in (2 * kp, 2 * kp + 1):
            r0 = C_WIDTH + kh * HEAD_DIM
            rows += [rope_rows(r0), qkv[r0 + ROT_DIM:r0 + HEAD_DIM]]
        kt = jnp.concatenate(rows, axis=0)
        ka_ref[0, pos, kp * LANES:(kp + 1) * LANES] = kt.T.astype(BF16)


def _rope_tables(seq):
    inv = ROPE_THETA ** (-np.arange(0, ROT_DIM, 2, dtype=np.float64) / ROT_DIM)
    ang = np.arange(seq, dtype=np.float64)[None, :] * inv[:, None]
    return jnp.asarray(np.cos(ang), F32), jnp.asarray(np.sin(ang), F32)


def _attn_kernel(sink_ref, x_ref, h_ref, qt_ref, kap_ref, kat_ref, kan_ref, vtp_ref, vtt_ref,
                 vtn_ref, wgate_ref, wout_ref, fn_ref, o_ref,
                 s_ref, p_ref, sg_ref, y_ref, *, tile, seq):
    j = pl.program_id(1)
    nblk = tile // ATTN_BLOCK
    last_blk = seq // ATTN_BLOCK - 1

    def k_band(i, kh):
        cols = slice(kh // 2 * LANES, (kh // 2 + 1) * LANES)
        lo = max(i - 1, 0) * ATTN_BLOCK
        hi = min(i + 2, nblk) * ATTN_BLOCK
        parts = [kat_ref[0, lo:hi, cols]]
        if i == 0:
            parts.insert(0, kap_ref[0, :, cols])
        if i == nblk - 1:
            parts.append(kan_ref[0, :, cols])
        return jnp.concatenate(parts, axis=0)

    def vt_band(i, kh):
        rows = slice(kh * HEAD_DIM, (kh + 1) * HEAD_DIM)
        lo = max(i - 1, 0) * ATTN_BLOCK
        hi = min(i + 2, nblk) * ATTN_BLOCK
        parts = [vtt_ref[0, 0, rows, lo:hi]]
        if i == 0:
            parts.insert(0, vtp_ref[0, 0, rows, :])
        if i == nblk - 1:
            parts.append(vtn_ref[0, 0, rows, :])
        return parts
    diff = (lax.broadcasted_iota(jnp.int32, (ATTN_BLOCK, ATTN_BLOCK), 0)
            - lax.broadcasted_iota(jnp.int32, (ATTN_BLOCK, ATTN_BLOCK), 1))
    never = 2 * ATTN_BLOCK
    ones_rows = jnp.ones((HEAD_DIM // 2, BAND), BF16)

    def scores(u, half):
        i, kh = divmod(u, N_KV_HEADS)
        slot = u % 2
        c0 = i * ATTN_BLOCK
        h0 = kh * GQA_GROUP + half * HEADS_PER_HALF
        qg = jnp.concatenate(
            [qt_ref[0, 0, (h0 + g) * HEAD_DIM:(h0 + g + 1) * HEAD_DIM, c0:c0 + ATTN_BLOCK]
             for g in range(HEADS_PER_HALF)], axis=1)
        zq = jnp.zeros((LANES - HEAD_DIM, HALF_Q), BF16)
        rq = jnp.concatenate([qg, zq] if kh % 2 == 0 else [zq, qg], axis=0)
        ka = k_band(i, kh)
        s_ref[slot, :, half * HALF_Q:(half + 1) * HALF_Q] = _dot(ka, rq)

    def softmax_head(u, g):
        i, kh = divmod(u, N_KV_HEADS)
        slot = u % 2
        gblk = j * nblk + i
        ok_prev = diff >= jnp.where(gblk > 0, 0, never)
        ok_next = diff <= jnp.where(gblk < last_blk, 0, -never)
        cs = slice(g * ATTN_BLOCK, (g + 1) * ATTN_BLOCK)
        sink = sink_ref[kh * GQA_GROUP + g] * LOG2_E
        s0 = jnp.where(ok_prev, s_ref[slot, 0:ATTN_BLOCK, cs], NEG_INF)
        s1 = s_ref[slot, ATTN_BLOCK:2 * ATTN_BLOCK, cs]
        s2 = jnp.where(ok_next, s_ref[slot, 2 * ATTN_BLOCK:, cs], NEG_INF)
        m = jnp.max(jnp.maximum(jnp.maximum(s0, s1), s2), axis=0, keepdims=True)
        m = jnp.maximum(m, sink)
        p0 = jnp.exp2(s0 - m)
        p1 = jnp.exp2(s1 - m)
        p2 = jnp.exp2(s2 - m)
        p_ref[slot, 0:ATTN_BLOCK, cs] = p0.astype(BF16)
        p_ref[slot, ATTN_BLOCK:2 * ATTN_BLOCK, cs] = p1.astype(BF16)
        p_ref[slot, 2 * ATTN_BLOCK:, cs] = p2.astype(BF16)
        return jnp.exp2(sink - m)

    def weighted(u, half, sink_share):
        i, kh = divmod(u, N_KV_HEADS)
        slot = u % 2
        c0 = i * ATTN_BLOCK
        vt = jnp.concatenate([jnp.concatenate(vt_band(i, kh), axis=1), ones_rows],
                             axis=0)
        ot = _dot(vt, p_ref[slot, :, half * HALF_Q:(half + 1) * HALF_Q])
        parts = []
        for g in range(HEADS_PER_HALF):
            cq = slice(g * ATTN_BLOCK, (g + 1) * ATTN_BLOCK)
            l = ot[HEAD_DIM:HEAD_DIM + 1, cq] + sink_share[g]
            parts.append(ot[0:HEAD_DIM, cq] * (1.0 / l))
        yt = jnp.concatenate(parts, axis=0)
        f0 = (kh * GQA_GROUP + half * HEADS_PER_HALF) * HEAD_DIM
        cs = slice(f0, f0 + HEADS_PER_HALF * HEAD_DIM)
        y = yt.T * sg_ref[c0:c0 + ATTN_BLOCK, cs]
        y_ref[c0:c0 + ATTN_BLOCK, cs] = y.astype(BF16)

    def gate_chunk(kh):
        cs = slice(kh * GQA_GROUP * HEAD_DIM, (kh + 1) * GQA_GROUP * HEAD_DIM)
        sg_ref[:, cs] = _silu(_dot(h_ref[0], wgate_ref[:, cs]))

    def finish_rows(r0, n):
        rows = slice(r0, r0 + n)
        out = x_ref[0, rows, :] + _dot(y_ref[rows, :], wout_ref[...])
        o_ref[0, rows, :] = _rms(out, fn_ref[...])

    units = nblk * N_KV_HEADS
    inv = {}
    for step in range(units + 2):
        sc = step if step < units else None
        sm = step - 1 if 1 <= step <= units else None
        wt = step - 2 if step >= 2 else None
        if step < N_KV_HEADS:
            gate_chunk(step)
        for half in range(2):
            if sc is not None:
                scores(sc, half)
            if sm is not None:
                inv[sm, 2 * half] = softmax_head(sm, 2 * half)
            if wt is not None:
                weighted(wt, half, [inv.pop((wt, half * HEADS_PER_HALF + g))
                                    for g in range(HEADS_PER_HALF)])
            if sm is not None:
                inv[sm, 2 * half + 1] = softmax_head(sm, 2 * half + 1)
        if wt is not None and (wt + 1) % (units // 2) == 0:
            finish_rows((wt + 1 - units // 2) // N_KV_HEADS * ATTN_BLOCK, tile // 2)


def _attention(x, h, qt, ka, vt, w_gate, sink_1, w_out_1, final_norm):
    bn, seq, d = x.shape
    tile = TILE_ATTN
    assert tile == TILE_L0
    nt = seq // tile
    per = tile // ATTN_BLOCK
    nb = seq // ATTN_BLOCK

    def const(shape):
        return pl.BlockSpec(shape, lambda b, j: (0,) * len(shape), pipeline_mode=pl.Buffered(1))

    def prev_blk(j):
        return jnp.maximum(j * per - 1, 0)

    def next_blk(j):
        return jnp.minimum((j + 1) * per, nb - 1)

    kern = functools.partial(_attn_kernel, tile=tile, seq=seq)
    return pl.pallas_call(
        kern,
        grid=(bn, nt),
        in_specs=[
            pl.BlockSpec(memory_space=pltpu.SMEM),
            pl.BlockSpec((1, tile, d), lambda b, j: (b, j, 0)),
            pl.BlockSpec((1, tile, d), lambda b, j: (b, j, 0)),
            pl.BlockSpec((1, 1, C_WIDTH, tile), lambda b, j: (b, j, 0, 0)),
            pl.BlockSpec((1, ATTN_BLOCK, KV_WIDTH), lambda b, j: (b, prev_blk(j), 0)),
            pl.BlockSpec((1, tile, KV_WIDTH), lambda b, j: (b, j, 0)),
            pl.BlockSpec((1, ATTN_BLOCK, KV_WIDTH), lambda b, j: (b, next_blk(j), 0)),
            pl.BlockSpec((1, 1, KV_WIDTH, ATTN_BLOCK),
                         lambda b, j: (b, jnp.maximum(j - 1, 0), 0, per - 1)),
            pl.BlockSpec((1, 1, KV_WIDTH, tile), lambda b, j: (b, j, 0, 0)),
            pl.BlockSpec((1, 1, KV_WIDTH, ATTN_BLOCK),
                         lambda b, j: (b, jnp.minimum(j + 1, nt - 1), 0, 0)),
            const(w_gate.shape),
            const(w_out_1.shape),
            const((1, d)),
        ],
        out_specs=pl.BlockSpec((1, tile, d), lambda b, j: (b, j, 0)),
        out_shape=jax.ShapeDtypeStruct(x.shape, F32),
        scratch_shapes=[
            pltpu.VMEM((2, BAND, GROUP_Q), F32),
            pltpu.VMEM((2, BAND, GROUP_Q), BF16),
            pltpu.VMEM((tile, C_WIDTH), F32),
            pltpu.VMEM((tile, C_WIDTH), BF16),
        ],
        compiler_params=pltpu.CompilerParams(
            dimension_semantics=("arbitrary", "arbitrary"), vmem_limit_bytes=VMEM_LIMIT),
        name="layer1_attention",
    )(sink_1, x, h, qt, ka, ka, ka, vt, vt, vt, w_gate, w_out_1, final_norm.reshape(1, d))


def kernel(x, norm_0, w_in_0, a_v_norm_0, a_spatial_w_0, a_spatial_b_0, b_group_w_0, b_scale_0,
           w_out_0, norm_1, w_in_1, sink_1, w_out_1, final_norm):
    x1, w_gate, w_out_1b, h1, qt, ka, vt = _layer0(
        x, norm_0, w_in_0, a_v_norm_0, a_spatial_w_0, a_spatial_b_0, b_group_w_0, b_scale_0,
        w_out_0, norm_1, w_in_1, w_out_1)
    return _attention(x1, h1, qt, ka, vt, w_gate, sink_1, w_out_1b, final_norm)
```

```python
import functools
import math

import jax
import jax.numpy as jnp
import numpy as np
from jax import lax
from jax.experimental import pallas as pl
from jax.experimental.pallas import tpu as pltpu

F32 = jnp.float32
BF16 = jnp.bfloat16

D_MODEL = 1024
EPS = 1e-6
NEG_INF = -1e30
LOG2_E = math.log2(math.e)
CHUNK = 128
A_GROUPS = 4
A_WIDTH = D_MODEL
A_GROUP_DIM = A_WIDTH // A_GROUPS
B_WIDTH = D_MODEL
POOL_WINDOWS = (2, 4, 8, 16)
B_GROUP_DIM = B_WIDTH // len(POOL_WINDOWS)
POOL_HALO = max(POOL_WINDOWS) // 2
N_HEADS = 16
N_KV_HEADS = 4
HEAD_DIM = 64
GQA_GROUP = N_HEADS // N_KV_HEADS
C_WIDTH = N_HEADS * HEAD_DIM
KV_WIDTH = N_KV_HEADS * HEAD_DIM
WINDOW = 128
ATTN_BLOCK = 128
BAND = 3 * ATTN_BLOCK
ROPE_THETA = 500000.0
ROT_DIM = HEAD_DIM // 4
LANES = 128
assert WINDOW == ATTN_BLOCK
GROUP_Q = GQA_GROUP * ATTN_BLOCK
HEADS_PER_HALF = GQA_GROUP // 2
HALF_Q = HEADS_PER_HALF * ATTN_BLOCK

QKV_COLS = C_WIDTH + 2 * KV_WIDTH
W1_CHUNK_ROWS = 128
W1_CHUNKS = D_MODEL // W1_CHUNK_ROWS
W0_STAGE_CHUNK = (1024, 512)
W0_STAGE_SLOTS = 4

TILE_L0 = 512
TILE_ATTN = 512
VMEM_LIMIT = 56 * 1024 * 1024


def _rms(x, g):
    return x * lax.rsqrt(jnp.mean(x * x, axis=-1, keepdims=True) + EPS) * g


def _silu(x):
    hx = 0.5 * x
    return hx + hx * jnp.tanh(hx)


def _dot(a, b):
    return jnp.dot(a, b, preferred_element_type=F32)


def _dot_nt(a, b):
    return lax.dot_general(a, b, (((1,), (1,)), ((), ())), preferred_element_type=F32)


def _layer0_kernel(x_ref, xp_ref, xn_ref, n0_ref, win_hbm, gv_ref, ws_ref, bsb_ref, wg_ref,
                   sc_ref, wout_hbm, w1_ref, wo1_ref, n1_ref, w1_hbm, cost_ref, sint_ref,
                   o_ref, wgate_ref, wo1b_ref, h1_ref, qt_ref, ka_ref, vt_ref,
                   hext_ref, v_ref, cat_ref, wqkvt_ref, win_ref, wout_ref, stage_ref, wsem,
                   *, tile, seq):
    j = pl.program_id(1)
    nj = pl.num_programs(1)
    step = pl.program_id(0) * nj + j

    @pl.when(step == 0)
    def _():
        slots, cr, cc = stage_ref.shape
        chunks = [(win_hbm, win_ref, r0, c0, False)
                  for r0 in range(0, win_ref.shape[0], cr) for c0 in range(0, win_ref.shape[1], cc)]
        chunks += [(wout_hbm, wout_ref, r0, c0, False)
                   for r0 in range(0, wout_ref.shape[0], cr)
                   for c0 in range(0, wout_ref.shape[1], cc)]
        chunks += [(w1_hbm, wqkvt_ref, r0, c0, True)
                   for r0 in range(0, wqkvt_ref.shape[1], cr)
                   for c0 in range(0, wqkvt_ref.shape[0], cc)]

        def chunk_copy(k):
            src, _, r0, c0, _ = chunks[k]
            return pltpu.make_async_copy(src.at[r0:r0 + cr, c0:c0 + cc], stage_ref.at[k % slots],
                                         wsem.at[k % slots])

        for k in range(min(slots - 1, len(chunks))):
            chunk_copy(k).start()
        for k, (_, dst, r0, c0, transposed) in enumerate(chunks):
            ahead = k + slots - 1
            if ahead < len(chunks):
                chunk_copy(ahead).start()
            chunk_copy(k).wait()
            if transposed:
                dst[c0:c0 + cc, r0:r0 + cr] = stage_ref[k % slots].T.astype(BF16)
            else:
                dst[r0:r0 + cr, c0:c0 + cc] = stage_ref[k % slots].astype(BF16)

    @pl.when(step < W1_CHUNKS)
    def _():
        wgate_ref[...] = w1_ref[:, QKV_COLS:].astype(BF16)
        wo1b_ref[...] = wo1_ref[...].astype(BF16)

    n0 = n0_ref[...]
    x = x_ref[0]
    h = _rms(x, n0).astype(BF16)
    hext_ref[0:tile, :] = h
    hp = jnp.where(j > 0, _rms(xp_ref[0], n0), 0.0)
    hn = jnp.where(j < nj - 1, _rms(xn_ref[0], n0), 0.0)
    hext_ref[tile:tile + 2 * POOL_HALO, :] = jnp.concatenate([hp, hn], axis=0).astype(BF16)

    def col(k):
        return k * D_MODEL

    a_v = jax.nn.gelu(_dot(h, win_ref[:, col(1):col(2)]))
    v_ref[...] = _rms(a_v, gv_ref[...]).astype(BF16)
    for g in range(A_GROUPS):
        c0 = g * A_GROUP_DIM
        c1 = c0 + A_GROUP_DIM
        u = jax.nn.gelu(_dot(h, win_ref[:, col(0) + c0:col(0) + c1]))
        gate = _silu(_dot(h, win_ref[:, col(2) + c0:col(2) + c1]))
        ws_g = ws_ref[g].astype(BF16)
        for c in range(tile // CHUNK):
            r0 = c * CHUNK
            r1 = r0 + CHUNK
            mixed = _dot(ws_g, v_ref[r0:r1, c0:c1]) + bsb_ref[:, c0:c1]
            cat_ref[r0:r1, c0:c1] = (u[r0:r1] * mixed * gate[r0:r1]).astype(BF16)

    bxe = _dot(hext_ref[...], win_ref[:, col(3):col(4)])
    n_ext = tile + 2 * POOL_HALO
    edge_row = lax.broadcasted_iota(jnp.int32, (POOL_HALO, 1), 0)
    for gi, w in enumerate(POOL_WINDOWS):
        c0 = gi * B_GROUP_DIM
        c1 = c0 + B_GROUP_DIM
        half = w // 2
        e = jnp.concatenate([bxe[tile:tile + POOL_HALO, c0:c1], bxe[0:tile, c0:c1],
                             bxe[tile + POOL_HALO:, c0:c1]], axis=0)
        a = e
        span = 1
        while span < half:
            a = a + pltpu.roll(a, n_ext - span, 0)
            span *= 2
        win_sum = (a + pltpu.roll(a, half, 0))[POOL_HALO:POOL_HALO + tile]

        def edge_mean(slab, row0, half=half):
            t = j * tile + row0 + edge_row
            cnt = (jnp.minimum(t + half, seq) - jnp.maximum(t - half, 0)).astype(F32)
            return slab / cnt

        mean = jnp.concatenate(
            [edge_mean(win_sum[0:POOL_HALO], 0),
             win_sum[POOL_HALO:tile - POOL_HALO] * (1.0 / w),
             edge_mean(win_sum[tile - POOL_HALO:], tile - POOL_HALO)], axis=0)
        p = (mean - bxe[0:tile, c0:c1]).astype(BF16)
        gate = _silu(_dot(h, win_ref[:, col(4) + c0:col(4) + c1]))
        yb = _dot(p, wg_ref[gi].astype(BF16)) * sc_ref[:, c0:c1] * gate
        cat_ref[:, A_WIDTH + c0:A_WIDTH + c1] = yb.astype(BF16)

    n1 = n1_ref[...]
    hrows = tile // 2
    for r0 in range(0, tile, hrows):
        rows = slice(r0, r0 + hrows)
        out = x_ref[0, rows, :] + _dot(cat_ref[rows, :], wout_ref[...])
        o_ref[0, rows, :] = out
        h1_ref[0, rows, :] = _rms(out, n1).astype(BF16)
    _qkv_project(h1_ref[0], wqkvt_ref, cost_ref[...], sint_ref[...], qt_ref, ka_ref, vt_ref,
                 slice(0, tile))


def _layer0(x, norm_0, w_in_0, a_v_norm_0, a_spatial_w_0, a_spatial_b_0, b_group_w_0, b_scale_0,
            w_out_0, norm_1, w_in_1, w_out_1):
    bn, seq, d = x.shape
    tile = TILE_L0
    nt = seq // tile
    assert bn * nt >= W1_CHUNKS

    def w1_chunk(b, j):
        return jnp.minimum(b * nt + j, W1_CHUNKS - 1)
    halo_blocks = seq // POOL_HALO
    ws = a_spatial_w_0
    wg = b_group_w_0
    cos_fm, sin_fm = _rope_tables(seq)
    bsb = jnp.repeat(a_spatial_b_0.T, A_GROUP_DIM, axis=1)

    def const(shape):
        return pl.BlockSpec(shape, lambda b, j: (0,) * len(shape), pipeline_mode=pl.Buffered(1))

    per_halo = tile // POOL_HALO
    kern = functools.partial(_layer0_kernel, tile=tile, seq=seq)
    return pl.pallas_call(
        kern,
        grid=(bn, nt),
        in_specs=[
            pl.BlockSpec((1, tile, d), lambda b, j: (b, j, 0)),
            pl.BlockSpec((1, POOL_HALO, d), lambda b, j: (b, jnp.maximum(j * per_halo - 1, 0), 0)),
            pl.BlockSpec((1, POOL_HALO, d),
                         lambda b, j: (b, jnp.minimum((j + 1) * per_halo, halo_blocks - 1), 0)),
            const((1, d)),
            pl.BlockSpec(memory_space=pl.ANY),
            const((1, A_WIDTH)),
            const(ws.shape),
            const(bsb.shape),
            const(wg.shape),
            const((1, B_WIDTH)),
            pl.BlockSpec(memory_space=pl.ANY),
            pl.BlockSpec((W1_CHUNK_ROWS, w_in_1.shape[1]), lambda b, j: (w1_chunk(b, j), 0)),
            pl.BlockSpec((W1_CHUNK_ROWS, d), lambda b, j: (w1_chunk(b, j), 0)),
            const((1, d)),
            pl.BlockSpec(memory_space=pl.ANY),
            pl.BlockSpec((ROT_DIM // 2, tile), lambda b, j: (0, j)),
            pl.BlockSpec((ROT_DIM // 2, tile), lambda b, j: (0, j)),
        ],
        out_specs=[
            pl.BlockSpec((1, tile, d), lambda b, j: (b, j, 0)),
            pl.BlockSpec((W1_CHUNK_ROWS, C_WIDTH), lambda b, j: (w1_chunk(b, j), 0)),
            pl.BlockSpec((W1_CHUNK_ROWS, d), lambda b, j: (w1_chunk(b, j), 0)),
            pl.BlockSpec((1, tile, d), lambda b, j: (b, j, 0)),
            pl.BlockSpec((1, 1, C_WIDTH, tile), lambda b, j: (b, j, 0, 0)),
            pl.BlockSpec((1, tile, KV_WIDTH), lambda b, j: (b, j, 0)),
            pl.BlockSpec((1, 1, KV_WIDTH, tile), lambda b, j: (b, j, 0, 0)),
        ],
        out_shape=[
            jax.ShapeDtypeStruct(x.shape, F32),
            jax.ShapeDtypeStruct((d, C_WIDTH), BF16),
            jax.ShapeDtypeStruct((C_WIDTH, d), BF16),
            jax.ShapeDtypeStruct((bn, seq, d), BF16),
            jax.ShapeDtypeStruct((bn, nt, C_WIDTH, tile), BF16),
            jax.ShapeDtypeStruct((bn, seq, KV_WIDTH), BF16),
            jax.ShapeDtypeStruct((bn, nt, KV_WIDTH, tile), BF16),
        ],
        scratch_shapes=[
            pltpu.VMEM((tile + 2 * POOL_HALO, d), BF16),
            pltpu.VMEM((tile, A_WIDTH), BF16),
            pltpu.VMEM((tile, A_WIDTH + B_WIDTH), BF16),
            pltpu.VMEM((QKV_COLS, d), BF16),
            pltpu.VMEM(w_in_0.shape, BF16),
            pltpu.VMEM(w_out_0.shape, BF16),
            pltpu.VMEM((W0_STAGE_SLOTS,) + W0_STAGE_CHUNK, F32),
            pltpu.SemaphoreType.DMA((W0_STAGE_SLOTS,)),
        ],
        compiler_params=pltpu.CompilerParams(
            dimension_semantics=("arbitrary", "arbitrary"), vmem_limit_bytes=VMEM_LIMIT),
        name="layer0_mixer",
    )(x, x, x, norm_0.reshape(1, d), w_in_0, a_v_norm_0.reshape(1, A_WIDTH), ws, bsb, wg,
      b_scale_0.reshape(1, B_WIDTH), w_out_0, w_in_1, w_out_1, norm_1.reshape(1, d), w_in_1,
      cos_fm, sin_fm)


def _qkv_project(h, w_ref, ct, st, qt_ref, ka_ref, vt_ref, pos):
    half = ROT_DIM // 2
    n = h.shape[0]
    scale = HEAD_DIM ** -0.5 * LOG2_E

    qkv = _dot_nt(w_ref[...], h)

    def rope_rows(r0):
        x1 = qkv[r0:r0 + half]
        x2 = qkv[r0 + half:r0 + ROT_DIM]
        return jnp.concatenate([x1 * ct - x2 * st, x2 * ct + x1 * st], axis=0)

    for hd in range(N_HEADS):
        r0 = hd * HEAD_DIM
        qt_ref[0, 0, r0:r0 + ROT_DIM, pos] = (rope_rows(r0) * scale).astype(BF16)
        qt_ref[0, 0, r0 + ROT_DIM:r0 + HEAD_DIM, pos] = (
            qkv[r0 + ROT_DIM:r0 + HEAD_DIM] * scale).astype(BF16)
    vt_ref[0, 0, :, pos] = qkv[C_WIDTH + KV_WIDTH:].astype(BF16)

    for kp in range(N_KV_HEADS // 2):
        rows = []
        for kh in (2 * kp, 2 * kp + 1):
            r0 = C_WIDTH + kh * HEAD_DIM
            rows += [rope_rows(r0), qkv[r0 + ROT_DIM:r0 + HEAD_DIM]]
        kt = jnp.concatenate(rows, axis=0)
        ka_ref[0, pos, kp * LANES:(kp + 1) * LANES] = kt.T.astype(BF16)


def _rope_tables(seq):
    inv = ROPE_THETA ** (-np.arange(0, ROT_DIM, 2, dtype=np.float64) / ROT_DIM)
    ang = np.arange(seq, dtype=np.float64)[None, :] * inv[:, None]
    return jnp.asarray(np.cos(ang), F32), jnp.asarray(np.sin(ang), F32)


def _attn_kernel(sink_ref, x_ref, h_ref, qt_ref, kap_ref, kat_ref, kan_ref, vtp_ref, vtt_ref,
                 vtn_ref, wgate_ref, wout_ref, fn_ref, o_ref,
                 s_ref, p_ref, sg_ref, y_ref, *, tile, seq):
    j = pl.program_id(1)
    nblk = tile // ATTN_BLOCK
    last_blk = seq // ATTN_BLOCK - 1

    def k_band(i, kh):
        cols = slice(kh // 2 * LANES, (kh // 2 + 1) * LANES)
        lo = max(i - 1, 0) * ATTN_BLOCK
        hi = min(i + 2, nblk) * ATTN_BLOCK
        parts = [kat_ref[0, lo:hi, cols]]
        if i == 0:
            parts.insert(0, kap_ref[0, :, cols])
        if i == nblk - 1:
            parts.append(kan_ref[0, :, cols])
        return jnp.concatenate(parts, axis=0)

    def vt_band(i, kh):
        rows = slice(kh * HEAD_DIM, (kh + 1) * HEAD_DIM)
        lo = max(i - 1, 0) * ATTN_BLOCK
        hi = min(i + 2, nblk) * ATTN_BLOCK
        parts = [vtt_ref[0, 0, rows, lo:hi]]
        if i == 0:
            parts.insert(0, vtp_ref[0, 0, rows, :])
        if i == nblk - 1:
            parts.append(vtn_ref[0, 0, rows, :])
        return parts
    diff = (lax.broadcasted_iota(jnp.int32, (ATTN_BLOCK, ATTN_BLOCK), 0)
            - lax.broadcasted_iota(jnp.int32, (ATTN_BLOCK, ATTN_BLOCK), 1))
    never = 2 * ATTN_BLOCK
    ones_rows = jnp.ones((HEAD_DIM // 2, BAND), BF16)

    def scores(u, half):
        i, kh = divmod(u, N_KV_HEADS)
        slot = u % 2
        c0 = i * ATTN_BLOCK
        h0 = kh * GQA_GROUP + half * HEADS_PER_HALF
        qg = jnp.concatenate(
            [qt_ref[0, 0, (h0 + g) * HEAD_DIM:(h0 + g + 1) * HEAD_DIM, c0:c0 + ATTN_BLOCK]
             for g in range(HEADS_PER_HALF)], axis=1)
        zq = jnp.zeros((LANES - HEAD_DIM, HALF_Q), BF16)
        rq = jnp.concatenate([qg, zq] if kh % 2 == 0 else [zq, qg], axis=0)
        ka = k_band(i, kh)
        s_ref[slot, :, half * HALF_Q:(half + 1) * HALF_Q] = _dot(ka, rq)

    def softmax_head(u, g):
        i, kh = divmod(u, N_KV_HEADS)
        slot = u % 2
        gblk = j * nblk + i
        ok_prev = diff >= jnp.where(gblk > 0, 0, never)
        ok_next = diff <= jnp.where(gblk < last_blk, 0, -never)
        cs = slice(g * ATTN_BLOCK, (g + 1) * ATTN_BLOCK)
        sink = sink_ref[kh * GQA_GROUP + g] * LOG2_E
        s0 = jnp.where(ok_prev, s_ref[slot, 0:ATTN_BLOCK, cs], NEG_INF)
        s1 = s_ref[slot, ATTN_BLOCK:2 * ATTN_BLOCK, cs]
        s2 = jnp.where(ok_next, s_ref[slot, 2 * ATTN_BLOCK:, cs], NEG_INF)
        m = jnp.max(jnp.maximum(jnp.maximum(s0, s1), s2), axis=0, keepdims=True)
        m = jnp.maximum(m, sink)
        p0 = jnp.exp2(s0 - m)
        p1 = jnp.exp2(s1 - m)
        p2 = jnp.exp2(s2 - m)
        p_ref[slot, 0:ATTN_BLOCK, cs] = p0.astype(BF16)
        p_ref[slot, ATTN_BLOCK:2 * ATTN_BLOCK, cs] = p1.astype(BF16)
        p_ref[slot, 2 * ATTN_BLOCK:, cs] = p2.astype(BF16)
        return jnp.exp2(sink - m)

    def weighted(u, half, sink_share):
        i, kh = divmod(u, N_KV_HEADS)
        slot = u % 2
        c0 = i * ATTN_BLOCK
        vt = jnp.concatenate([jnp.concatenate(vt_band(i, kh), axis=1), ones_rows],
                             axis=0)
        ot = _dot(vt, p_ref[slot, :, half * HALF_Q:(half + 1) * HALF_Q])
        parts = []
        for g in range(HEADS_PER_HALF):
            cq = slice(g * ATTN_BLOCK, (g + 1) * ATTN_BLOCK)
            l = ot[HEAD_DIM:HEAD_DIM + 1, cq] + sink_share[g]
            parts.append(ot[0:HEAD_DIM, cq] * (1.0 / l))
        yt = jnp.concatenate(parts, axis=0)
        f0 = (kh * GQA_GROUP + half * HEADS_PER_HALF) * HEAD_DIM
        cs = slice(f0, f0 + HEADS_PER_HALF * HEAD_DIM)
        y = yt.T * sg_ref[c0:c0 + ATTN_BLOCK, cs]
        y_ref[c0:c0 + ATTN_BLOCK, cs] = y.astype(BF16)

    def gate_chunk(kh):
        cs = slice(kh * GQA_GROUP * HEAD_DIM, (kh + 1) * GQA_GROUP * HEAD_DIM)
        sg_ref[:, cs] = _silu(_dot(h_ref[0], wgate_ref[:, cs]))

    def finish_rows(r0, n):
        rows = slice(r0, r0 + n)
        out = x_ref[0, rows, :] + _dot(y_ref[rows, :], wout_ref[...])
        o_ref[0, rows, :] = _rms(out, fn_ref[...])

    units = nblk * N_KV_HEADS
    inv = {}
    for step in range(units + 2):
        sc = step if step < units else None
        sm = step - 1 if 1 <= step <= units else None
        wt = step - 2 if step >= 2 else None
        if step < N_KV_HEADS:
            gate_chunk(step)
        for half in range(2):
            if sc is not None:
                scores(sc, half)
            if sm is not None:
                inv[sm, 2 * half] = softmax_head(sm, 2 * half)
            if wt is not None:
                weighted(wt, half, [inv.pop((wt, half * HEADS_PER_HALF + g))
                                    for g in range(HEADS_PER_HALF)])
            if sm is not None:
                inv[sm, 2 * half + 1] = softmax_head(sm, 2 * half + 1)
        if wt is not None and (wt + 1) % (units // 2) == 0:
            finish_rows((wt + 1 - units // 2) // N_KV_HEADS * ATTN_BLOCK, tile // 2)


def _attention(x, h, qt, ka, vt, w_gate, sink_1, w_out_1, final_norm):
    bn, seq, d = x.shape
    tile = TILE_ATTN
    assert tile == TILE_L0
    nt = seq // tile
    per = tile // ATTN_BLOCK
    nb = seq // ATTN_BLOCK

    def const(shape):
        return pl.BlockSpec(shape, lambda b, j: (0,) * len(shape), pipeline_mode=pl.Buffered(1))

    def prev_blk(j):
        return jnp.maximum(j * per - 1, 0)

    def next_blk(j):
        return jnp.minimum((j + 1) * per, nb - 1)

    kern = functools.partial(_attn_kernel, tile=tile, seq=seq)
    return pl.pallas_call(
        kern,
        grid=(bn, nt),
        in_specs=[
            pl.BlockSpec(memory_space=pltpu.SMEM),
            pl.BlockSpec((1, tile, d), lambda b, j: (b, j, 0)),
            pl.BlockSpec((1, tile, d), lambda b, j: (b, j, 0)),
            pl.BlockSpec((1, 1, C_WIDTH, tile), lambda b, j: (b, j, 0, 0)),
            pl.BlockSpec((1, ATTN_BLOCK, KV_WIDTH), lambda b, j: (b, prev_blk(j), 0)),
            pl.BlockSpec((1, tile, KV_WIDTH), lambda b, j: (b, j, 0)),
            pl.BlockSpec((1, ATTN_BLOCK, KV_WIDTH), lambda b, j: (b, next_blk(j), 0)),
            pl.BlockSpec((1, 1, KV_WIDTH, ATTN_BLOCK),
                         lambda b, j: (b, jnp.maximum(j - 1, 0), 0, per - 1)),
            pl.BlockSpec((1, 1, KV_WIDTH, tile), lambda b, j: (b, j, 0, 0)),
            pl.BlockSpec((1, 1, KV_WIDTH, ATTN_BLOCK),
                         lambda b, j: (b, jnp.minimum(j + 1, nt - 1), 0, 0)),
            const(w_gate.shape),
            const(w_out_1.shape),
            const((1, d)),
        ],
        out_specs=pl.BlockSpec((1, tile, d), lambda b, j: (b, j, 0)),
        out_shape=jax.ShapeDtypeStruct(x.shape, F32),
        scratch_shapes=[
            pltpu.VMEM((2, BAND, GROUP_Q), F32),
            pltpu.VMEM((2, BAND, GROUP_Q), BF16),
            pltpu.VMEM((tile, C_WIDTH), F32),
            pltpu.VMEM((tile, C_WIDTH), BF16),
        ],
        compiler_params=pltpu.CompilerParams(
            dimension_semantics=("arbitrary", "arbitrary"), vmem_limit_bytes=VMEM_LIMIT),
        name="layer1_attention",
    )(sink_1, x, h, qt, ka, ka, ka, vt, vt, vt, w_gate, w_out_1, final_norm.reshape(1, d))


def kernel(x, norm_0, w_in_0, a_v_norm_0, a_spatial_w_0, a_spatial_b_0, b_group_w_0, b_scale_0,
           w_out_0, norm_1, w_in_1, sink_1, w_out_1, final_norm):
    x1, w_gate, w_out_1b, h1, qt, ka, vt = _layer0(
        x, norm_0, w_in_0, a_v_norm_0, a_spatial_w_0, a_spatial_b_0, b_group_w_0, b_scale_0,
        w_out_0, norm_1, w_in_1, w_out_1)
    return _attention(x1, h1, qt, ka, vt, w_gate, sink_1, w_out_1b, final_norm)
```

```python
import functools
import math

import jax
import jax.numpy as jnp
import numpy as np
from jax import lax
from jax.experimental import pallas as pl
from jax.experimental.pallas import tpu as pltpu

F32 = jnp.float32
BF16 = jnp.bfloat16

D_MODEL = 1024
EPS = 1e-6
NEG_INF = -1e30
LOG2_E = math.log2(math.e)
CHUNK = 128
A_GROUPS = 4
A_WIDTH = D_MODEL
A_GROUP_DIM = A_WIDTH // A_GROUPS
B_WIDTH = D_MODEL
POOL_WINDOWS = (2, 4, 8, 16)
B_GROUP_DIM = B_WIDTH // len(POOL_WINDOWS)
POOL_HALO = max(POOL_WINDOWS) // 2
N_HEADS = 16
N_KV_HEADS = 4
HEAD_DIM = 64
GQA_GROUP = N_HEADS // N_KV_HEADS
C_WIDTH = N_HEADS * HEAD_DIM
KV_WIDTH = N_KV_HEADS * HEAD_DIM
WINDOW = 128
ATTN_BLOCK = 128
BAND = 3 * ATTN_BLOCK
ROPE_THETA = 500000.0
ROT_DIM = HEAD_DIM // 4
LANES = 128
assert WINDOW == ATTN_BLOCK
GROUP_Q = GQA_GROUP * ATTN_BLOCK
HEADS_PER_HALF = GQA_GROUP // 2
HALF_Q = HEADS_PER_HALF * ATTN_BLOCK

QKV_COLS = C_WIDTH + 2 * KV_WIDTH
W1_CHUNK_ROWS = 128
W1_CHUNKS = D_MODEL // W1_CHUNK_ROWS
W0_STAGE_CHUNK = (1024, 512)
W0_STAGE_SLOTS = 4

TILE_L0 = 512
TILE_ATTN = 512
VMEM_LIMIT = 56 * 1024 * 1024


def _rms(x, g):
    return x * lax.rsqrt(jnp.mean(x * x, axis=-1, keepdims=True) + EPS) * g


def _silu(x):
    hx = 0.5 * x
    return hx + hx * jnp.tanh(hx)


def _dot(a, b):
    return jnp.dot(a, b, preferred_element_type=F32)


def _dot_nt(a, b):
    return lax.dot_general(a, b, (((1,), (1,)), ((), ())), preferred_element_type=F32)


def _layer0_kernel(x_ref, xp_ref, xn_ref, n0_ref, win_hbm, gv_ref, ws_ref, bsb_ref, wg_ref,
                   sc_ref, wout_hbm, w1_ref, wo1_ref, n1_ref, w1_hbm, cost_ref, sint_ref,
                   o_ref, wgate_ref, wo1b_ref, h1_ref, qt_ref, ka_ref, vt_ref,
                   hext_ref, v_ref, cat_ref, wqkvt_ref, win_ref, wout_ref, stage_ref, wsem,
                   *, tile, seq):
    j = pl.program_id(1)
    nj = pl.num_programs(1)
    step = pl.program_id(0) * nj + j

    slots, cr, cc = stage_ref.shape
    head = [(win_hbm, win_ref, r0, c0, False)
            for r0 in range(0, win_ref.shape[0], cr) for c0 in range(0, win_ref.shape[1], cc)]
    tail = [(wout_hbm, wout_ref, r0, c0, False)
            for r0 in range(0, wout_ref.shape[0], cr) for c0 in range(0, wout_ref.shape[1], cc)]
    tail += [(w1_hbm, wqkvt_ref, r0, c0, True)
             for r0 in range(0, wqkvt_ref.shape[1], cr) for c0 in range(0, wqkvt_ref.shape[0], cc)]

    def chunk_copy(chunks, k):
        src, _, r0, c0, _ = chunks[k]
        return pltpu.make_async_copy(src.at[r0:r0 + cr, c0:c0 + cc], stage_ref.at[k % slots],
                                     wsem.at[k % slots])

    def start_first(chunks):
        for k in range(min(slots, len(chunks))):
            chunk_copy(chunks, k).start()

    def finish_all(chunks):
        for k, (_, dst, r0, c0, transposed) in enumerate(chunks):
            chunk_copy(chunks, k).wait()
            if transposed:
                dst[c0:c0 + cc, r0:r0 + cr] = stage_ref[k % slots].T.astype(BF16)
            else:
                dst[r0:r0 + cr, c0:c0 + cc] = stage_ref[k % slots].astype(BF16)
            if k + slots < len(chunks):
                chunk_copy(chunks, k + slots).start()

    @pl.when(step == 0)
    def _():
        start_first(head)
        finish_all(head)
        start_first(tail)

    @pl.when(step < W1_CHUNKS)
    def _():
        wgate_ref[...] = w1_ref[:, QKV_COLS:].astype(BF16)
        wo1b_ref[...] = wo1_ref[...].astype(BF16)

    n0 = n0_ref[...]
    x = x_ref[0]
    h = _rms(x, n0).astype(BF16)
    hext_ref[0:tile, :] = h
    hp = jnp.where(j > 0, _rms(xp_ref[0], n0), 0.0)
    hn = jnp.where(j < nj - 1, _rms(xn_ref[0], n0), 0.0)
    hext_ref[tile:tile + 2 * POOL_HALO, :] = jnp.concatenate([hp, hn], axis=0).astype(BF16)

    def col(k):
        return k * D_MODEL

    a_v = jax.nn.gelu(_dot(h, win_ref[:, col(1):col(2)]))
    v_ref[...] = _rms(a_v, gv_ref[...]).astype(BF16)
    for g in range(A_GROUPS):
        c0 = g * A_GROUP_DIM
        c1 = c0 + A_GROUP_DIM
        u = jax.nn.gelu(_dot(h, win_ref[:, col(0) + c0:col(0) + c1]))
        gate = _silu(_dot(h, win_ref[:, col(2) + c0:col(2) + c1]))
        ws_g = ws_ref[g].astype(BF16)
        for c in range(tile // CHUNK):
            r0 = c * CHUNK
            r1 = r0 + CHUNK
            mixed = _dot(ws_g, v_ref[r0:r1, c0:c1]) + bsb_ref[:, c0:c1]
            cat_ref[r0:r1, c0:c1] = (u[r0:r1] * mixed * gate[r0:r1]).astype(BF16)

    bxe = _dot(hext_ref[...], win_ref[:, col(3):col(4)])
    n_ext = tile + 2 * POOL_HALO
    edge_row = lax.broadcasted_iota(jnp.int32, (POOL_HALO, 1), 0)
    for gi, w in enumerate(POOL_WINDOWS):
        c0 = gi * B_GROUP_DIM
        c1 = c0 + B_GROUP_DIM
        half = w // 2
        e = jnp.concatenate([bxe[tile:tile + POOL_HALO, c0:c1], bxe[0:tile, c0:c1],
                             bxe[tile + POOL_HALO:, c0:c1]], axis=0)
        a = e
        span = 1
        while span < half:
            a = a + pltpu.roll(a, n_ext - span, 0)
            span *= 2
        win_sum = (a + pltpu.roll(a, half, 0))[POOL_HALO:POOL_HALO + tile]

        def edge_mean(slab, row0, half=half):
            t = j * tile + row0 + edge_row
            cnt = (jnp.minimum(t + half, seq) - jnp.maximum(t - half, 0)).astype(F32)
            return slab / cnt

        mean = jnp.concatenate(
            [edge_mean(win_sum[0:POOL_HALO], 0),
             win_sum[POOL_HALO:tile - POOL_HALO] * (1.0 / w),
             edge_mean(win_sum[tile - POOL_HALO:], tile - POOL_HALO)], axis=0)
        p = (mean - bxe[0:tile, c0:c1]).astype(BF16)
        gate = _silu(_dot(h, win_ref[:, col(4) + c0:col(4) + c1]))
        yb = _dot(p, wg_ref[gi].astype(BF16)) * sc_ref[:, c0:c1] * gate
        cat_ref[:, A_WIDTH + c0:A_WIDTH + c1] = yb.astype(BF16)

    @pl.when(step == 0)
    def _():
        finish_all(tail)

    n1 = n1_ref[...]
    hrows = tile // 2
    for r0 in range(0, tile, hrows):
        rows = slice(r0, r0 + hrows)
        out = x_ref[0, rows, :] + _dot(cat_ref[rows, :], wout_ref[...])
        o_ref[0, rows, :] = out
        h1_ref[0, rows, :] = _rms(out, n1).astype(BF16)
    _qkv_project(h1_ref[0], wqkvt_ref, cost_ref[...], sint_ref[...], qt_ref, ka_ref, vt_ref,
                 slice(0, tile))


def _layer0(x, norm_0, w_in_0, a_v_norm_0, a_spatial_w_0, a_spatial_b_0, b_group_w_0, b_scale_0,
            w_out_0, norm_1, w_in_1, w_out_1):
    bn, seq, d = x.shape
    tile = TILE_L0
    nt = seq // tile
    assert bn * nt >= W1_CHUNKS

    def w1_chunk(b, j):
        return jnp.minimum(b * nt + j, W1_CHUNKS - 1)
    halo_blocks = seq // POOL_HALO
    ws = a_spatial_w_0
    wg = b_group_w_0
    cos_fm, sin_fm = _rope_tables(seq)
    bsb = jnp.repeat(a_spatial_b_0.T, A_GROUP_DIM, axis=1)

    def const(shape):
        return pl.BlockSpec(shape, lambda b, j: (0,) * len(shape), pipeline_mode=pl.Buffered(1))

    per_halo = tile // POOL_HALO
    kern = functools.partial(_layer0_kernel, tile=tile, seq=seq)
    return pl.pallas_call(
        kern,
        grid=(bn, nt),
        in_specs=[
            pl.BlockSpec((1, tile, d), lambda b, j: (b, j, 0)),
            pl.BlockSpec((1, POOL_HALO, d), lambda b, j: (b, jnp.maximum(j * per_halo - 1, 0), 0)),
            pl.BlockSpec((1, POOL_HALO, d),
                         lambda b, j: (b, jnp.minimum((j + 1) * per_halo, halo_blocks - 1), 0)),
            const((1, d)),
            pl.BlockSpec(memory_space=pl.ANY),
            const((1, A_WIDTH)),
            const(ws.shape),
            const(bsb.shape),
            const(wg.shape),
            const((1, B_WIDTH)),
            pl.BlockSpec(memory_space=pl.ANY),
            pl.BlockSpec((W1_CHUNK_ROWS, w_in_1.shape[1]), lambda b, j: (w1_chunk(b, j), 0)),
            pl.BlockSpec((W1_CHUNK_ROWS, d), lambda b, j: (w1_chunk(b, j), 0)),
            const((1, d)),
            pl.BlockSpec(memory_space=pl.ANY),
            pl.BlockSpec((ROT_DIM // 2, tile), lambda b, j: (0, j)),
            pl.BlockSpec((ROT_DIM // 2, tile), lambda b, j: (0, j)),
        ],
        out_specs=[
            pl.BlockSpec((1, tile, d), lambda b, j: (b, j, 0)),
            pl.BlockSpec((W1_CHUNK_ROWS, C_WIDTH), lambda b, j: (w1_chunk(b, j), 0)),
            pl.BlockSpec((W1_CHUNK_ROWS, d), lambda b, j: (w1_chunk(b, j), 0)),
            pl.BlockSpec((1, tile, d), lambda b, j: (b, j, 0)),
            pl.BlockSpec((1, 1, C_WIDTH, tile), lambda b, j: (b, j, 0, 0)),
            pl.BlockSpec((1, tile, KV_WIDTH), lambda b, j: (b, j, 0)),
            pl.BlockSpec((1, 1, KV_WIDTH, tile), lambda b, j: (b, j, 0, 0)),
        ],
        out_shape=[
            jax.ShapeDtypeStruct(x.shape, F32),
            jax.ShapeDtypeStruct((d, C_WIDTH), BF16),
            jax.ShapeDtypeStruct((C_WIDTH, d), BF16),
            jax.ShapeDtypeStruct((bn, seq, d), BF16),
            jax.ShapeDtypeStruct((bn, nt, C_WIDTH, tile), BF16),
            jax.ShapeDtypeStruct((bn, seq, KV_WIDTH), BF16),
            jax.ShapeDtypeStruct((bn, nt, KV_WIDTH, tile), BF16),
        ],
        scratch_shapes=[
            pltpu.VMEM((tile + 2 * POOL_HALO, d), BF16),
            pltpu.VMEM((tile, A_WIDTH), BF16),
            pltpu.VMEM((tile, A_WIDTH + B_WIDTH), BF16),
            pltpu.VMEM((QKV_COLS, d), BF16),
            pltpu.VMEM(w_in_0.shape, BF16),
            pltpu.VMEM(w_out_0.shape, BF16),
            pltpu.VMEM((W0_STAGE_SLOTS,) + W0_STAGE_CHUNK, F32),
            pltpu.SemaphoreType.DMA((W0_STAGE_SLOTS,)),
        ],
        compiler_params=pltpu.CompilerParams(
            dimension_semantics=("arbitrary", "arbitrary"), vmem_limit_bytes=VMEM_LIMIT),
        name="layer0_mixer",
    )(x, x, x, norm_0.reshape(1, d), w_in_0, a_v_norm_0.reshape(1, A_WIDTH), ws, bsb, wg,
      b_scale_0.reshape(1, B_WIDTH), w_out_0, w_in_1, w_out_1, norm_1.reshape(1, d), w_in_1,
      cos_fm, sin_fm)


def _qkv_project(h, w_ref, ct, st, qt_ref, ka_ref, vt_ref, pos):
    half = ROT_DIM // 2
    n = h.shape[0]
    scale = HEAD_DIM ** -0.5 * LOG2_E

    qkv = _dot_nt(w_ref[...], h)

    def rope_rows(r0):
        x1 = qkv[r0:r0 + half]
        x2 = qkv[r0 + half:r0 + ROT_DIM]
        return jnp.concatenate([x1 * ct - x2 * st, x2 * ct + x1 * st], axis=0)

    for hd in range(N_HEADS):
        r0 = hd * HEAD_DIM
        qt_ref[0, 0, r0:r0 + ROT_DIM, pos] = (rope_rows(r0) * scale).astype(BF16)
        qt_ref[0, 0, r0 + ROT_DIM:r0 + HEAD_DIM, pos] = (
            qkv[r0 + ROT_DIM:r0 + HEAD_DIM] * scale).astype(BF16)
    vt_ref[0, 0, :, pos] = qkv[C_WIDTH + KV_WIDTH:].astype(BF16)

    for kp in range(N_KV_HEADS // 2):
        rows = []
        for kh in (2 * kp, 2 * kp + 1):
            r0 = C_WIDTH + kh * HEAD_DIM
            rows += [rope_rows(r0), qkv[r0 + ROT_DIM:r0 + HEAD_DIM]]
        kt = jnp.concatenate(rows, axis=0)
        ka_ref[0, pos, kp * LANES:(kp + 1) * LANES] = kt.T.astype(BF16)


def _rope_tables(seq):
    inv = ROPE_THETA ** (-np.arange(0, ROT_DIM, 2, dtype=np.float64) / ROT_DIM)
    ang = np.arange(seq, dtype=np.float64)[None, :] * inv[:, None]
    return jnp.asarray(np.cos(ang), F32), jnp.asarray(np.sin(ang), F32)


def _attn_kernel(sink_ref, x_ref, h_ref, qt_ref, kap_ref, kat_ref, kan_ref, vtp_ref, vtt_ref,
                 vtn_ref, wgate_ref, wout_ref, fn_ref, o_ref,
                 s_ref, p_ref, sg_ref, y_ref, *, tile, seq):
    j = pl.program_id(1)
    nblk = tile // ATTN_BLOCK
    last_blk = seq // ATTN_BLOCK - 1

    def k_band(i, kh):
        cols = slice(kh // 2 * LANES, (kh // 2 + 1) * LANES)
        lo = max(i - 1, 0) * ATTN_BLOCK
        hi = min(i + 2, nblk) * ATTN_BLOCK
        parts = [kat_ref[0, lo:hi, cols]]
        if i == 0:
            parts.insert(0, kap_ref[0, :, cols])
        if i == nblk - 1:
            parts.append(kan_ref[0, :, cols])
        return jnp.concatenate(parts, axis=0)

    def vt_band(i, kh):
        rows = slice(kh * HEAD_DIM, (kh + 1) * HEAD_DIM)
        lo = max(i - 1, 0) * ATTN_BLOCK
        hi = min(i + 2, nblk) * ATTN_BLOCK
        parts = [vtt_ref[0, 0, rows, lo:hi]]
        if i == 0:
            parts.insert(0, vtp_ref[0, 0, rows, :])
        if i == nblk - 1:
            parts.append(vtn_ref[0, 0, rows, :])
        return parts
    diff = (lax.broadcasted_iota(jnp.int32, (ATTN_BLOCK, ATTN_BLOCK), 0)
            - lax.broadcasted_iota(jnp.int32, (ATTN_BLOCK, ATTN_BLOCK), 1))
    never = 2 * ATTN_BLOCK
    ones_rows = jnp.ones((HEAD_DIM // 2, BAND), BF16)

    def scores(u, half):
        i, kh = divmod(u, N_KV_HEADS)
        slot = u % 2
        c0 = i * ATTN_BLOCK
        h0 = kh * GQA_GROUP + half * HEADS_PER_HALF
        qg = jnp.concatenate(
            [qt_ref[0, 0, (h0 + g) * HEAD_DIM:(h0 + g + 1) * HEAD_DIM, c0:c0 + ATTN_BLOCK]
             for g in range(HEADS_PER_HALF)], axis=1)
        zq = jnp.zeros((LANES - HEAD_DIM, HALF_Q), BF16)
        rq = jnp.concatenate([qg, zq] if kh % 2 == 0 else [zq, qg], axis=0)
        ka = k_band(i, kh)
        s_ref[slot, :, half * HALF_Q:(half + 1) * HALF_Q] = _dot(ka, rq)

    def softmax_head(u, g):
        i, kh = divmod(u, N_KV_HEADS)
        slot = u % 2
        gblk = j * nblk + i
        ok_prev = diff >= jnp.where(gblk > 0, 0, never)
        ok_next = diff <= jnp.where(gblk < last_blk, 0, -never)
        cs = slice(g * ATTN_BLOCK, (g + 1) * ATTN_BLOCK)
        sink = sink_ref[kh * GQA_GROUP + g] * LOG2_E
        s0 = jnp.where(ok_prev, s_ref[slot, 0:ATTN_BLOCK, cs], NEG_INF)
        s1 = s_ref[slot, ATTN_BLOCK:2 * ATTN_BLOCK, cs]
        s2 = jnp.where(ok_next, s_ref[slot, 2 * ATTN_BLOCK:, cs], NEG_INF)
        m = jnp.max(jnp.maximum(jnp.maximum(s0, s1), s2), axis=0, keepdims=True)
        m = jnp.maximum(m, sink)
        p0 = jnp.exp2(s0 - m)
        p1 = jnp.exp2(s1 - m)
        p2 = jnp.exp2(s2 - m)
        p_ref[slot, 0:ATTN_BLOCK, cs] = p0.astype(BF16)
        p_ref[slot, ATTN_BLOCK:2 * ATTN_BLOCK, cs] = p1.astype(BF16)
        p_ref[slot, 2 * ATTN_BLOCK:, cs] = p2.astype(BF16)
        return jnp.exp2(sink - m)

    def weighted(u, half, sink_share):
        i, kh = divmod(u, N_KV_HEADS)
        slot = u % 2
        c0 = i * ATTN_BLOCK
        vt = jnp.concatenate([jnp.concatenate(vt_band(i, kh), axis=1), ones_rows],
                             axis=0)
        ot = _dot(vt, p_ref[slot, :, half * HALF_Q:(half + 1) * HALF_Q])
        parts = []
        for g in range(HEADS_PER_HALF):
            cq = slice(g * ATTN_BLOCK, (g + 1) * ATTN_BLOCK)
            l = ot[HEAD_DIM:HEAD_DIM + 1, cq] + sink_share[g]
            parts.append(ot[0:HEAD_DIM, cq] * (1.0 / l))
        yt = jnp.concatenate(parts, axis=0)
        f0 = (kh * GQA_GROUP + half * HEADS_PER_HALF) * HEAD_DIM
        cs = slice(f0, f0 + HEADS_PER_HALF * HEAD_DIM)
        y = yt.T * sg_ref[c0:c0 + ATTN_BLOCK, cs]
        y_ref[c0:c0 + ATTN_BLOCK, cs] = y.astype(BF16)

    def gate_chunk(kh):
        cs = slice(kh * GQA_GROUP * HEAD_DIM, (kh + 1) * GQA_GROUP * HEAD_DIM)
        sg_ref[:, cs] = _silu(_dot(h_ref[0], wgate_ref[:, cs]))

    def finish_rows(r0, n):
        rows = slice(r0, r0 + n)
        out = x_ref[0, rows, :] + _dot(y_ref[rows, :], wout_ref[...])
        o_ref[0, rows, :] = _rms(out, fn_ref[...])

    units = nblk * N_KV_HEADS
    inv = {}
    for step in range(units + 2):
        sc = step if step < units else None
        sm = step - 1 if 1 <= step <= units else None
        wt = step - 2 if step >= 2 else None
        if step < N_KV_HEADS:
            gate_chunk(step)
        for half in range(2):
            if sc is not None:
                scores(sc, half)
            if sm is not None:
                inv[sm, 2 * half] = softmax_head(sm, 2 * half)
            if wt is not None:
                weighted(wt, half, [inv.pop((wt, half * HEADS_PER_HALF + g))
                                    for g in range(HEADS_PER_HALF)])
            if sm is not None:
                inv[sm, 2 * half + 1] = softmax_head(sm, 2 * half + 1)
        if wt is not None and (wt + 1) % (units // 2) == 0:
            finish_rows((wt + 1 - units // 2) // N_KV_HEADS * ATTN_BLOCK, tile // 2)


def _attention(x, h, qt, ka, vt, w_gate, sink_1, w_out_1, final_norm):
    bn, seq, d = x.shape
    tile = TILE_ATTN
    assert tile == TILE_L0
    nt = seq // tile
    per = tile // ATTN_BLOCK
    nb = seq // ATTN_BLOCK

    def const(shape):
        return pl.BlockSpec(shape, lambda b, j: (0,) * len(shape), pipeline_mode=pl.Buffered(1))

    def prev_blk(j):
        return jnp.maximum(j * per - 1, 0)

    def next_blk(j):
        return jnp.minimum((j + 1) * per, nb - 1)

    kern = functools.partial(_attn_kernel, tile=tile, seq=seq)
    return pl.pallas_call(
        kern,
        grid=(bn, nt),
        in_specs=[
            pl.BlockSpec(memory_space=pltpu.SMEM),
            pl.BlockSpec((1, tile, d), lambda b, j: (b, j, 0)),
            pl.BlockSpec((1, tile, d), lambda b, j: (b, j, 0)),
            pl.BlockSpec((1, 1, C_WIDTH, tile), lambda b, j: (b, j, 0, 0)),
            pl.BlockSpec((1, ATTN_BLOCK, KV_WIDTH), lambda b, j: (b, prev_blk(j), 0)),
            pl.BlockSpec((1, tile, KV_WIDTH), lambda b, j: (b, j, 0)),
            pl.BlockSpec((1, ATTN_BLOCK, KV_WIDTH), lambda b, j: (b, next_blk(j), 0)),
            pl.BlockSpec((1, 1, KV_WIDTH, ATTN_BLOCK),
                         lambda b, j: (b, jnp.maximum(j - 1, 0), 0, per - 1)),
            pl.BlockSpec((1, 1, KV_WIDTH, tile), lambda b, j: (b, j, 0, 0)),
            pl.BlockSpec((1, 1, KV_WIDTH, ATTN_BLOCK),
                         lambda b, j: (b, jnp.minimum(j + 1, nt - 1), 0, 0)),
            const(w_gate.shape),
            const(w_out_1.shape),
            const((1, d)),
        ],
        out_specs=pl.BlockSpec((1, tile, d), lambda b, j: (b, j, 0)),
        out_shape=jax.ShapeDtypeStruct(x.shape, F32),
        scratch_shapes=[
            pltpu.VMEM((2, BAND, GROUP_Q), F32),
            pltpu.VMEM((2, BAND, GROUP_Q), BF16),
            pltpu.VMEM((tile, C_WIDTH), F32),
            pltpu.VMEM((tile, C_WIDTH), BF16),
        ],
        compiler_params=pltpu.CompilerParams(
            dimension_semantics=("arbitrary", "arbitrary"), vmem_limit_bytes=VMEM_LIMIT),
        name="layer1_attention",
    )(sink_1, x, h, qt, ka, ka, ka, vt, vt, vt, w_gate, w_out_1, final_norm.reshape(1, d))


def kernel(x, norm_0, w_in_0, a_v_norm_0, a_spatial_w_0, a_spatial_b_0, b_group_w_0, b_scale_0,
           w_out_0, norm_1, w_in_1, sink_1, w_out_1, final_norm):
    x1, w_gate, w_out_1b, h1, qt, ka, vt = _layer0(
        x, norm_0, w_in_0, a_v_norm_0, a_spatial_w_0, a_spatial_b_0, b_group_w_0, b_scale_0,
        w_out_0, norm_1, w_in_1, w_out_1)
    return _attention(x1, h1, qt, ka, vt, w_gate, sink_1, w_out_1b, final_norm)
```

```python
import functools
import math

import jax
import jax.numpy as jnp
import numpy as np
from jax import lax
from jax.experimental import pallas as pl
from jax.experimental.pallas import tpu as pltpu

F32 = jnp.float32
BF16 = jnp.bfloat16

D_MODEL = 1024
EPS = 1e-6
NEG_INF = -1e30
LOG2_E = math.log2(math.e)
CHUNK = 128
A_GROUPS = 4
A_WIDTH = D_MODEL
A_GROUP_DIM = A_WIDTH // A_GROUPS
B_WIDTH = D_MODEL
POOL_WINDOWS = (2, 4, 8, 16)
B_GROUP_DIM = B_WIDTH // len(POOL_WINDOWS)
POOL_HALO = max(POOL_WINDOWS) // 2
N_HEADS = 16
N_KV_HEADS = 4
HEAD_DIM = 64
GQA_GROUP = N_HEADS // N_KV_HEADS
C_WIDTH = N_HEADS * HEAD_DIM
KV_WIDTH = N_KV_HEADS * HEAD_DIM
WINDOW = 128
ATTN_BLOCK = 128
BAND = 3 * ATTN_BLOCK
ROPE_THETA = 500000.0
ROT_DIM = HEAD_DIM // 4
LANES = 128
assert WINDOW == ATTN_BLOCK
GROUP_Q = GQA_GROUP * ATTN_BLOCK
HEADS_PER_HALF = GQA_GROUP // 2
HALF_Q = HEADS_PER_HALF * ATTN_BLOCK
SUM_ROWS = 32

QKV_COLS = C_WIDTH + 2 * KV_WIDTH
W1_CHUNK_ROWS = 128
W1_CHUNKS = D_MODEL // W1_CHUNK_ROWS
W0_STAGE_CHUNK = (1024, 512)
W0_STAGE_SLOTS = 6

TILE_L0 = 512
TILE_ATTN = 512
VMEM_LIMIT = 56 * 1024 * 1024


def _rms(x, g):
    return x * lax.rsqrt(jnp.mean(x * x, axis=-1, keepdims=True) + EPS) * g


def _silu(x):
    hx = 0.5 * x
    return hx + hx * jnp.tanh(hx)


def _dot(a, b):
    return jnp.dot(a, b, preferred_element_type=F32)


def _dot_nt(a, b):
    return lax.dot_general(a, b, (((1,), (1,)), ((), ())), preferred_element_type=F32)


def _layer0_kernel(x_ref, xp_ref, xn_ref, n0_ref, win_hbm, gv_ref, ws_ref, bsb_ref, wg_ref,
                   sc_ref, wout_hbm, w1_ref, wo1_ref, n1_ref, w1_hbm, cost_ref, sint_ref,
                   o_ref, wgate_ref, wo1b_ref, h1_ref, qt_ref, ka_ref, vt_ref,
                   hext_ref, v_ref, cat_ref, wqkvt_ref, win_ref, wout_ref, stage_ref, wsem,
                   *, tile, seq):
    j = pl.program_id(1)
    nj = pl.num_programs(1)
    step = pl.program_id(0) * nj + j

    @pl.when(step == 0)
    def _():
        slots, cr, cc = stage_ref.shape
        chunks = [(win_hbm, win_ref, r0, c0, False)
                  for r0 in range(0, win_ref.shape[0], cr) for c0 in range(0, win_ref.shape[1], cc)]
        chunks += [(wout_hbm, wout_ref, r0, c0, False)
                   for r0 in range(0, wout_ref.shape[0], cr)
                   for c0 in range(0, wout_ref.shape[1], cc)]
        chunks += [(w1_hbm, wqkvt_ref, r0, c0, True)
                   for r0 in range(0, wqkvt_ref.shape[1], cr)
                   for c0 in range(0, wqkvt_ref.shape[0], cc)]

        def chunk_copy(k):
            src, _, r0, c0, _ = chunks[k]
            return pltpu.make_async_copy(src.at[r0:r0 + cr, c0:c0 + cc], stage_ref.at[k % slots],
                                         wsem.at[k % slots])

        for k in range(min(slots - 1, len(chunks))):
            chunk_copy(k).start()
        for k, (_, dst, r0, c0, transposed) in enumerate(chunks):
            ahead = k + slots - 1
            if ahead < len(chunks):
                chunk_copy(ahead).start()
            chunk_copy(k).wait()
            if transposed:
                dst[c0:c0 + cc, r0:r0 + cr] = stage_ref[k % slots].T.astype(BF16)
            else:
                dst[r0:r0 + cr, c0:c0 + cc] = stage_ref[k % slots].astype(BF16)

    @pl.when(step < W1_CHUNKS)
    def _():
        wgate_ref[...] = w1_ref[:, QKV_COLS:].astype(BF16)
        wo1b_ref[...] = wo1_ref[...].astype(BF16)

    n0 = n0_ref[...]
    x = x_ref[0]
    h = _rms(x, n0).astype(BF16)
    hext_ref[0:tile, :] = h
    hp = jnp.where(j > 0, _rms(xp_ref[0], n0), 0.0)
    hn = jnp.where(j < nj - 1, _rms(xn_ref[0], n0), 0.0)
    hext_ref[tile:tile + 2 * POOL_HALO, :] = jnp.concatenate([hp, hn], axis=0).astype(BF16)

    def col(k):
        return k * D_MODEL

    a_v = jax.nn.gelu(_dot(h, win_ref[:, col(1):col(2)]))
    v_ref[...] = _rms(a_v, gv_ref[...]).astype(BF16)
    for g in range(A_GROUPS):
        c0 = g * A_GROUP_DIM
        c1 = c0 + A_GROUP_DIM
        u = jax.nn.gelu(_dot(h, win_ref[:, col(0) + c0:col(0) + c1]))
        gate = _silu(_dot(h, win_ref[:, col(2) + c0:col(2) + c1]))
        ws_g = ws_ref[g].astype(BF16)
        for c in range(tile // CHUNK):
            r0 = c * CHUNK
            r1 = r0 + CHUNK
            mixed = _dot(ws_g, v_ref[r0:r1, c0:c1]) + bsb_ref[:, c0:c1]
            cat_ref[r0:r1, c0:c1] = (u[r0:r1] * mixed * gate[r0:r1]).astype(BF16)

    bxe = _dot(hext_ref[...], win_ref[:, col(3):col(4)])
    n_ext = tile + 2 * POOL_HALO
    edge_row = lax.broadcasted_iota(jnp.int32, (POOL_HALO, 1), 0)
    for gi, w in enumerate(POOL_WINDOWS):
        c0 = gi * B_GROUP_DIM
        c1 = c0 + B_GROUP_DIM
        half = w // 2
        e = jnp.concatenate([bxe[tile:tile + POOL_HALO, c0:c1], bxe[0:tile, c0:c1],
                             bxe[tile + POOL_HALO:, c0:c1]], axis=0)
        a = e
        span = 1
        while span < half:
            a = a + pltpu.roll(a, n_ext - span, 0)
            span *= 2
        win_sum = (a + pltpu.roll(a, half, 0))[POOL_HALO:POOL_HALO + tile]

        def edge_mean(slab, row0, half=half):
            t = j * tile + row0 + edge_row
            cnt = (jnp.minimum(t + half, seq) - jnp.maximum(t - half, 0)).astype(F32)
            return slab / cnt

        mean = jnp.concatenate(
            [edge_mean(win_sum[0:POOL_HALO], 0),
             win_sum[POOL_HALO:tile - POOL_HALO] * (1.0 / w),
             edge_mean(win_sum[tile - POOL_HALO:], tile - POOL_HALO)], axis=0)
        p = (mean - bxe[0:tile, c0:c1]).astype(BF16)
        gate = _silu(_dot(h, win_ref[:, col(4) + c0:col(4) + c1]))
        yb = _dot(p, wg_ref[gi].astype(BF16)) * sc_ref[:, c0:c1] * gate
        cat_ref[:, A_WIDTH + c0:A_WIDTH + c1] = yb.astype(BF16)

    n1 = n1_ref[...]
    hrows = tile // 2
    for r0 in range(0, tile, hrows):
        rows = slice(r0, r0 + hrows)
        out = x_ref[0, rows, :] + _dot(cat_ref[rows, :], wout_ref[...])
        o_ref[0, rows, :] = out
        h1_ref[0, rows, :] = _rms(out, n1).astype(BF16)
    _qkv_project(h1_ref[0], wqkvt_ref, cost_ref[...], sint_ref[...], qt_ref, ka_ref, vt_ref,
                 slice(0, tile))


def _layer0(x, norm_0, w_in_0, a_v_norm_0, a_spatial_w_0, a_spatial_b_0, b_group_w_0, b_scale_0,
            w_out_0, norm_1, w_in_1, w_out_1):
    bn, seq, d = x.shape
    tile = TILE_L0
    nt = seq // tile
    assert bn * nt >= W1_CHUNKS

    def w1_chunk(b, j):
        return jnp.minimum(b * nt + j, W1_CHUNKS - 1)
    halo_blocks = seq // POOL_HALO
    ws = a_spatial_w_0
    wg = b_group_w_0
    cos_fm, sin_fm = _rope_tables(seq)
    bsb = jnp.repeat(a_spatial_b_0.T, A_GROUP_DIM, axis=1)

    def const(shape):
        return pl.BlockSpec(shape, lambda b, j: (0,) * len(shape), pipeline_mode=pl.Buffered(1))

    per_halo = tile // POOL_HALO
    kern = functools.partial(_layer0_kernel, tile=tile, seq=seq)
    return pl.pallas_call(
        kern,
        grid=(bn, nt),
        in_specs=[
            pl.BlockSpec((1, tile, d), lambda b, j: (b, j, 0)),
            pl.BlockSpec((1, POOL_HALO, d), lambda b, j: (b, jnp.maximum(j * per_halo - 1, 0), 0)),
            pl.BlockSpec((1, POOL_HALO, d),
                         lambda b, j: (b, jnp.minimum((j + 1) * per_halo, halo_blocks - 1), 0)),
            const((1, d)),
            pl.BlockSpec(memory_space=pl.ANY),
            const((1, A_WIDTH)),
            const(ws.shape),
            const(bsb.shape),
            const(wg.shape),
            const((1, B_WIDTH)),
            pl.BlockSpec(memory_space=pl.ANY),
            pl.BlockSpec((W1_CHUNK_ROWS, w_in_1.shape[1]), lambda b, j: (w1_chunk(b, j), 0)),
            pl.BlockSpec((W1_CHUNK_ROWS, d), lambda b, j: (w1_chunk(b, j), 0)),
            const((1, d)),
            pl.BlockSpec(memory_space=pl.ANY),
            pl.BlockSpec((ROT_DIM // 2, tile), lambda b, j: (0, j)),
            pl.BlockSpec((ROT_DIM // 2, tile), lambda b, j: (0, j)),
        ],
        out_specs=[
            pl.BlockSpec((1, tile, d), lambda b, j: (b, j, 0)),
            pl.BlockSpec((W1_CHUNK_ROWS, C_WIDTH), lambda b, j: (w1_chunk(b, j), 0)),
            pl.BlockSpec((W1_CHUNK_ROWS, d), lambda b, j: (w1_chunk(b, j), 0)),
            pl.BlockSpec((1, tile, d), lambda b, j: (b, j, 0)),
            pl.BlockSpec((1, 1, C_WIDTH, tile), lambda b, j: (b, j, 0, 0)),
            pl.BlockSpec((1, tile, KV_WIDTH), lambda b, j: (b, j, 0)),
            pl.BlockSpec((1, 1, KV_WIDTH, tile), lambda b, j: (b, j, 0, 0)),
        ],
        out_shape=[
            jax.ShapeDtypeStruct(x.shape, F32),
            jax.ShapeDtypeStruct((d, C_WIDTH), BF16),
            jax.ShapeDtypeStruct((C_WIDTH, d), BF16),
            jax.ShapeDtypeStruct((bn, seq, d), BF16),
            jax.ShapeDtypeStruct((bn, nt, C_WIDTH, tile), BF16),
            jax.ShapeDtypeStruct((bn, seq, KV_WIDTH), BF16),
            jax.ShapeDtypeStruct((bn, nt, KV_WIDTH, tile), BF16),
        ],
        scratch_shapes=[
            pltpu.VMEM((tile + 2 * POOL_HALO, d), BF16),
            pltpu.VMEM((tile, A_WIDTH), BF16),
            pltpu.VMEM((tile, A_WIDTH + B_WIDTH), BF16),
            pltpu.VMEM((QKV_COLS, d), BF16),
            pltpu.VMEM(w_in_0.shape, BF16),
            pltpu.VMEM(w_out_0.shape, BF16),
            pltpu.VMEM((W0_STAGE_SLOTS,) + W0_STAGE_CHUNK, F32),
            pltpu.SemaphoreType.DMA((W0_STAGE_SLOTS,)),
        ],
        compiler_params=pltpu.CompilerParams(
            dimension_semantics=("arbitrary", "arbitrary"), vmem_limit_bytes=VMEM_LIMIT),
        name="layer0_mixer",
    )(x, x, x, norm_0.reshape(1, d), w_in_0, a_v_norm_0.reshape(1, A_WIDTH), ws, bsb, wg,
      b_scale_0.reshape(1, B_WIDTH), w_out_0, w_in_1, w_out_1, norm_1.reshape(1, d), w_in_1,
      cos_fm, sin_fm)


def _qkv_project(h, w_ref, ct, st, qt_ref, ka_ref, vt_ref, pos):
    half = ROT_DIM // 2
    n = h.shape[0]
    scale = HEAD_DIM ** -0.5 * LOG2_E

    qkv = _dot_nt(w_ref[...], h)

    def rope_rows(r0):
        x1 = qkv[r0:r0 + half]
        x2 = qkv[r0 + half:r0 + ROT_DIM]
        return jnp.concatenate([x1 * ct - x2 * st, x2 * ct + x1 * st], axis=0)

    for hd in range(N_HEADS):
        r0 = hd * HEAD_DIM
        qt_ref[0, 0, r0:r0 + ROT_DIM, pos] = (rope_rows(r0) * scale).astype(BF16)
        qt_ref[0, 0, r0 + ROT_DIM:r0 + HEAD_DIM, pos] = (
            qkv[r0 + ROT_DIM:r0 + HEAD_DIM] * scale).astype(BF16)
    vt_ref[0, 0, :, pos] = qkv[C_WIDTH + KV_WIDTH:].astype(BF16)

    for kp in range(N_KV_HEADS // 2):
        rows = []
        for kh in (2 * kp, 2 * kp + 1):
            r0 = C_WIDTH + kh * HEAD_DIM
            rows += [rope_rows(r0), qkv[r0 + ROT_DIM:r0 + HEAD_DIM]]
        kt = jnp.concatenate(rows, axis=0)
        ka_ref[0, pos, kp * LANES:(kp + 1) * LANES] = kt.T.astype(BF16)


def _rope_tables(seq):
    inv = ROPE_THETA ** (-np.arange(0, ROT_DIM, 2, dtype=np.float64) / ROT_DIM)
    ang = np.arange(seq, dtype=np.float64)[None, :] * inv[:, None]
    return jnp.asarray(np.cos(ang), F32), jnp.asarray(np.sin(ang), F32)


def _attn_kernel(sink_ref, x_ref, h_ref, qt_ref, kap_ref, kat_ref, kan_ref, vtp_ref, vtt_ref,
                 vtn_ref, wgate_ref, wout_ref, fn_ref, o_ref,
                 s_ref, p_ref, sg_ref, y_ref, *, tile, seq):
    j = pl.program_id(1)
    nblk = tile // ATTN_BLOCK
    last_blk = seq // ATTN_BLOCK - 1

    def k_band(i, kh):
        cols = slice(kh // 2 * LANES, (kh // 2 + 1) * LANES)
        lo = max(i - 1, 0) * ATTN_BLOCK
        hi = min(i + 2, nblk) * ATTN_BLOCK
        parts = [kat_ref[0, lo:hi, cols]]
        if i == 0:
            parts.insert(0, kap_ref[0, :, cols])
        if i == nblk - 1:
            parts.append(kan_ref[0, :, cols])
        return jnp.concatenate(parts, axis=0)

    def vt_band(i, kh):
        rows = slice(kh * HEAD_DIM, (kh + 1) * HEAD_DIM)
        lo = max(i - 1, 0) * ATTN_BLOCK
        hi = min(i + 2, nblk) * ATTN_BLOCK
        parts = [vtt_ref[0, 0, rows, lo:hi]]
        if i == 0:
            parts.insert(0, vtp_ref[0, 0, rows, :])
        if i == nblk - 1:
            parts.append(vtn_ref[0, 0, rows, :])
        return parts
    diff = (lax.broadcasted_iota(jnp.int32, (ATTN_BLOCK, ATTN_BLOCK), 0)
            - lax.broadcasted_iota(jnp.int32, (ATTN_BLOCK, ATTN_BLOCK), 1))
    never = 2 * ATTN_BLOCK
    ones_rows = jnp.ones((SUM_ROWS, BAND), BF16)

    def scores(u, half):
        i, kh = divmod(u, N_KV_HEADS)
        slot = u % 2
        c0 = i * ATTN_BLOCK
        h0 = kh * GQA_GROUP + half * HEADS_PER_HALF
        qg = jnp.concatenate(
            [qt_ref[0, 0, (h0 + g) * HEAD_DIM:(h0 + g + 1) * HEAD_DIM, c0:c0 + ATTN_BLOCK]
             for g in range(HEADS_PER_HALF)], axis=1)
        zq = jnp.zeros((LANES - HEAD_DIM, HALF_Q), BF16)
        rq = jnp.concatenate([qg, zq] if kh % 2 == 0 else [zq, qg], axis=0)
        ka = k_band(i, kh)
        s_ref[slot, :, half * HALF_Q:(half + 1) * HALF_Q] = _dot(ka, rq)

    def softmax_head(u, g):
        i, kh = divmod(u, N_KV_HEADS)
        slot = u % 2
        gblk = j * nblk + i
        ok_prev = diff >= jnp.where(gblk > 0, 0, never)
        ok_next = diff <= jnp.where(gblk < last_blk, 0, -never)
        cs = slice(g * ATTN_BLOCK, (g + 1) * ATTN_BLOCK)
        sink = sink_ref[kh * GQA_GROUP + g] * LOG2_E
        s0 = jnp.where(ok_prev, s_ref[slot, 0:ATTN_BLOCK, cs], NEG_INF)
        s1 = s_ref[slot, ATTN_BLOCK:2 * ATTN_BLOCK, cs]
        s2 = jnp.where(ok_next, s_ref[slot, 2 * ATTN_BLOCK:, cs], NEG_INF)
        m = jnp.max(jnp.maximum(jnp.maximum(s0, s1), s2), axis=0, keepdims=True)
        m = jnp.maximum(m, sink)
        p0 = jnp.exp2(s0 - m)
        p1 = jnp.exp2(s1 - m)
        p2 = jnp.exp2(s2 - m)
        p_ref[slot, 0:ATTN_BLOCK, cs] = p0.astype(BF16)
        p_ref[slot, ATTN_BLOCK:2 * ATTN_BLOCK, cs] = p1.astype(BF16)
        p_ref[slot, 2 * ATTN_BLOCK:, cs] = p2.astype(BF16)
        return jnp.exp2(sink - m)

    def weighted(u, half, sink_share):
        i, kh = divmod(u, N_KV_HEADS)
        slot = u % 2
        c0 = i * ATTN_BLOCK
        vt = jnp.concatenate([jnp.concatenate(vt_band(i, kh), axis=1), ones_rows],
                             axis=0)
        ot = _dot(vt, p_ref[slot, :, half * HALF_Q:(half + 1) * HALF_Q])
        parts = []
        for g in range(HEADS_PER_HALF):
            cq = slice(g * ATTN_BLOCK, (g + 1) * ATTN_BLOCK)
            l = ot[HEAD_DIM:HEAD_DIM + 1, cq] + sink_share[g]
            parts.append(ot[0:HEAD_DIM, cq] * (1.0 / l))
        yt = jnp.concatenate(parts, axis=0)
        f0 = (kh * GQA_GROUP + half * HEADS_PER_HALF) * HEAD_DIM
        cs = slice(f0, f0 + HEADS_PER_HALF * HEAD_DIM)
        y = yt.T * sg_ref[c0:c0 + ATTN_BLOCK, cs]
        y_ref[c0:c0 + ATTN_BLOCK, cs] = y.astype(BF16)

    def gate_chunk(kh):
        cs = slice(kh * GQA_GROUP * HEAD_DIM, (kh + 1) * GQA_GROUP * HEAD_DIM)
        sg_ref[:, cs] = _silu(_dot(h_ref[0], wgate_ref[:, cs]))

    def finish_rows(r0, n):
        rows = slice(r0, r0 + n)
        out = x_ref[0, rows, :] + _dot(y_ref[rows, :], wout_ref[...])
        o_ref[0, rows, :] = _rms(out, fn_ref[...])

    units = nblk * N_KV_HEADS
    inv = {}
    for step in range(units + 2):
        sc = step if step < units else None
        sm = step - 1 if 1 <= step <= units else None
        wt = step - 2 if step >= 2 else None
        if step < N_KV_HEADS:
            gate_chunk(step)
        for half in range(2):
            if sc is not None:
                scores(sc, half)
            if sm is not None:
                inv[sm, 2 * half] = softmax_head(sm, 2 * half)
            if wt is not None:
                weighted(wt, half, [inv.pop((wt, half * HEADS_PER_HALF + g))
                                    for g in range(HEADS_PER_HALF)])
            if sm is not None:
                inv[sm, 2 * half + 1] = softmax_head(sm, 2 * half + 1)
        if wt is not None and (wt + 1) % (units // 2) == 0:
            finish_rows((wt + 1 - units // 2) // N_KV_HEADS * ATTN_BLOCK, tile // 2)


def _attention(x, h, qt, ka, vt, w_gate, sink_1, w_out_1, final_norm):
    bn, seq, d = x.shape
    tile = TILE_ATTN
    assert tile == TILE_L0
    nt = seq // tile
    per = tile // ATTN_BLOCK
    nb = seq // ATTN_BLOCK

    def const(shape):
        return pl.BlockSpec(shape, lambda b, j: (0,) * len(shape), pipeline_mode=pl.Buffered(1))

    def prev_blk(j):
        return jnp.maximum(j * per - 1, 0)

    def next_blk(j):
        return jnp.minimum((j + 1) * per, nb - 1)

    kern = functools.partial(_attn_kernel, tile=tile, seq=seq)
    return pl.pallas_call(
        kern,
        grid=(bn, nt),
        in_specs=[
            pl.BlockSpec(memory_space=pltpu.SMEM),
            pl.BlockSpec((1, tile, d), lambda b, j: (b, j, 0)),
            pl.BlockSpec((1, tile, d), lambda b, j: (b, j, 0)),
            pl.BlockSpec((1, 1, C_WIDTH, tile), lambda b, j: (b, j, 0, 0)),
            pl.BlockSpec((1, ATTN_BLOCK, KV_WIDTH), lambda b, j: (b, prev_blk(j), 0)),
            pl.BlockSpec((1, tile, KV_WIDTH), lambda b, j: (b, j, 0)),
            pl.BlockSpec((1, ATTN_BLOCK, KV_WIDTH), lambda b, j: (b, next_blk(j), 0)),
            pl.BlockSpec((1, 1, KV_WIDTH, ATTN_BLOCK),
                         lambda b, j: (b, jnp.maximum(j - 1, 0), 0, per - 1)),
            pl.BlockSpec((1, 1, KV_WIDTH, tile), lambda b, j: (b, j, 0, 0)),
            pl.BlockSpec((1, 1, KV_WIDTH, ATTN_BLOCK),
                         lambda b, j: (b, jnp.minimum(j + 1, nt - 1), 0, 0)),
            const(w_gate.shape),
            const(w_out_1.shape),
            const((1, d)),
        ],
        out_specs=pl.BlockSpec((1, tile, d), lambda b, j: (b, j, 0)),
        out_shape=jax.ShapeDtypeStruct(x.shape, F32),
        scratch_shapes=[
            pltpu.VMEM((2, BAND, GROUP_Q), F32),
            pltpu.VMEM((2, BAND, GROUP_Q), BF16),
            pltpu.VMEM((tile, C_WIDTH), F32),
            pltpu.VMEM((tile, C_WIDTH), BF16),
        ],
        compiler_params=pltpu.CompilerParams(
            dimension_semantics=("arbitrary", "arbitrary"), vmem_limit_bytes=VMEM_LIMIT),
        name="layer1_attention",
    )(sink_1, x, h, qt, ka, ka, ka, vt, vt, vt, w_gate, w_out_1, final_norm.reshape(1, d))


def kernel(x, norm_0, w_in_0, a_v_norm_0, a_spatial_w_0, a_spatial_b_0, b_group_w_0, b_scale_0,
           w_out_0, norm_1, w_in_1, sink_1, w_out_1, final_norm):
    x1, w_gate, w_out_1b, h1, qt, ka, vt = _layer0(
        x, norm_0, w_in_0, a_v_norm_0, a_spatial_w_0, a_spatial_b_0, b_group_w_0, b_scale_0,
        w_out_0, norm_1, w_in_1, w_out_1)
    return _attention(x1, h1, qt, ka, vt, w_gate, sink_1, w_out_1b, final_norm)
```

```python
import functools
import math

import jax
import jax.numpy as jnp
import numpy as np
from jax import lax
from jax.experimental import pallas as pl
from jax.experimental.pallas import tpu as pltpu

F32 = jnp.float32
BF16 = jnp.bfloat16

D_MODEL = 1024
EPS = 1e-6
NEG_INF = -1e30
LOG2_E = math.log2(math.e)
CHUNK = 128
A_GROUPS = 4
A_WIDTH = D_MODEL
A_GROUP_DIM = A_WIDTH // A_GROUPS
B_WIDTH = D_MODEL
POOL_WINDOWS = (2, 4, 8, 16)
B_GROUP_DIM = B_WIDTH // len(POOL_WINDOWS)
POOL_HALO = max(POOL_WINDOWS) // 2
N_HEADS = 16
N_KV_HEADS = 4
HEAD_DIM = 64
GQA_GROUP = N_HEADS // N_KV_HEADS
C_WIDTH = N_HEADS * HEAD_DIM
KV_WIDTH = N_KV_HEADS * HEAD_DIM
WINDOW = 128
ATTN_BLOCK = 128
BAND = 3 * ATTN_BLOCK
ROPE_THETA = 500000.0
ROT_DIM = HEAD_DIM // 4
LANES = 128
assert WINDOW == ATTN_BLOCK
GROUP_Q = GQA_GROUP * ATTN_BLOCK
HEADS_PER_HALF = GQA_GROUP // 2
HALF_Q = HEADS_PER_HALF * ATTN_BLOCK
SUM_ROWS = 32

QKV_COLS = C_WIDTH + 2 * KV_WIDTH
W1_CHUNK_ROWS = 128
W1_CHUNKS = D_MODEL // W1_CHUNK_ROWS
W0_STAGE_CHUNK = (1024, 512)
W0_STAGE_SLOTS = 6

TILE_L0 = 512
TILE_ATTN = 512
VMEM_LIMIT = 56 * 1024 * 1024


def _rms(x, g):
    return x * lax.rsqrt(jnp.mean(x * x, axis=-1, keepdims=True) + EPS) * g


def _silu(x):
    hx = 0.5 * x
    return hx + hx * jnp.tanh(hx)


def _dot(a, b):
    return jnp.dot(a, b, preferred_element_type=F32)


def _dot_nt(a, b):
    return lax.dot_general(a, b, (((1,), (1,)), ((), ())), preferred_element_type=F32)


def _layer0_kernel(x_ref, xp_ref, xn_ref, n0_ref, win_hbm, gv_ref, ws_ref, bsb_ref, wg_ref,
                   sc_ref, wout_hbm, w1_ref, wo1_ref, n1_ref, w1_hbm, cost_ref, sint_ref,
                   o_ref, wgate_ref, wo1b_ref, h1_ref, qt_ref, ka_ref, vt_ref,
                   hext_ref, v_ref, cat_ref, wqkvt_ref, win_ref, wout_ref, stage_ref, wsem,
                   *, tile, seq):
    j = pl.program_id(1)
    nj = pl.num_programs(1)
    step = pl.program_id(0) * nj + j

    @pl.when(step == 0)
    def _():
        slots, cr, cc = stage_ref.shape
        chunks = [(win_hbm, win_ref, r0, c0, False)
                  for r0 in range(0, win_ref.shape[0], cr) for c0 in range(0, win_ref.shape[1], cc)]
        chunks += [(wout_hbm, wout_ref, r0, c0, False)
                   for r0 in range(0, wout_ref.shape[0], cr)
                   for c0 in range(0, wout_ref.shape[1], cc)]
        chunks += [(w1_hbm, wqkvt_ref, r0, c0, True)
                   for r0 in range(0, wqkvt_ref.shape[1], cr)
                   for c0 in range(0, wqkvt_ref.shape[0], cc)]

        def chunk_copy(k):
            src, _, r0, c0, _ = chunks[k]
            return pltpu.make_async_copy(src.at[r0:r0 + cr, c0:c0 + cc], stage_ref.at[k % slots],
                                         wsem.at[k % slots])

        for k in range(min(slots - 1, len(chunks))):
            chunk_copy(k).start()
        for k, (_, dst, r0, c0, transposed) in enumerate(chunks):
            ahead = k + slots - 1
            if ahead < len(chunks):
                chunk_copy(ahead).start()
            chunk_copy(k).wait()
            if transposed:
                dst[c0:c0 + cc, r0:r0 + cr] = stage_ref[k % slots].T.astype(BF16)
            else:
                dst[r0:r0 + cr, c0:c0 + cc] = stage_ref[k % slots].astype(BF16)

    @pl.when(step < W1_CHUNKS)
    def _():
        wgate_ref[...] = w1_ref[:, QKV_COLS:].astype(BF16)
        wo1b_ref[...] = wo1_ref[...].astype(BF16)

    n0 = n0_ref[...]
    x = x_ref[0]
    h = _rms(x, n0).astype(BF16)
    hext_ref[0:tile, :] = h
    hp = jnp.where(j > 0, _rms(xp_ref[0], n0), 0.0)
    hn = jnp.where(j < nj - 1, _rms(xn_ref[0], n0), 0.0)
    hext_ref[tile:tile + 2 * POOL_HALO, :] = jnp.concatenate([hp, hn], axis=0).astype(BF16)

    def col(k):
        return k * D_MODEL

    a_v = jax.nn.gelu(_dot(h, win_ref[:, col(1):col(2)]))
    v_ref[...] = _rms(a_v, gv_ref[...]).astype(BF16)
    for g in range(A_GROUPS):
        c0 = g * A_GROUP_DIM
        c1 = c0 + A_GROUP_DIM
        u = jax.nn.gelu(_dot(h, win_ref[:, col(0) + c0:col(0) + c1]))
        gate = _silu(_dot(h, win_ref[:, col(2) + c0:col(2) + c1]))
        ws_g = ws_ref[g].astype(BF16)
        for c in range(tile // CHUNK):
            r0 = c * CHUNK
            r1 = r0 + CHUNK
            mixed = _dot(ws_g, v_ref[r0:r1, c0:c1]) + bsb_ref[:, c0:c1]
            cat_ref[r0:r1, c0:c1] = (u[r0:r1] * mixed * gate[r0:r1]).astype(BF16)

    bxe = _dot(hext_ref[...], win_ref[:, col(3):col(4)])
    n_ext = tile + 2 * POOL_HALO
    edge_row = lax.broadcasted_iota(jnp.int32, (POOL_HALO, 1), 0)
    for gi, w in enumerate(POOL_WINDOWS):
        c0 = gi * B_GROUP_DIM
        c1 = c0 + B_GROUP_DIM
        half = w // 2
        e = jnp.concatenate([bxe[tile:tile + POOL_HALO, c0:c1], bxe[0:tile, c0:c1],
                             bxe[tile + POOL_HALO:, c0:c1]], axis=0)
        a = e
        span = 1
        while span < half:
            a = a + pltpu.roll(a, n_ext - span, 0)
            span *= 2
        win_sum = (a + pltpu.roll(a, half, 0))[POOL_HALO:POOL_HALO + tile]

        def edge_mean(slab, row0, half=half):
            t = j * tile + row0 + edge_row
            cnt = (jnp.minimum(t + half, seq) - jnp.maximum(t - half, 0)).astype(F32)
            return slab / cnt

        mean = jnp.concatenate(
            [edge_mean(win_sum[0:POOL_HALO], 0),
             win_sum[POOL_HALO:tile - POOL_HALO] * (1.0 / w),
             edge_mean(win_sum[tile - POOL_HALO:], tile - POOL_HALO)], axis=0)
        p = (mean - bxe[0:tile, c0:c1]).astype(BF16)
        gate = _silu(_dot(h, win_ref[:, col(4) + c0:col(4) + c1]))
        yb = _dot(p, wg_ref[gi].astype(BF16)) * sc_ref[:, c0:c1] * gate
        cat_ref[:, A_WIDTH + c0:A_WIDTH + c1] = yb.astype(BF16)

    n1 = n1_ref[...]
    hrows = tile
    for r0 in range(0, tile, hrows):
        rows = slice(r0, r0 + hrows)
        out = x_ref[0, rows, :] + _dot(cat_ref[rows, :], wout_ref[...])
        o_ref[0, rows, :] = out
        h1_ref[0, rows, :] = _rms(out, n1).astype(BF16)
    _qkv_project(h1_ref[0], wqkvt_ref, cost_ref[...], sint_ref[...], qt_ref, ka_ref, vt_ref,
                 slice(0, tile))


def _layer0(x, norm_0, w_in_0, a_v_norm_0, a_spatial_w_0, a_spatial_b_0, b_group_w_0, b_scale_0,
            w_out_0, norm_1, w_in_1, w_out_1):
    bn, seq, d = x.shape
    tile = TILE_L0
    nt = seq // tile
    assert bn * nt >= W1_CHUNKS

    def w1_chunk(b, j):
        return jnp.minimum(b * nt + j, W1_CHUNKS - 1)
    halo_blocks = seq // POOL_HALO
    ws = a_spatial_w_0
    wg = b_group_w_0
    cos_fm, sin_fm = _rope_tables(seq)
    bsb = jnp.repeat(a_spatial_b_0.T, A_GROUP_DIM, axis=1)

    def const(shape):
        return pl.BlockSpec(shape, lambda b, j: (0,) * len(shape), pipeline_mode=pl.Buffered(1))

    per_halo = tile // POOL_HALO
    kern = functools.partial(_layer0_kernel, tile=tile, seq=seq)
    return pl.pallas_call(
        kern,
        grid=(bn, nt),
        in_specs=[
            pl.BlockSpec((1, tile, d), lambda b, j: (b, j, 0)),
            pl.BlockSpec((1, POOL_HALO, d), lambda b, j: (b, jnp.maximum(j * per_halo - 1, 0), 0)),
            pl.BlockSpec((1, POOL_HALO, d),
                         lambda b, j: (b, jnp.minimum((j + 1) * per_halo, halo_blocks - 1), 0)),
            const((1, d)),
            pl.BlockSpec(memory_space=pl.ANY),
            const((1, A_WIDTH)),
            const(ws.shape),
            const(bsb.shape),
            const(wg.shape),
            const((1, B_WIDTH)),
            pl.BlockSpec(memory_space=pl.ANY),
            pl.BlockSpec((W1_CHUNK_ROWS, w_in_1.shape[1]), lambda b, j: (w1_chunk(b, j), 0)),
            pl.BlockSpec((W1_CHUNK_ROWS, d), lambda b, j: (w1_chunk(b, j), 0)),
            const((1, d)),
            pl.BlockSpec(memory_space=pl.ANY),
            pl.BlockSpec((ROT_DIM // 2, tile), lambda b, j: (0, j)),
            pl.BlockSpec((ROT_DIM // 2, tile), lambda b, j: (0, j)),
        ],
        out_specs=[
            pl.BlockSpec((1, tile, d), lambda b, j: (b, j, 0)),
            pl.BlockSpec((W1_CHUNK_ROWS, C_WIDTH), lambda b, j: (w1_chunk(b, j), 0)),
            pl.BlockSpec((W1_CHUNK_ROWS, d), lambda b, j: (w1_chunk(b, j), 0)),
            pl.BlockSpec((1, tile, d), lambda b, j: (b, j, 0)),
            pl.BlockSpec((1, 1, C_WIDTH, tile), lambda b, j: (b, j, 0, 0)),
            pl.BlockSpec((1, tile, KV_WIDTH), lambda b, j: (b, j, 0)),
            pl.BlockSpec((1, 1, KV_WIDTH, tile), lambda b, j: (b, j, 0, 0)),
        ],
        out_shape=[
            jax.ShapeDtypeStruct(x.shape, F32),
            jax.ShapeDtypeStruct((d, C_WIDTH), BF16),
            jax.ShapeDtypeStruct((C_WIDTH, d), BF16),
            jax.ShapeDtypeStruct((bn, seq, d), BF16),
            jax.ShapeDtypeStruct((bn, nt, C_WIDTH, tile), BF16),
            jax.ShapeDtypeStruct((bn, seq, KV_WIDTH), BF16),
            jax.ShapeDtypeStruct((bn, nt, KV_WIDTH, tile), BF16),
        ],
        scratch_shapes=[
            pltpu.VMEM((tile + 2 * POOL_HALO, d), BF16),
            pltpu.VMEM((tile, A_WIDTH), BF16),
            pltpu.VMEM((tile, A_WIDTH + B_WIDTH), BF16),
            pltpu.VMEM((QKV_COLS, d), BF16),
            pltpu.VMEM(w_in_0.shape, BF16),
            pltpu.VMEM(w_out_0.shape, BF16),
            pltpu.VMEM((W0_STAGE_SLOTS,) + W0_STAGE_CHUNK, F32),
            pltpu.SemaphoreType.DMA((W0_STAGE_SLOTS,)),
        ],
        compiler_params=pltpu.CompilerParams(
            dimension_semantics=("arbitrary", "arbitrary"), vmem_limit_bytes=VMEM_LIMIT),
        name="layer0_mixer",
    )(x, x, x, norm_0.reshape(1, d), w_in_0, a_v_norm_0.reshape(1, A_WIDTH), ws, bsb, wg,
      b_scale_0.reshape(1, B_WIDTH), w_out_0, w_in_1, w_out_1, norm_1.reshape(1, d), w_in_1,
      cos_fm, sin_fm)


def _qkv_project(h, w_ref, ct, st, qt_ref, ka_ref, vt_ref, pos):
    half = ROT_DIM // 2
    n = h.shape[0]
    scale = HEAD_DIM ** -0.5 * LOG2_E

    qkv = _dot_nt(w_ref[...], h)

    def rope_rows(r0):
        x1 = qkv[r0:r0 + half]
        x2 = qkv[r0 + half:r0 + ROT_DIM]
        return jnp.concatenate([x1 * ct - x2 * st, x2 * ct + x1 * st], axis=0)

    for hd in range(N_HEADS):
        r0 = hd * HEAD_DIM
        qt_ref[0, 0, r0:r0 + ROT_DIM, pos] = (rope_rows(r0) * scale).astype(BF16)
        qt_ref[0, 0, r0 + ROT_DIM:r0 + HEAD_DIM, pos] = (
            qkv[r0 + ROT_DIM:r0 + HEAD_DIM] * scale).astype(BF16)
    vt_ref[0, 0, :, pos] = qkv[C_WIDTH + KV_WIDTH:].astype(BF16)

    for kp in range(N_KV_HEADS // 2):
        rows = []
        for kh in (2 * kp, 2 * kp + 1):
            r0 = C_WIDTH + kh * HEAD_DIM
            rows += [rope_rows(r0), qkv[r0 + ROT_DIM:r0 + HEAD_DIM]]
        kt = jnp.concatenate(rows, axis=0)
        ka_ref[0, pos, kp * LANES:(kp + 1) * LANES] = kt.T.astype(BF16)


def _rope_tables(seq):
    inv = ROPE_THETA ** (-np.arange(0, ROT_DIM, 2, dtype=np.float64) / ROT_DIM)
    ang = np.arange(seq, dtype=np.float64)[None, :] * inv[:, None]
    return jnp.asarray(np.cos(ang), F32), jnp.asarray(np.sin(ang), F32)


def _attn_kernel(sink_ref, x_ref, h_ref, qt_ref, kap_ref, kat_ref, kan_ref, vtp_ref, vtt_ref,
                 vtn_ref, wgate_ref, wout_ref, fn_ref, o_ref,
                 s_ref, p_ref, sg_ref, y_ref, *, tile, seq):
    j = pl.program_id(1)
    nblk = tile // ATTN_BLOCK
    last_blk = seq // ATTN_BLOCK - 1

    def k_band(i, kh):
        cols = slice(kh // 2 * LANES, (kh // 2 + 1) * LANES)
        lo = max(i - 1, 0) * ATTN_BLOCK
        hi = min(i + 2, nblk) * ATTN_BLOCK
        parts = [kat_ref[0, lo:hi, cols]]
        if i == 0:
            parts.insert(0, kap_ref[0, :, cols])
        if i == nblk - 1:
            parts.append(kan_ref[0, :, cols])
        return jnp.concatenate(parts, axis=0)

    def vt_band(i, kh):
        rows = slice(kh * HEAD_DIM, (kh + 1) * HEAD_DIM)
        lo = max(i - 1, 0) * ATTN_BLOCK
        hi = min(i + 2, nblk) * ATTN_BLOCK
        parts = [vtt_ref[0, 0, rows, lo:hi]]
        if i == 0:
            parts.insert(0, vtp_ref[0, 0, rows, :])
        if i == nblk - 1:
            parts.append(vtn_ref[0, 0, rows, :])
        return parts
    diff = (lax.broadcasted_iota(jnp.int32, (ATTN_BLOCK, ATTN_BLOCK), 0)
            - lax.broadcasted_iota(jnp.int32, (ATTN_BLOCK, ATTN_BLOCK), 1))
    never = 2 * ATTN_BLOCK
    ones_rows = jnp.ones((SUM_ROWS, BAND), BF16)

    def scores(u, half):
        i, kh = divmod(u, N_KV_HEADS)
        slot = u % 2
        c0 = i * ATTN_BLOCK
        h0 = kh * GQA_GROUP + half * HEADS_PER_HALF
        qg = jnp.concatenate(
            [qt_ref[0, 0, (h0 + g) * HEAD_DIM:(h0 + g + 1) * HEAD_DIM, c0:c0 + ATTN_BLOCK]
             for g in range(HEADS_PER_HALF)], axis=1)
        zq = jnp.zeros((LANES - HEAD_DIM, HALF_Q), BF16)
        rq = jnp.concatenate([qg, zq] if kh % 2 == 0 else [zq, qg], axis=0)
        ka = k_band(i, kh)
        s_ref[slot, :, half * HALF_Q:(half + 1) * HALF_Q] = _dot(ka, rq)

    def softmax_head(u, g):
        i, kh = divmod(u, N_KV_HEADS)
        slot = u % 2
        gblk = j * nblk + i
        ok_prev = diff >= jnp.where(gblk > 0, 0, never)
        ok_next = diff <= jnp.where(gblk < last_blk, 0, -never)
        cs = slice(g * ATTN_BLOCK, (g + 1) * ATTN_BLOCK)
        sink = sink_ref[kh * GQA_GROUP + g] * LOG2_E
        s0 = jnp.where(ok_prev, s_ref[slot, 0:ATTN_BLOCK, cs], NEG_INF)
        s1 = s_ref[slot, ATTN_BLOCK:2 * ATTN_BLOCK, cs]
        s2 = jnp.where(ok_next, s_ref[slot, 2 * ATTN_BLOCK:, cs], NEG_INF)
        m = jnp.max(jnp.maximum(jnp.maximum(s0, s1), s2), axis=0, keepdims=True)
        m = jnp.maximum(m, sink)
        p0 = jnp.exp2(s0 - m)
        p1 = jnp.exp2(s1 - m)
        p2 = jnp.exp2(s2 - m)
        p_ref[slot, 0:ATTN_BLOCK, cs] = p0.astype(BF16)
        p_ref[slot, ATTN_BLOCK:2 * ATTN_BLOCK, cs] = p1.astype(BF16)
        p_ref[slot, 2 * ATTN_BLOCK:, cs] = p2.astype(BF16)
        return jnp.exp2(sink - m)

    def weighted(u, half, sink_share):
        i, kh = divmod(u, N_KV_HEADS)
        slot = u % 2
        c0 = i * ATTN_BLOCK
        vt = jnp.concatenate([jnp.concatenate(vt_band(i, kh), axis=1), ones_rows],
                             axis=0)
        ot = _dot(vt, p_ref[slot, :, half * HALF_Q:(half + 1) * HALF_Q])
        parts = []
        for g in range(HEADS_PER_HALF):
            cq = slice(g * ATTN_BLOCK, (g + 1) * ATTN_BLOCK)
            l = ot[HEAD_DIM:HEAD_DIM + 1, cq] + sink_share[g]
            parts.append(ot[0:HEAD_DIM, cq] * (1.0 / l))
        yt = jnp.concatenate(parts, axis=0)
        f0 = (kh * GQA_GROUP + half * HEADS_PER_HALF) * HEAD_DIM
        cs = slice(f0, f0 + HEADS_PER_HALF * HEAD_DIM)
        y = yt.T * sg_ref[c0:c0 + ATTN_BLOCK, cs]
        y_ref[c0:c0 + ATTN_BLOCK, cs] = y.astype(BF16)

    def gate_chunk(kh):
        cs = slice(kh * GQA_GROUP * HEAD_DIM, (kh + 1) * GQA_GROUP * HEAD_DIM)
        sg_ref[:, cs] = _silu(_dot(h_ref[0], wgate_ref[:, cs]))

    def finish_rows(r0, n):
        rows = slice(r0, r0 + n)
        out = x_ref[0, rows, :] + _dot(y_ref[rows, :], wout_ref[...])
        o_ref[0, rows, :] = _rms(out, fn_ref[...])

    units = nblk * N_KV_HEADS
    inv = {}
    for step in range(units + 2):
        sc = step if step < units else None
        sm = step - 1 if 1 <= step <= units else None
        wt = step - 2 if step >= 2 else None
        if step < N_KV_HEADS:
            gate_chunk(step)
        for half in range(2):
            if sc is not None:
                scores(sc, half)
            if sm is not None:
                inv[sm, 2 * half] = softmax_head(sm, 2 * half)
            if wt is not None:
                weighted(wt, half, [inv.pop((wt, half * HEADS_PER_HALF + g))
                                    for g in range(HEADS_PER_HALF)])
            if sm is not None:
                inv[sm, 2 * half + 1] = softmax_head(sm, 2 * half + 1)
        if wt is not None and (wt + 1) % (units // 2) == 0:
            finish_rows((wt + 1 - units // 2) // N_KV_HEADS * ATTN_BLOCK, tile // 2)


def _attention(x, h, qt, ka, vt, w_gate, sink_1, w_out_1, final_norm):
    bn, seq, d = x.shape
    tile = TILE_ATTN
    assert tile == TILE_L0
    nt = seq // tile
    per = tile // ATTN_BLOCK
    nb = seq // ATTN_BLOCK

    def const(shape):
        return pl.BlockSpec(shape, lambda b, j: (0,) * len(shape), pipeline_mode=pl.Buffered(1))

    def prev_blk(j):
        return jnp.maximum(j * per - 1, 0)

    def next_blk(j):
        return jnp.minimum((j + 1) * per, nb - 1)

    kern = functools.partial(_attn_kernel, tile=tile, seq=seq)
    return pl.pallas_call(
        kern,
        grid=(bn, nt),
        in_specs=[
            pl.BlockSpec(memory_space=pltpu.SMEM),
            pl.BlockSpec((1, tile, d), lambda b, j: (b, j, 0)),
            pl.BlockSpec((1, tile, d), lambda b, j: (b, j, 0)),
            pl.BlockSpec((1, 1, C_WIDTH, tile), lambda b, j: (b, j, 0, 0)),
            pl.BlockSpec((1, ATTN_BLOCK, KV_WIDTH), lambda b, j: (b, prev_blk(j), 0)),
            pl.BlockSpec((1, tile, KV_WIDTH), lambda b, j: (b, j, 0)),
            pl.BlockSpec((1, ATTN_BLOCK, KV_WIDTH), lambda b, j: (b, next_blk(j), 0)),
            pl.BlockSpec((1, 1, KV_WIDTH, ATTN_BLOCK),
                         lambda b, j: (b, jnp.maximum(j - 1, 0), 0, per - 1)),
            pl.BlockSpec((1, 1, KV_WIDTH, tile), lambda b, j: (b, j, 0, 0)),
            pl.BlockSpec((1, 1, KV_WIDTH, ATTN_BLOCK),
                         lambda b, j: (b, jnp.minimum(j + 1, nt - 1), 0, 0)),
            const(w_gate.shape),
            const(w_out_1.shape),
            const((1, d)),
        ],
        out_specs=pl.BlockSpec((1, tile, d), lambda b, j: (b, j, 0)),
        out_shape=jax.ShapeDtypeStruct(x.shape, F32),
        scratch_shapes=[
            pltpu.VMEM((2, BAND, GROUP_Q), F32),
            pltpu.VMEM((2, BAND, GROUP_Q), BF16),
            pltpu.VMEM((tile, C_WIDTH), F32),
            pltpu.VMEM((tile, C_WIDTH), BF16),
        ],
        compiler_params=pltpu.CompilerParams(
            dimension_semantics=("arbitrary", "arbitrary"), vmem_limit_bytes=VMEM_LIMIT),
        name="layer1_attention",
    )(sink_1, x, h, qt, ka, ka, ka, vt, vt, vt, w_gate, w_out_1, final_norm.reshape(1, d))


def kernel(x, norm_0, w_in_0, a_v_norm_0, a_spatial_w_0, a_spatial_b_0, b_group_w_0, b_scale_0,
           w_out_0, norm_1, w_in_1, sink_1, w_out_1, final_norm):
    x1, w_gate, w_out_1b, h1, qt, ka, vt = _layer0(
        x, norm_0, w_in_0, a_v_norm_0, a_spatial_w_0, a_spatial_b_0, b_group_w_0, b_scale_0,
        w_out_0, norm_1, w_in_1, w_out_1)
    return _attention(x1, h1, qt, ka, vt, w_gate, sink_1, w_out_1b, final_norm)
```

```python
import functools
import math

import jax
import jax.numpy as jnp
import numpy as np
from jax import lax
from jax.experimental import pallas as pl
from jax.experimental.pallas import tpu as pltpu

F32 = jnp.float32
BF16 = jnp.bfloat16

D_MODEL = 1024
EPS = 1e-6
NEG_INF = -1e30
LOG2_E = math.log2(math.e)
CHUNK = 128
A_GROUPS = 4
A_WIDTH = D_MODEL
A_GROUP_DIM = A_WIDTH // A_GROUPS
B_WIDTH = D_MODEL
POOL_WINDOWS = (2, 4, 8, 16)
B_GROUP_DIM = B_WIDTH // len(POOL_WINDOWS)
POOL_HALO = max(POOL_WINDOWS) // 2
N_HEADS = 16
N_KV_HEADS = 4
HEAD_DIM = 64
GQA_GROUP = N_HEADS // N_KV_HEADS
C_WIDTH = N_HEADS * HEAD_DIM
KV_WIDTH = N_KV_HEADS * HEAD_DIM
WINDOW = 128
ATTN_BLOCK = 128
BAND = 3 * ATTN_BLOCK
ROPE_THETA = 500000.0
ROT_DIM = HEAD_DIM // 4
LANES = 128
assert WINDOW == ATTN_BLOCK
GROUP_Q = GQA_GROUP * ATTN_BLOCK
HEADS_PER_HALF = GQA_GROUP // 2
HALF_Q = HEADS_PER_HALF * ATTN_BLOCK
SUM_ROWS = 32

QKV_COLS = C_WIDTH + 2 * KV_WIDTH
W1_CHUNK_ROWS = 128
W1_CHUNKS = D_MODEL // W1_CHUNK_ROWS
W0_STAGE_CHUNK = (1024, 512)
W0_STAGE_SLOTS = 6

TILE_L0 = 512
TILE_ATTN = 512
VMEM_LIMIT = 56 * 1024 * 1024


def _rms(x, g):
    return x * lax.rsqrt(jnp.mean(x * x, axis=-1, keepdims=True) + EPS) * g


def _silu(x):
    hx = 0.5 * x
    return hx + hx * jnp.tanh(hx)


def _dot(a, b):
    return jnp.dot(a, b, preferred_element_type=F32)


def _dot_nt(a, b):
    return lax.dot_general(a, b, (((1,), (1,)), ((), ())), preferred_element_type=F32)


def _layer0_kernel(x_ref, xp_ref, xn_ref, n0_ref, win_hbm, gv_ref, ws_ref, bsb_ref, wg_ref,
                   sc_ref, wout_hbm, w1_ref, wo1_ref, n1_ref, w1_hbm, cost_ref, sint_ref,
                   o_ref, wgate_ref, wo1b_ref, h1_ref, qt_ref, ka_ref, vt_ref,
                   hext_ref, v_ref, cat_ref, wqkvt_ref, win_ref, wout_ref, stage_ref, wsem,
                   *, tile, seq):
    j = pl.program_id(1)
    nj = pl.num_programs(1)
    step = pl.program_id(0) * nj + j

    @pl.when(step == 0)
    def _():
        slots, cr, cc = stage_ref.shape
        chunks = [(win_hbm, win_ref, r0, c0, False)
                  for r0 in range(0, win_ref.shape[0], cr) for c0 in range(0, win_ref.shape[1], cc)]
        chunks += [(wout_hbm, wout_ref, r0, c0, False)
                   for r0 in range(0, wout_ref.shape[0], cr)
                   for c0 in range(0, wout_ref.shape[1], cc)]
        chunks += [(w1_hbm, wqkvt_ref, r0, c0, True)
                   for r0 in range(0, wqkvt_ref.shape[1], cr)
                   for c0 in range(0, wqkvt_ref.shape[0], cc)]

        def chunk_copy(k):
            src, _, r0, c0, _ = chunks[k]
            return pltpu.make_async_copy(src.at[r0:r0 + cr, c0:c0 + cc], stage_ref.at[k % slots],
                                         wsem.at[k % slots])

        for k in range(min(slots - 1, len(chunks))):
            chunk_copy(k).start()
        for k, (_, dst, r0, c0, transposed) in enumerate(chunks):
            ahead = k + slots - 1
            if ahead < len(chunks):
                chunk_copy(ahead).start()
            chunk_copy(k).wait()
            if transposed:
                dst[c0:c0 + cc, r0:r0 + cr] = stage_ref[k % slots].T.astype(BF16)
            else:
                dst[r0:r0 + cr, c0:c0 + cc] = stage_ref[k % slots].astype(BF16)

    @pl.when(step < W1_CHUNKS)
    def _():
        wgate_ref[...] = w1_ref[:, QKV_COLS:].astype(BF16)
        wo1b_ref[...] = wo1_ref[...].astype(BF16)

    n0 = n0_ref[...]
    x = x_ref[0]
    h = _rms(x, n0).astype(BF16)
    hext_ref[0:tile, :] = h
    hp = jnp.where(j > 0, _rms(xp_ref[0], n0), 0.0)
    hn = jnp.where(j < nj - 1, _rms(xn_ref[0], n0), 0.0)
    hext_ref[tile:tile + 2 * POOL_HALO, :] = jnp.concatenate([hp, hn], axis=0).astype(BF16)

    def col(k):
        return k * D_MODEL

    a_v = jax.nn.gelu(_dot(h, win_ref[:, col(1):col(2)]))
    v_ref[...] = _rms(a_v, gv_ref[...]).astype(BF16)
    for g in range(A_GROUPS):
        c0 = g * A_GROUP_DIM
        c1 = c0 + A_GROUP_DIM
        u = jax.nn.gelu(_dot(h, win_ref[:, col(0) + c0:col(0) + c1]))
        gate = _silu(_dot(h, win_ref[:, col(2) + c0:col(2) + c1]))
        ws_g = ws_ref[g].astype(BF16)
        for c in range(tile // CHUNK):
            r0 = c * CHUNK
            r1 = r0 + CHUNK
            mixed = _dot(ws_g, v_ref[r0:r1, c0:c1]) + bsb_ref[:, c0:c1]
            cat_ref[r0:r1, c0:c1] = (u[r0:r1] * mixed * gate[r0:r1]).astype(BF16)

    bxe = _dot(hext_ref[...], win_ref[:, col(3):col(4)])
    n_ext = tile + 2 * POOL_HALO
    edge_row = lax.broadcasted_iota(jnp.int32, (POOL_HALO, 1), 0)
    for gi, w in enumerate(POOL_WINDOWS):
        c0 = gi * B_GROUP_DIM
        c1 = c0 + B_GROUP_DIM
        half = w // 2
        e = jnp.concatenate([bxe[tile:tile + POOL_HALO, c0:c1], bxe[0:tile, c0:c1],
                             bxe[tile + POOL_HALO:, c0:c1]], axis=0)
        a = e
        span = 1
        while span < half:
            a = a + pltpu.roll(a, n_ext - span, 0)
            span *= 2
        win_sum = (a + pltpu.roll(a, half, 0))[POOL_HALO:POOL_HALO + tile]

        def edge_mean(slab, row0, half=half):
            t = j * tile + row0 + edge_row
            cnt = (jnp.minimum(t + half, seq) - jnp.maximum(t - half, 0)).astype(F32)
            return slab / cnt

        mean = jnp.concatenate(
            [edge_mean(win_sum[0:POOL_HALO], 0),
             win_sum[POOL_HALO:tile - POOL_HALO] * (1.0 / w),
             edge_mean(win_sum[tile - POOL_HALO:], tile - POOL_HALO)], axis=0)
        p = (mean - bxe[0:tile, c0:c1]).astype(BF16)
        gate = _silu(_dot(h, win_ref[:, col(4) + c0:col(4) + c1]))
        yb = _dot(p, wg_ref[gi].astype(BF16)) * sc_ref[:, c0:c1] * gate
        cat_ref[:, A_WIDTH + c0:A_WIDTH + c1] = yb.astype(BF16)

    out = x_ref[0] + _dot(cat_ref[...], wout_ref[...])
    o_ref[0] = out
    h1 = _rms(out, n1_ref[...]).astype(BF16)
    h1_ref[0] = h1
    _qkv_project(h1, wqkvt_ref, cost_ref[...], sint_ref[...], qt_ref, ka_ref, vt_ref)


def _layer0(x, norm_0, w_in_0, a_v_norm_0, a_spatial_w_0, a_spatial_b_0, b_group_w_0, b_scale_0,
            w_out_0, norm_1, w_in_1, w_out_1):
    bn, seq, d = x.shape
    tile = TILE_L0
    nt = seq // tile
    assert bn * nt >= W1_CHUNKS

    def w1_chunk(b, j):
        return jnp.minimum(b * nt + j, W1_CHUNKS - 1)
    halo_blocks = seq // POOL_HALO
    ws = a_spatial_w_0
    wg = b_group_w_0
    cos_fm, sin_fm = _rope_tables(seq)
    bsb = jnp.repeat(a_spatial_b_0.T, A_GROUP_DIM, axis=1)

    def const(shape):
        return pl.BlockSpec(shape, lambda b, j: (0,) * len(shape), pipeline_mode=pl.Buffered(1))

    per_halo = tile // POOL_HALO
    kern = functools.partial(_layer0_kernel, tile=tile, seq=seq)
    return pl.pallas_call(
        kern,
        grid=(bn, nt),
        in_specs=[
            pl.BlockSpec((1, tile, d), lambda b, j: (b, j, 0)),
            pl.BlockSpec((1, POOL_HALO, d), lambda b, j: (b, jnp.maximum(j * per_halo - 1, 0), 0)),
            pl.BlockSpec((1, POOL_HALO, d),
                         lambda b, j: (b, jnp.minimum((j + 1) * per_halo, halo_blocks - 1), 0)),
            const((1, d)),
            pl.BlockSpec(memory_space=pl.ANY),
            const((1, A_WIDTH)),
            const(ws.shape),
            const(bsb.shape),
            const(wg.shape),
            const((1, B_WIDTH)),
            pl.BlockSpec(memory_space=pl.ANY),
            pl.BlockSpec((W1_CHUNK_ROWS, w_in_1.shape[1]), lambda b, j: (w1_chunk(b, j), 0)),
            pl.BlockSpec((W1_CHUNK_ROWS, d), lambda b, j: (w1_chunk(b, j), 0)),
            const((1, d)),
            pl.BlockSpec(memory_space=pl.ANY),
            pl.BlockSpec((ROT_DIM // 2, tile), lambda b, j: (0, j)),
            pl.BlockSpec((ROT_DIM // 2, tile), lambda b, j: (0, j)),
        ],
        out_specs=[
            pl.BlockSpec((1, tile, d), lambda b, j: (b, j, 0)),
            pl.BlockSpec((W1_CHUNK_ROWS, C_WIDTH), lambda b, j: (w1_chunk(b, j), 0)),
            pl.BlockSpec((W1_CHUNK_ROWS, d), lambda b, j: (w1_chunk(b, j), 0)),
            pl.BlockSpec((1, tile, d), lambda b, j: (b, j, 0)),
            pl.BlockSpec((1, 1, C_WIDTH, tile), lambda b, j: (b, j, 0, 0)),
            pl.BlockSpec((1, tile, KV_WIDTH), lambda b, j: (b, j, 0)),
            pl.BlockSpec((1, 1, KV_WIDTH, tile), lambda b, j: (b, j, 0, 0)),
        ],
        out_shape=[
            jax.ShapeDtypeStruct(x.shape, F32),
            jax.ShapeDtypeStruct((d, C_WIDTH), BF16),
            jax.ShapeDtypeStruct((C_WIDTH, d), BF16),
            jax.ShapeDtypeStruct((bn, seq, d), BF16),
            jax.ShapeDtypeStruct((bn, nt, C_WIDTH, tile), BF16),
            jax.ShapeDtypeStruct((bn, seq, KV_WIDTH), BF16),
            jax.ShapeDtypeStruct((bn, nt, KV_WIDTH, tile), BF16),
        ],
        scratch_shapes=[
            pltpu.VMEM((tile + 2 * POOL_HALO, d), BF16),
            pltpu.VMEM((tile, A_WIDTH), BF16),
            pltpu.VMEM((tile, A_WIDTH + B_WIDTH), BF16),
            pltpu.VMEM((QKV_COLS, d), BF16),
            pltpu.VMEM(w_in_0.shape, BF16),
            pltpu.VMEM(w_out_0.shape, BF16),
            pltpu.VMEM((W0_STAGE_SLOTS,) + W0_STAGE_CHUNK, F32),
            pltpu.SemaphoreType.DMA((W0_STAGE_SLOTS,)),
        ],
        compiler_params=pltpu.CompilerParams(
            dimension_semantics=("arbitrary", "arbitrary"), vmem_limit_bytes=VMEM_LIMIT),
        name="layer0_mixer",
    )(x, x, x, norm_0.reshape(1, d), w_in_0, a_v_norm_0.reshape(1, A_WIDTH), ws, bsb, wg,
      b_scale_0.reshape(1, B_WIDTH), w_out_0, w_in_1, w_out_1, norm_1.reshape(1, d), w_in_1,
      cos_fm, sin_fm)


def _qkv_project(h, w_ref, ct, st, qt_ref, ka_ref, vt_ref):
    half = ROT_DIM // 2
    n = h.shape[0]
    scale = HEAD_DIM ** -0.5 * LOG2_E

    qkv = _dot_nt(w_ref[...], h)

    def rope_rows(r0):
        x1 = qkv[r0:r0 + half]
        x2 = qkv[r0 + half:r0 + ROT_DIM]
        return jnp.concatenate([x1 * ct - x2 * st, x2 * ct + x1 * st], axis=0)

    for hd in range(N_HEADS):
        r0 = hd * HEAD_DIM
        qt_ref[0, 0, r0:r0 + ROT_DIM, :] = (rope_rows(r0) * scale).astype(BF16)
        qt_ref[0, 0, r0 + ROT_DIM:r0 + HEAD_DIM, :] = (
            qkv[r0 + ROT_DIM:r0 + HEAD_DIM] * scale).astype(BF16)
    vt_ref[0, 0] = qkv[C_WIDTH + KV_WIDTH:].astype(BF16)

    for kp in range(N_KV_HEADS // 2):
        rows = []
        for kh in (2 * kp, 2 * kp + 1):
            r0 = C_WIDTH + kh * HEAD_DIM
            rows += [rope_rows(r0), qkv[r0 + ROT_DIM:r0 + HEAD_DIM]]
        kt = jnp.concatenate(rows, axis=0)
        ka_ref[0, :, kp * LANES:(kp + 1) * LANES] = kt.T.astype(BF16)


def _rope_tables(seq):
    inv = ROPE_THETA ** (-np.arange(0, ROT_DIM, 2, dtype=np.float64) / ROT_DIM)
    ang = np.arange(seq, dtype=np.float64)[None, :] * inv[:, None]
    return jnp.asarray(np.cos(ang), F32), jnp.asarray(np.sin(ang), F32)


def _attn_kernel(sink_ref, x_ref, h_ref, qt_ref, kap_ref, kat_ref, kan_ref, vtp_ref, vtt_ref,
                 vtn_ref, wgate_ref, wout_ref, fn_ref, o_ref,
                 s_ref, p_ref, sg_ref, y_ref, *, tile, seq):
    j = pl.program_id(1)
    nblk = tile // ATTN_BLOCK
    last_blk = seq // ATTN_BLOCK - 1

    def k_band(i, kh):
        cols = slice(kh // 2 * LANES, (kh // 2 + 1) * LANES)
        lo = max(i - 1, 0) * ATTN_BLOCK
        hi = min(i + 2, nblk) * ATTN_BLOCK
        parts = [kat_ref[0, lo:hi, cols]]
        if i == 0:
            parts.insert(0, kap_ref[0, :, cols])
        if i == nblk - 1:
            parts.append(kan_ref[0, :, cols])
        return jnp.concatenate(parts, axis=0)

    def vt_band(i, kh):
        rows = slice(kh * HEAD_DIM, (kh + 1) * HEAD_DIM)
        lo = max(i - 1, 0) * ATTN_BLOCK
        hi = min(i + 2, nblk) * ATTN_BLOCK
        parts = [vtt_ref[0, 0, rows, lo:hi]]
        if i == 0:
            parts.insert(0, vtp_ref[0, 0, rows, :])
        if i == nblk - 1:
            parts.append(vtn_ref[0, 0, rows, :])
        return parts
    diff = (lax.broadcasted_iota(jnp.int32, (ATTN_BLOCK, ATTN_BLOCK), 0)
            - lax.broadcasted_iota(jnp.int32, (ATTN_BLOCK, ATTN_BLOCK), 1))
    never = 2 * ATTN_BLOCK
    ones_rows = jnp.ones((SUM_ROWS, BAND), BF16)

    def scores(u, half):
        i, kh = divmod(u, N_KV_HEADS)
        slot = u % 2
        c0 = i * ATTN_BLOCK
        h0 = kh * GQA_GROUP + half * HEADS_PER_HALF
        qg = jnp.concatenate(
            [qt_ref[0, 0, (h0 + g) * HEAD_DIM:(h0 + g + 1) * HEAD_DIM, c0:c0 + ATTN_BLOCK]
             for g in range(HEADS_PER_HALF)], axis=1)
        zq = jnp.zeros((LANES - HEAD_DIM, HALF_Q), BF16)
        rq = jnp.concatenate([qg, zq] if kh % 2 == 0 else [zq, qg], axis=0)
        ka = k_band(i, kh)
        s_ref[slot, :, half * HALF_Q:(half + 1) * HALF_Q] = _dot(ka, rq)

    def softmax_head(u, g):
        i, kh = divmod(u, N_KV_HEADS)
        slot = u % 2
        gblk = j * nblk + i
        ok_prev = diff >= jnp.where(gblk > 0, 0, never)
        ok_next = diff <= jnp.where(gblk < last_blk, 0, -never)
        cs = slice(g * ATTN_BLOCK, (g + 1) * ATTN_BLOCK)
        sink = sink_ref[kh * GQA_GROUP + g] * LOG2_E
        s0 = jnp.where(ok_prev, s_ref[slot, 0:ATTN_BLOCK, cs], NEG_INF)
        s1 = s_ref[slot, ATTN_BLOCK:2 * ATTN_BLOCK, cs]
        s2 = jnp.where(ok_next, s_ref[slot, 2 * ATTN_BLOCK:, cs], NEG_INF)
        m = jnp.max(jnp.maximum(jnp.maximum(s0, s1), s2), axis=0, keepdims=True)
        m = jnp.maximum(m, sink)
        p0 = jnp.exp2(s0 - m)
        p1 = jnp.exp2(s1 - m)
        p2 = jnp.exp2(s2 - m)
        p_ref[slot, 0:ATTN_BLOCK, cs] = p0.astype(BF16)
        p_ref[slot, ATTN_BLOCK:2 * ATTN_BLOCK, cs] = p1.astype(BF16)
        p_ref[slot, 2 * ATTN_BLOCK:, cs] = p2.astype(BF16)
        return jnp.exp2(sink - m)

    def weighted(u, half, sink_share):
        i, kh = divmod(u, N_KV_HEADS)
        slot = u % 2
        c0 = i * ATTN_BLOCK
        vt = jnp.concatenate([jnp.concatenate(vt_band(i, kh), axis=1), ones_rows],
                             axis=0)
        ot = _dot(vt, p_ref[slot, :, half * HALF_Q:(half + 1) * HALF_Q])
        parts = []
        for g in range(HEADS_PER_HALF):
            cq = slice(g * ATTN_BLOCK, (g + 1) * ATTN_BLOCK)
            l = ot[HEAD_DIM:HEAD_DIM + 1, cq] + sink_share[g]
            parts.append(ot[0:HEAD_DIM, cq] * (1.0 / l))
        yt = jnp.concatenate(parts, axis=0)
        f0 = (kh * GQA_GROUP + half * HEADS_PER_HALF) * HEAD_DIM
        cs = slice(f0, f0 + HEADS_PER_HALF * HEAD_DIM)
        y = yt.T * sg_ref[c0:c0 + ATTN_BLOCK, cs]
        y_ref[c0:c0 + ATTN_BLOCK, cs] = y.astype(BF16)

    def gate_chunk(kh):
        cs = slice(kh * GQA_GROUP * HEAD_DIM, (kh + 1) * GQA_GROUP * HEAD_DIM)
        sg_ref[:, cs] = _silu(_dot(h_ref[0], wgate_ref[:, cs]))

    def finish_rows(r0, n):
        rows = slice(r0, r0 + n)
        out = x_ref[0, rows, :] + _dot(y_ref[rows, :], wout_ref[...])
        o_ref[0, rows, :] = _rms(out, fn_ref[...])

    units = nblk * N_KV_HEADS
    inv = {}
    for step in range(units + 2):
        sc = step if step < units else None
        sm = step - 1 if 1 <= step <= units else None
        wt = step - 2 if step >= 2 else None
        if step < N_KV_HEADS:
            gate_chunk(step)
        for half in range(2):
            if sc is not None:
                scores(sc, half)
            if sm is not None:
                inv[sm, 2 * half] = softmax_head(sm, 2 * half)
            if wt is not None:
                weighted(wt, half, [inv.pop((wt, half * HEADS_PER_HALF + g))
                                    for g in range(HEADS_PER_HALF)])
            if sm is not None:
                inv[sm, 2 * half + 1] = softmax_head(sm, 2 * half + 1)
    finish_rows(0, tile)


def _attention(x, h, qt, ka, vt, w_gate, sink_1, w_out_1, final_norm):
    bn, seq, d = x.shape
    tile = TILE_ATTN
    assert tile == TILE_L0
    nt = seq // tile
    per = tile // ATTN_BLOCK
    nb = seq // ATTN_BLOCK

    def const(shape):
        return pl.BlockSpec(shape, lambda b, j: (0,) * len(shape), pipeline_mode=pl.Buffered(1))

    def prev_blk(j):
        return jnp.maximum(j * per - 1, 0)

    def next_blk(j):
        return jnp.minimum((j + 1) * per, nb - 1)

    kern = functools.partial(_attn_kernel, tile=tile, seq=seq)
    return pl.pallas_call(
        kern,
        grid=(bn, nt),
        in_specs=[
            pl.BlockSpec(memory_space=pltpu.SMEM),
            pl.BlockSpec((1, tile, d), lambda b, j: (b, j, 0)),
            pl.BlockSpec((1, tile, d), lambda b, j: (b, j, 0)),
            pl.BlockSpec((1, 1, C_WIDTH, tile), lambda b, j: (b, j, 0, 0)),
            pl.BlockSpec((1, ATTN_BLOCK, KV_WIDTH), lambda b, j: (b, prev_blk(j), 0)),
            pl.BlockSpec((1, tile, KV_WIDTH), lambda b, j: (b, j, 0)),
            pl.BlockSpec((1, ATTN_BLOCK, KV_WIDTH), lambda b, j: (b, next_blk(j), 0)),
            pl.BlockSpec((1, 1, KV_WIDTH, ATTN_BLOCK),
                         lambda b, j: (b, jnp.maximum(j - 1, 0), 0, per - 1)),
            pl.BlockSpec((1, 1, KV_WIDTH, tile), lambda b, j: (b, j, 0, 0)),
            pl.BlockSpec((1, 1, KV_WIDTH, ATTN_BLOCK),
                         lambda b, j: (b, jnp.minimum(j + 1, nt - 1), 0, 0)),
            const(w_gate.shape),
            const(w_out_1.shape),
            const((1, d)),
        ],
        out_specs=pl.BlockSpec((1, tile, d), lambda b, j: (b, j, 0)),
        out_shape=jax.ShapeDtypeStruct(x.shape, F32),
        scratch_shapes=[
            pltpu.VMEM((2, BAND, GROUP_Q), F32),
            pltpu.VMEM((2, BAND, GROUP_Q), BF16),
            pltpu.VMEM((tile, C_WIDTH), F32),
            pltpu.VMEM((tile, C_WIDTH), BF16),
        ],
        compiler_params=pltpu.CompilerParams(
            dimension_semantics=("arbitrary", "arbitrary"), vmem_limit_bytes=VMEM_LIMIT),
        name="layer1_attention",
    )(sink_1, x, h, qt, ka, ka, ka, vt, vt, vt, w_gate, w_out_1, final_norm.reshape(1, d))


def kernel(x, norm_0, w_in_0, a_v_norm_0, a_spatial_w_0, a_spatial_b_0, b_group_w_0, b_scale_0,
           w_out_0, norm_1, w_in_1, sink_1, w_out_1, final_norm):
    x1, w_gate, w_out_1b, h1, qt, ka, vt = _layer0(
        x, norm_0, w_in_0, a_v_norm_0, a_spatial_w_0, a_spatial_b_0, b_group_w_0, b_scale_0,
        w_out_0, norm_1, w_in_1, w_out_1)
    return _attention(x1, h1, qt, ka, vt, w_gate, sink_1, w_out_1b, final_norm)
```

```python
import functools
import math

import jax
import jax.numpy as jnp
import numpy as np
from jax import lax
from jax.experimental import pallas as pl
from jax.experimental.pallas import tpu as pltpu

F32 = jnp.float32
BF16 = jnp.bfloat16

D_MODEL = 1024
EPS = 1e-6
NEG_INF = -1e30
LOG2_E = math.log2(math.e)
CHUNK = 128
A_GROUPS = 4
A_WIDTH = D_MODEL
A_GROUP_DIM = A_WIDTH // A_GROUPS
B_WIDTH = D_MODEL
POOL_WINDOWS = (2, 4, 8, 16)
B_GROUP_DIM = B_WIDTH // len(POOL_WINDOWS)
POOL_HALO = max(POOL_WINDOWS) // 2
N_HEADS = 16
N_KV_HEADS = 4
HEAD_DIM = 64
GQA_GROUP = N_HEADS // N_KV_HEADS
C_WIDTH = N_HEADS * HEAD_DIM
KV_WIDTH = N_KV_HEADS * HEAD_DIM
WINDOW = 128
ATTN_BLOCK = 128
BAND = 3 * ATTN_BLOCK
ROPE_THETA = 500000.0
ROT_DIM = HEAD_DIM // 4
LANES = 128
assert WINDOW == ATTN_BLOCK
GROUP_Q = GQA_GROUP * ATTN_BLOCK
HEADS_PER_HALF = GQA_GROUP // 2
HALF_Q = HEADS_PER_HALF * ATTN_BLOCK
SUM_ROWS = 32

QKV_COLS = C_WIDTH + 2 * KV_WIDTH
W1_CHUNK_ROWS = 128
W1_CHUNKS = D_MODEL // W1_CHUNK_ROWS
W0_STAGE_CHUNK = (1024, 512)
W0_STAGE_SLOTS = 6

TILE_L0 = 512
TILE_ATTN = 512
VMEM_LIMIT = 56 * 1024 * 1024


def _rms(x, g):
    return x * lax.rsqrt(jnp.mean(x * x, axis=-1, keepdims=True) + EPS) * g


def _silu(x):
    hx = 0.5 * x
    return hx + hx * jnp.tanh(hx)


def _dot(a, b):
    return jnp.dot(a, b, preferred_element_type=F32)


def _dot_nt(a, b):
    return lax.dot_general(a, b, (((1,), (1,)), ((), ())), preferred_element_type=F32)


def _layer0_kernel(x_ref, xp_ref, xn_ref, n0_ref, win_hbm, gv_ref, ws_ref, bsb_ref, wg_ref,
                   sc_ref, wout_hbm, w1_ref, wo1_ref, n1_ref, w1_hbm, cost_ref, sint_ref,
                   o_ref, wgate_ref, wo1b_ref, h1_ref, qt_ref, ka_ref, vt_ref,
                   hext_ref, v_ref, cat_ref, wqkvt_ref, win_ref, wout_ref, stage_ref, wsem,
                   *, tile, seq):
    j = pl.program_id(1)
    nj = pl.num_programs(1)
    step = pl.program_id(0) * nj + j

    @pl.when(step == 0)
    def _():
        slots, cr, cc = stage_ref.shape
        chunks = [(win_hbm, win_ref, r0, c0, False)
                  for r0 in range(0, win_ref.shape[0], cr) for c0 in range(0, win_ref.shape[1], cc)]
        chunks += [(wout_hbm, wout_ref, r0, c0, False)
                   for r0 in range(0, wout_ref.shape[0], cr)
                   for c0 in range(0, wout_ref.shape[1], cc)]
        chunks += [(w1_hbm, wqkvt_ref, r0, c0, True)
                   for r0 in range(0, wqkvt_ref.shape[1], cr)
                   for c0 in range(0, wqkvt_ref.shape[0], cc)]

        def chunk_copy(k):
            src, _, r0, c0, _ = chunks[k]
            return pltpu.make_async_copy(src.at[r0:r0 + cr, c0:c0 + cc], stage_ref.at[k % slots],
                                         wsem.at[k % slots])

        for k in range(min(slots - 1, len(chunks))):
            chunk_copy(k).start()
        for k, (_, dst, r0, c0, transposed) in enumerate(chunks):
            ahead = k + slots - 1
            if ahead < len(chunks):
                chunk_copy(ahead).start()
            chunk_copy(k).wait()
            if transposed:
                dst[c0:c0 + cc, r0:r0 + cr] = stage_ref[k % slots].T.astype(BF16)
            else:
                dst[r0:r0 + cr, c0:c0 + cc] = stage_ref[k % slots].astype(BF16)

    @pl.when(step < W1_CHUNKS)
    def _():
        wgate_ref[...] = w1_ref[:, QKV_COLS:].astype(BF16)
        wo1b_ref[...] = wo1_ref[...].astype(BF16)

    n0 = n0_ref[...]
    x = x_ref[0]
    h = _rms(x, n0).astype(BF16)
    hext_ref[0:tile, :] = h
    hp = jnp.where(j > 0, _rms(xp_ref[0], n0), 0.0)
    hn = jnp.where(j < nj - 1, _rms(xn_ref[0], n0), 0.0)
    hext_ref[tile:tile + 2 * POOL_HALO, :] = jnp.concatenate([hp, hn], axis=0).astype(BF16)

    def col(k):
        return k * D_MODEL

    a_v = jax.nn.gelu(_dot(h, win_ref[:, col(1):col(2)]))
    v_ref[...] = _rms(a_v, gv_ref[...]).astype(BF16)
    for g in range(A_GROUPS):
        c0 = g * A_GROUP_DIM
        c1 = c0 + A_GROUP_DIM
        u = jax.nn.gelu(_dot(h, win_ref[:, col(0) + c0:col(0) + c1]))
        gate = _silu(_dot(h, win_ref[:, col(2) + c0:col(2) + c1]))
        ws_g = ws_ref[g].astype(BF16)
        for c in range(tile // CHUNK):
            r0 = c * CHUNK
            r1 = r0 + CHUNK
            mixed = _dot(ws_g, v_ref[r0:r1, c0:c1]) + bsb_ref[:, c0:c1]
            cat_ref[r0:r1, c0:c1] = (u[r0:r1] * mixed * gate[r0:r1]).astype(BF16)

    bxe = _dot(hext_ref[...], win_ref[:, col(3):col(4)])
    n_ext = tile + 2 * POOL_HALO
    edge_row = lax.broadcasted_iota(jnp.int32, (POOL_HALO, 1), 0)
    for gi, w in enumerate(POOL_WINDOWS):
        c0 = gi * B_GROUP_DIM
        c1 = c0 + B_GROUP_DIM
        half = w // 2
        e = jnp.concatenate([bxe[tile:tile + POOL_HALO, c0:c1], bxe[0:tile, c0:c1],
                             bxe[tile + POOL_HALO:, c0:c1]], axis=0)
        a = e
        span = 1
        while span < half:
            a = a + pltpu.roll(a, n_ext - span, 0)
            span *= 2
        win_sum = (a + pltpu.roll(a, half, 0))[POOL_HALO:POOL_HALO + tile]

        def edge_mean(slab, row0, half=half):
            t = j * tile + row0 + edge_row
            cnt = (jnp.minimum(t + half, seq) - jnp.maximum(t - half, 0)).astype(F32)
            return slab / cnt

        mean = jnp.concatenate(
            [edge_mean(win_sum[0:POOL_HALO], 0),
             win_sum[POOL_HALO:tile - POOL_HALO] * (1.0 / w),
             edge_mean(win_sum[tile - POOL_HALO:], tile - POOL_HALO)], axis=0)
        p = (mean - bxe[0:tile, c0:c1]).astype(BF16)
        gate = _silu(_dot(h, win_ref[:, col(4) + c0:col(4) + c1]))
        yb = _dot(p, wg_ref[gi].astype(BF16)) * sc_ref[:, c0:c1] * gate
        cat_ref[:, A_WIDTH + c0:A_WIDTH + c1] = yb.astype(BF16)

    out = x_ref[0] + _dot(cat_ref[...], wout_ref[...])
    o_ref[0] = out
    h1 = _rms(out, n1_ref[...]).astype(BF16)
    h1_ref[0] = h1
    _qkv_project(h1, wqkvt_ref, cost_ref[...], sint_ref[...], qt_ref, ka_ref, vt_ref)


def _layer0(x, norm_0, w_in_0, a_v_norm_0, a_spatial_w_0, a_spatial_b_0, b_group_w_0, b_scale_0,
            w_out_0, norm_1, w_in_1, w_out_1):
    bn, seq, d = x.shape
    tile = TILE_L0
    nt = seq // tile
    assert bn * nt >= W1_CHUNKS

    def w1_chunk(b, j):
        return jnp.minimum(b * nt + j, W1_CHUNKS - 1)
    halo_blocks = seq // POOL_HALO
    ws = a_spatial_w_0
    wg = b_group_w_0
    cos_fm, sin_fm = _rope_tables(seq)
    bsb = jnp.repeat(a_spatial_b_0.T, A_GROUP_DIM, axis=1)

    def const(shape):
        return pl.BlockSpec(shape, lambda b, j: (0,) * len(shape), pipeline_mode=pl.Buffered(1))

    per_halo = tile // POOL_HALO
    kern = functools.partial(_layer0_kernel, tile=tile, seq=seq)
    return pl.pallas_call(
        kern,
        grid=(bn, nt),
        in_specs=[
            pl.BlockSpec((1, tile, d), lambda b, j: (b, j, 0)),
            pl.BlockSpec((1, POOL_HALO, d), lambda b, j: (b, jnp.maximum(j * per_halo - 1, 0), 0)),
            pl.BlockSpec((1, POOL_HALO, d),
                         lambda b, j: (b, jnp.minimum((j + 1) * per_halo, halo_blocks - 1), 0)),
            const((1, d)),
            pl.BlockSpec(memory_space=pl.ANY),
            const((1, A_WIDTH)),
            const(ws.shape),
            const(bsb.shape),
            const(wg.shape),
            const((1, B_WIDTH)),
            pl.BlockSpec(memory_space=pl.ANY),
            pl.BlockSpec((W1_CHUNK_ROWS, w_in_1.shape[1]), lambda b, j: (w1_chunk(b, j), 0)),
            pl.BlockSpec((W1_CHUNK_ROWS, d), lambda b, j: (w1_chunk(b, j), 0)),
            const((1, d)),
            pl.BlockSpec(memory_space=pl.ANY),
            pl.BlockSpec((ROT_DIM // 2, tile), lambda b, j: (0, j)),
            pl.BlockSpec((ROT_DIM // 2, tile), lambda b, j: (0, j)),
        ],
        out_specs=[
            pl.BlockSpec((1, tile, d), lambda b, j: (b, j, 0)),
            pl.BlockSpec((W1_CHUNK_ROWS, C_WIDTH), lambda b, j: (w1_chunk(b, j), 0)),
            pl.BlockSpec((W1_CHUNK_ROWS, d), lambda b, j: (w1_chunk(b, j), 0)),
            pl.BlockSpec((1, tile, d), lambda b, j: (b, j, 0)),
            pl.BlockSpec((1, 1, C_WIDTH, tile), lambda b, j: (b, j, 0, 0)),
            pl.BlockSpec((1, tile, KV_WIDTH), lambda b, j: (b, j, 0)),
            pl.BlockSpec((1, 1, KV_WIDTH, tile), lambda b, j: (b, j, 0, 0)),
        ],
        out_shape=[
            jax.ShapeDtypeStruct(x.shape, F32),
            jax.ShapeDtypeStruct((d, C_WIDTH), BF16),
            jax.ShapeDtypeStruct((C_WIDTH, d), BF16),
            jax.ShapeDtypeStruct((bn, seq, d), BF16),
            jax.ShapeDtypeStruct((bn, nt, C_WIDTH, tile), BF16),
            jax.ShapeDtypeStruct((bn, seq, KV_WIDTH), BF16),
            jax.ShapeDtypeStruct((bn, nt, KV_WIDTH, tile), BF16),
        ],
        scratch_shapes=[
            pltpu.VMEM((tile + 2 * POOL_HALO, d), BF16),
            pltpu.VMEM((tile, A_WIDTH), BF16),
            pltpu.VMEM((tile, A_WIDTH + B_WIDTH), BF16),
            pltpu.VMEM((QKV_COLS, d), BF16),
            pltpu.VMEM(w_in_0.shape, BF16),
            pltpu.VMEM(w_out_0.shape, BF16),
            pltpu.VMEM((W0_STAGE_SLOTS,) + W0_STAGE_CHUNK, F32),
            pltpu.SemaphoreType.DMA((W0_STAGE_SLOTS,)),
        ],
        compiler_params=pltpu.CompilerParams(
            dimension_semantics=("arbitrary", "arbitrary"), vmem_limit_bytes=VMEM_LIMIT),
        name="layer0_mixer",
    )(x, x, x, norm_0.reshape(1, d), w_in_0, a_v_norm_0.reshape(1, A_WIDTH), ws, bsb, wg,
      b_scale_0.reshape(1, B_WIDTH), w_out_0, w_in_1, w_out_1, norm_1.reshape(1, d), w_in_1,
      cos_fm, sin_fm)


def _qkv_project(h, w_ref, ct, st, qt_ref, ka_ref, vt_ref):
    half = ROT_DIM // 2
    n = h.shape[0]
    scale = HEAD_DIM ** -0.5 * LOG2_E

    qkv = _dot_nt(w_ref[...], h)

    def rope_rows(r0):
        x1 = qkv[r0:r0 + half]
        x2 = qkv[r0 + half:r0 + ROT_DIM]
        return jnp.concatenate([x1 * ct - x2 * st, x2 * ct + x1 * st], axis=0)

    for hd in range(N_HEADS):
        r0 = hd * HEAD_DIM
        qt_ref[0, 0, r0:r0 + ROT_DIM, :] = (rope_rows(r0) * scale).astype(BF16)
        qt_ref[0, 0, r0 + ROT_DIM:r0 + HEAD_DIM, :] = (
            qkv[r0 + ROT_DIM:r0 + HEAD_DIM] * scale).astype(BF16)
    vt_ref[0, 0] = qkv[C_WIDTH + KV_WIDTH:].astype(BF16)

    for kp in range(N_KV_HEADS // 2):
        rows = []
        for kh in (2 * kp, 2 * kp + 1):
            r0 = C_WIDTH + kh * HEAD_DIM
            rows += [rope_rows(r0), qkv[r0 + ROT_DIM:r0 + HEAD_DIM]]
        kt = jnp.concatenate(rows, axis=0)
        ka_ref[0, :, kp * LANES:(kp + 1) * LANES] = kt.T.astype(BF16)


def _rope_tables(seq):
    inv = ROPE_THETA ** (-np.arange(0, ROT_DIM, 2, dtype=np.float64) / ROT_DIM)
    ang = np.arange(seq, dtype=np.float64)[None, :] * inv[:, None]
    return jnp.asarray(np.cos(ang), F32), jnp.asarray(np.sin(ang), F32)


def _attn_kernel(sink_ref, x_ref, h_ref, qt_ref, kap_ref, kat_ref, kan_ref, vtp_ref, vtt_ref,
                 vtn_ref, wgate_ref, wout_ref, fn_ref, o_ref,
                 s_ref, p_ref, sg_ref, y_ref, *, tile, seq):
    j = pl.program_id(1)
    nblk = tile // ATTN_BLOCK
    last_blk = seq // ATTN_BLOCK - 1

    def k_band(i, kh):
        cols = slice(kh // 2 * LANES, (kh // 2 + 1) * LANES)
        lo = max(i - 1, 0) * ATTN_BLOCK
        hi = min(i + 2, nblk) * ATTN_BLOCK
        parts = [kat_ref[0, lo:hi, cols]]
        if i == 0:
            parts.insert(0, kap_ref[0, :, cols])
        if i == nblk - 1:
            parts.append(kan_ref[0, :, cols])
        return jnp.concatenate(parts, axis=0)

    def vt_band(i, kh):
        rows = slice(kh * HEAD_DIM, (kh + 1) * HEAD_DIM)
        lo = max(i - 1, 0) * ATTN_BLOCK
        hi = min(i + 2, nblk) * ATTN_BLOCK
        parts = [vtt_ref[0, 0, rows, lo:hi]]
        if i == 0:
            parts.insert(0, vtp_ref[0, 0, rows, :])
        if i == nblk - 1:
            parts.append(vtn_ref[0, 0, rows, :])
        return parts
    diff = (lax.broadcasted_iota(jnp.int32, (ATTN_BLOCK, ATTN_BLOCK), 0)
            - lax.broadcasted_iota(jnp.int32, (ATTN_BLOCK, ATTN_BLOCK), 1))
    never = 2 * ATTN_BLOCK
    ones_rows = jnp.ones((SUM_ROWS, BAND), BF16)

    def scores(u, half):
        kh, i = divmod(u, nblk)
        slot = u % 2
        c0 = i * ATTN_BLOCK
        h0 = kh * GQA_GROUP + half * HEADS_PER_HALF
        qg = jnp.concatenate(
            [qt_ref[0, 0, (h0 + g) * HEAD_DIM:(h0 + g + 1) * HEAD_DIM, c0:c0 + ATTN_BLOCK]
             for g in range(HEADS_PER_HALF)], axis=1)
        zq = jnp.zeros((LANES - HEAD_DIM, HALF_Q), BF16)
        rq = jnp.concatenate([qg, zq] if kh % 2 == 0 else [zq, qg], axis=0)
        ka = k_band(i, kh)
        s_ref[slot, :, half * HALF_Q:(half + 1) * HALF_Q] = _dot(ka, rq)

    def softmax_head(u, g):
        kh, i = divmod(u, nblk)
        slot = u % 2
        gblk = j * nblk + i
        ok_prev = diff >= jnp.where(gblk > 0, 0, never)
        ok_next = diff <= jnp.where(gblk < last_blk, 0, -never)
        cs = slice(g * ATTN_BLOCK, (g + 1) * ATTN_BLOCK)
        sink = sink_ref[kh * GQA_GROUP + g] * LOG2_E
        s0 = jnp.where(ok_prev, s_ref[slot, 0:ATTN_BLOCK, cs], NEG_INF)
        s1 = s_ref[slot, ATTN_BLOCK:2 * ATTN_BLOCK, cs]
        s2 = jnp.where(ok_next, s_ref[slot, 2 * ATTN_BLOCK:, cs], NEG_INF)
        m = jnp.max(jnp.maximum(jnp.maximum(s0, s1), s2), axis=0, keepdims=True)
        m = jnp.maximum(m, sink)
        p0 = jnp.exp2(s0 - m)
        p1 = jnp.exp2(s1 - m)
        p2 = jnp.exp2(s2 - m)
        p_ref[slot, 0:ATTN_BLOCK, cs] = p0.astype(BF16)
        p_ref[slot, ATTN_BLOCK:2 * ATTN_BLOCK, cs] = p1.astype(BF16)
        p_ref[slot, 2 * ATTN_BLOCK:, cs] = p2.astype(BF16)
        return jnp.exp2(sink - m)

    def weighted(u, half, sink_share):
        kh, i = divmod(u, nblk)
        slot = u % 2
        c0 = i * ATTN_BLOCK
        vt = jnp.concatenate([jnp.concatenate(vt_band(i, kh), axis=1), ones_rows],
                             axis=0)
        ot = _dot(vt, p_ref[slot, :, half * HALF_Q:(half + 1) * HALF_Q])
        parts = []
        for g in range(HEADS_PER_HALF):
            cq = slice(g * ATTN_BLOCK, (g + 1) * ATTN_BLOCK)
            l = ot[HEAD_DIM:HEAD_DIM + 1, cq] + sink_share[g]
            parts.append(ot[0:HEAD_DIM, cq] * (1.0 / l))
        yt = jnp.concatenate(parts, axis=0)
        f0 = (kh * GQA_GROUP + half * HEADS_PER_HALF) * HEAD_DIM
        cs = slice(f0, f0 + HEADS_PER_HALF * HEAD_DIM)
        y = yt.T * sg_ref[c0:c0 + ATTN_BLOCK, cs]
        y_ref[c0:c0 + ATTN_BLOCK, cs] = y.astype(BF16)

    def gate_chunk(kh):
        cs = slice(kh * GQA_GROUP * HEAD_DIM, (kh + 1) * GQA_GROUP * HEAD_DIM)
        sg_ref[:, cs] = _silu(_dot(h_ref[0], wgate_ref[:, cs]))

    def finish_rows(r0, n):
        rows = slice(r0, r0 + n)
        out = x_ref[0, rows, :] + _dot(y_ref[rows, :], wout_ref[...])
        o_ref[0, rows, :] = _rms(out, fn_ref[...])

    units = nblk * N_KV_HEADS
    inv = {}
    for step in range(units + 2):
        sc = step if step < units else None
        sm = step - 1 if 1 <= step <= units else None
        wt = step - 2 if step >= 2 else None
        if step < units and step % nblk == 0:
            gate_chunk(step // nblk)
        for half in range(2):
            if sc is not None:
                scores(sc, half)
            if sm is not None:
                inv[sm, 2 * half] = softmax_head(sm, 2 * half)
            if wt is not None:
                weighted(wt, half, [inv.pop((wt, half * HEADS_PER_HALF + g))
                                    for g in range(HEADS_PER_HALF)])
            if sm is not None:
                inv[sm, 2 * half + 1] = softmax_head(sm, 2 * half + 1)
    finish_rows(0, tile)


def _attention(x, h, qt, ka, vt, w_gate, sink_1, w_out_1, final_norm):
    bn, seq, d = x.shape
    tile = TILE_ATTN
    assert tile == TILE_L0
    nt = seq // tile
    per = tile // ATTN_BLOCK
    nb = seq // ATTN_BLOCK

    def const(shape):
        return pl.BlockSpec(shape, lambda b, j: (0,) * len(shape), pipeline_mode=pl.Buffered(1))

    def prev_blk(j):
        return jnp.maximum(j * per - 1, 0)

    def next_blk(j):
        return jnp.minimum((j + 1) * per, nb - 1)

    kern = functools.partial(_attn_kernel, tile=tile, seq=seq)
    return pl.pallas_call(
        kern,
        grid=(bn, nt),
        in_specs=[
            pl.BlockSpec(memory_space=pltpu.SMEM),
            pl.BlockSpec((1, tile, d), lambda b, j: (b, j, 0)),
            pl.BlockSpec((1, tile, d), lambda b, j: (b, j, 0)),
            pl.BlockSpec((1, 1, C_WIDTH, tile), lambda b, j: (b, j, 0, 0)),
            pl.BlockSpec((1, ATTN_BLOCK, KV_WIDTH), lambda b, j: (b, prev_blk(j), 0)),
            pl.BlockSpec((1, tile, KV_WIDTH), lambda b, j: (b, j, 0)),
            pl.BlockSpec((1, ATTN_BLOCK, KV_WIDTH), lambda b, j: (b, next_blk(j), 0)),
            pl.BlockSpec((1, 1, KV_WIDTH, ATTN_BLOCK),
                         lambda b, j: (b, jnp.maximum(j - 1, 0), 0, per - 1)),
            pl.BlockSpec((1, 1, KV_WIDTH, tile), lambda b, j: (b, j, 0, 0)),
            pl.BlockSpec((1, 1, KV_WIDTH, ATTN_BLOCK),
                         lambda b, j: (b, jnp.minimum(j + 1, nt - 1), 0, 0)),
            const(w_gate.shape),
            const(w_out_1.shape),
            const((1, d)),
        ],
        out_specs=pl.BlockSpec((1, tile, d), lambda b, j: (b, j, 0)),
        out_shape=jax.ShapeDtypeStruct(x.shape, F32),
        scratch_shapes=[
            pltpu.VMEM((2, BAND, GROUP_Q), F32),
            pltpu.VMEM((2, BAND, GROUP_Q), BF16),
            pltpu.VMEM((tile, C_WIDTH), F32),
            pltpu.VMEM((tile, C_WIDTH), BF16),
        ],
        compiler_params=pltpu.CompilerParams(
            dimension_semantics=("arbitrary", "arbitrary"), vmem_limit_bytes=VMEM_LIMIT),
        name="layer1_attention",
    )(sink_1, x, h, qt, ka, ka, ka, vt, vt, vt, w_gate, w_out_1, final_norm.reshape(1, d))


def kernel(x, norm_0, w_in_0, a_v_norm_0, a_spatial_w_0, a_spatial_b_0, b_group_w_0, b_scale_0,
           w_out_0, norm_1, w_in_1, sink_1, w_out_1, final_norm):
    x1, w_gate, w_out_1b, h1, qt, ka, vt = _layer0(
        x, norm_0, w_in_0, a_v_norm_0, a_spatial_w_0, a_spatial_b_0, b_group_w_0, b_scale_0,
        w_out_0, norm_1, w_in_1, w_out_1)
    return _attention(x1, h1, qt, ka, vt, w_gate, sink_1, w_out_1b, final_norm)
```

```python
import functools
import math

import jax
import jax.numpy as jnp
import numpy as np
from jax import lax
from jax.experimental import pallas as pl
from jax.experimental.pallas import tpu as pltpu

F32 = jnp.float32
BF16 = jnp.bfloat16

D_MODEL = 1024
EPS = 1e-6
NEG_INF = -1e30
LOG2_E = math.log2(math.e)
CHUNK = 128
A_GROUPS = 4
A_WIDTH = D_MODEL
A_GROUP_DIM = A_WIDTH // A_GROUPS
B_WIDTH = D_MODEL
POOL_WINDOWS = (2, 4, 8, 16)
B_GROUP_DIM = B_WIDTH // len(POOL_WINDOWS)
POOL_HALO = max(POOL_WINDOWS) // 2
N_HEADS = 16
N_KV_HEADS = 4
HEAD_DIM = 64
GQA_GROUP = N_HEADS // N_KV_HEADS
C_WIDTH = N_HEADS * HEAD_DIM
KV_WIDTH = N_KV_HEADS * HEAD_DIM
WINDOW = 128
ATTN_BLOCK = 128
BAND = 3 * ATTN_BLOCK
ROPE_THETA = 500000.0
ROT_DIM = HEAD_DIM // 4
LANES = 128
assert WINDOW == ATTN_BLOCK
GROUP_Q = GQA_GROUP * ATTN_BLOCK
HEADS_PER_HALF = GQA_GROUP // 2
HALF_Q = HEADS_PER_HALF * ATTN_BLOCK
SUM_ROWS = 32

QKV_COLS = C_WIDTH + 2 * KV_WIDTH
W1_CHUNK_ROWS = 128
W1_CHUNKS = D_MODEL // W1_CHUNK_ROWS
W0_STAGE_CHUNK = (1024, 512)
W0_STAGE_SLOTS = 6

TILE_L0 = 512
TILE_ATTN = 512
VMEM_LIMIT = 56 * 1024 * 1024


def _rms(x, g):
    return x * lax.rsqrt(jnp.mean(x * x, axis=-1, keepdims=True) + EPS) * g


def _silu(x):
    hx = 0.5 * x
    return hx + hx * jnp.tanh(hx)


def _dot(a, b):
    return jnp.dot(a, b, preferred_element_type=F32)


def _dot_nt(a, b):
    return lax.dot_general(a, b, (((1,), (1,)), ((), ())), preferred_element_type=F32)


def _layer0_kernel(x_ref, xp_ref, xn_ref, n0_ref, win_hbm, gv_ref, ws_ref, bsb_ref, wg_ref,
                   sc_ref, wout_hbm, w1_ref, wo1_ref, n1_ref, w1_hbm, cost_ref, sint_ref,
                   o_ref, wgate_ref, wo1b_ref, h1_ref, qt_ref, ka_ref, vt_ref,
                   hext_ref, v_ref, cat_ref, wqkvt_ref, win_ref, wout_ref, stage_ref, wsem,
                   *, tile, seq):
    j = pl.program_id(1)
    nj = pl.num_programs(1)
    step = pl.program_id(0) * nj + j

    @pl.when(step == 0)
    def _():
        slots, cr, cc = stage_ref.shape
        chunks = [(win_hbm, win_ref, r0, c0, False)
                  for r0 in range(0, win_ref.shape[0], cr) for c0 in range(0, win_ref.shape[1], cc)]
        chunks += [(wout_hbm, wout_ref, r0, c0, False)
                   for r0 in range(0, wout_ref.shape[0], cr)
                   for c0 in range(0, wout_ref.shape[1], cc)]
        chunks += [(w1_hbm, wqkvt_ref, r0, c0, True)
                   for r0 in range(0, wqkvt_ref.shape[1], cr)
                   for c0 in range(0, wqkvt_ref.shape[0], cc)]

        def chunk_copy(k):
            src, _, r0, c0, _ = chunks[k]
            return pltpu.make_async_copy(src.at[r0:r0 + cr, c0:c0 + cc], stage_ref.at[k % slots],
                                         wsem.at[k % slots])

        for k in range(min(slots - 1, len(chunks))):
            chunk_copy(k).start()
        for k, (_, dst, r0, c0, transposed) in enumerate(chunks):
            ahead = k + slots - 1
            if ahead < len(chunks):
                chunk_copy(ahead).start()
            chunk_copy(k).wait()
            if transposed:
                dst[c0:c0 + cc, r0:r0 + cr] = stage_ref[k % slots].T.astype(BF16)
            else:
                dst[r0:r0 + cr, c0:c0 + cc] = stage_ref[k % slots].astype(BF16)

    @pl.when(step < W1_CHUNKS)
    def _():
        wgate_ref[...] = w1_ref[:, QKV_COLS:].astype(BF16)
        wo1b_ref[...] = wo1_ref[...].astype(BF16)

    n0 = n0_ref[...]
    x = x_ref[0]
    h = _rms(x, n0).astype(BF16)
    hext_ref[0:tile, :] = h
    hp = jnp.where(j > 0, _rms(xp_ref[0], n0), 0.0)
    hn = jnp.where(j < nj - 1, _rms(xn_ref[0], n0), 0.0)
    hext_ref[tile:tile + 2 * POOL_HALO, :] = jnp.concatenate([hp, hn], axis=0).astype(BF16)

    def col(k):
        return k * D_MODEL

    a_v = jax.nn.gelu(_dot(h, win_ref[:, col(1):col(2)]))
    v_ref[...] = _rms(a_v, gv_ref[...]).astype(BF16)
    for g in range(A_GROUPS):
        c0 = g * A_GROUP_DIM
        c1 = c0 + A_GROUP_DIM
        u = jax.nn.gelu(_dot(h, win_ref[:, col(0) + c0:col(0) + c1]))
        gate = _silu(_dot(h, win_ref[:, col(2) + c0:col(2) + c1]))
        ws_g = ws_ref[g].astype(BF16)
        for c in range(tile // CHUNK):
            r0 = c * CHUNK
            r1 = r0 + CHUNK
            mixed = _dot(ws_g, v_ref[r0:r1, c0:c1]) + bsb_ref[:, c0:c1]
            cat_ref[r0:r1, c0:c1] = (u[r0:r1] * mixed * gate[r0:r1]).astype(BF16)

    bxe = _dot(hext_ref[...], win_ref[:, col(3):col(4)])
    n_ext = tile + 2 * POOL_HALO
    edge_row = lax.broadcasted_iota(jnp.int32, (POOL_HALO, 1), 0)
    for gi, w in enumerate(POOL_WINDOWS):
        c0 = gi * B_GROUP_DIM
        c1 = c0 + B_GROUP_DIM
        half = w // 2
        e = jnp.concatenate([bxe[tile:tile + POOL_HALO, c0:c1], bxe[0:tile, c0:c1],
                             bxe[tile + POOL_HALO:, c0:c1]], axis=0)
        a = e
        span = 1
        while span < half:
            a = a + pltpu.roll(a, n_ext - span, 0)
            span *= 2
        win_sum = (a + pltpu.roll(a, half, 0))[POOL_HALO:POOL_HALO + tile]

        def edge_mean(slab, row0, half=half):
            t = j * tile + row0 + edge_row
            cnt = (jnp.minimum(t + half, seq) - jnp.maximum(t - half, 0)).astype(F32)
            return slab / cnt

        mean = jnp.concatenate(
            [edge_mean(win_sum[0:POOL_HALO], 0),
             win_sum[POOL_HALO:tile - POOL_HALO] * (1.0 / w),
             edge_mean(win_sum[tile - POOL_HALO:], tile - POOL_HALO)], axis=0)
        p = (mean - bxe[0:tile, c0:c1]).astype(BF16)
        gate = _silu(_dot(h, win_ref[:, col(4) + c0:col(4) + c1]))
        yb = _dot(p, wg_ref[gi].astype(BF16)) * sc_ref[:, c0:c1] * gate
        cat_ref[:, A_WIDTH + c0:A_WIDTH + c1] = yb.astype(BF16)

    out = x_ref[0] + _dot(cat_ref[...], wout_ref[...])
    o_ref[0] = out
    h1 = _rms(out, n1_ref[...]).astype(BF16)
    h1_ref[0] = h1
    _qkv_project(h1, wqkvt_ref, cost_ref[...], sint_ref[...], qt_ref, ka_ref, vt_ref)


def _layer0(x, norm_0, w_in_0, a_v_norm_0, a_spatial_w_0, a_spatial_b_0, b_group_w_0, b_scale_0,
            w_out_0, norm_1, w_in_1, w_out_1):
    bn, seq, d = x.shape
    tile = TILE_L0
    nt = seq // tile
    assert bn * nt >= W1_CHUNKS

    def w1_chunk(b, j):
        return jnp.minimum(b * nt + j, W1_CHUNKS - 1)
    halo_blocks = seq // POOL_HALO
    ws = a_spatial_w_0
    wg = b_group_w_0
    cos_fm, sin_fm = _rope_tables(seq)
    bsb = jnp.repeat(a_spatial_b_0.T, A_GROUP_DIM, axis=1)

    def const(shape):
        return pl.BlockSpec(shape, lambda b, j: (0,) * len(shape), pipeline_mode=pl.Buffered(1))

    per_halo = tile // POOL_HALO
    kern = functools.partial(_layer0_kernel, tile=tile, seq=seq)
    return pl.pallas_call(
        kern,
        grid=(bn, nt),
        in_specs=[
            pl.BlockSpec((1, tile, d), lambda b, j: (b, j, 0)),
            pl.BlockSpec((1, POOL_HALO, d), lambda b, j: (b, jnp.maximum(j * per_halo - 1, 0), 0)),
            pl.BlockSpec((1, POOL_HALO, d),
                         lambda b, j: (b, jnp.minimum((j + 1) * per_halo, halo_blocks - 1), 0)),
            const((1, d)),
            pl.BlockSpec(memory_space=pl.ANY),
            const((1, A_WIDTH)),
            const(ws.shape),
            const(bsb.shape),
            const(wg.shape),
            const((1, B_WIDTH)),
            pl.BlockSpec(memory_space=pl.ANY),
            pl.BlockSpec((W1_CHUNK_ROWS, w_in_1.shape[1]), lambda b, j: (w1_chunk(b, j), 0)),
            pl.BlockSpec((W1_CHUNK_ROWS, d), lambda b, j: (w1_chunk(b, j), 0)),
            const((1, d)),
            pl.BlockSpec(memory_space=pl.ANY),
            pl.BlockSpec((ROT_DIM // 2, tile), lambda b, j: (0, j)),
            pl.BlockSpec((ROT_DIM // 2, tile), lambda b, j: (0, j)),
        ],
        out_specs=[
            pl.BlockSpec((1, tile, d), lambda b, j: (b, j, 0)),
            pl.BlockSpec((W1_CHUNK_ROWS, C_WIDTH), lambda b, j: (w1_chunk(b, j), 0)),
            pl.BlockSpec((W1_CHUNK_ROWS, d), lambda b, j: (w1_chunk(b, j), 0)),
            pl.BlockSpec((1, tile, d), lambda b, j: (b, j, 0)),
            pl.BlockSpec((1, 1, C_WIDTH, tile), lambda b, j: (b, j, 0, 0)),
            pl.BlockSpec((1, tile, KV_WIDTH), lambda b, j: (b, j, 0)),
            pl.BlockSpec((1, 1, KV_WIDTH, tile), lambda b, j: (b, j, 0, 0)),
        ],
        out_shape=[
            jax.ShapeDtypeStruct(x.shape, F32),
            jax.ShapeDtypeStruct((d, C_WIDTH), BF16),
            jax.ShapeDtypeStruct((C_WIDTH, d), BF16),
            jax.ShapeDtypeStruct((bn, seq, d), BF16),
            jax.ShapeDtypeStruct((bn, nt, C_WIDTH, tile), BF16),
            jax.ShapeDtypeStruct((bn, seq, KV_WIDTH), BF16),
            jax.ShapeDtypeStruct((bn, nt, KV_WIDTH, tile), BF16),
        ],
        scratch_shapes=[
            pltpu.VMEM((tile + 2 * POOL_HALO, d), BF16),
            pltpu.VMEM((tile, A_WIDTH), BF16),
            pltpu.VMEM((tile, A_WIDTH + B_WIDTH), BF16),
            pltpu.VMEM((QKV_COLS, d), BF16),
            pltpu.VMEM(w_in_0.shape, BF16),
            pltpu.VMEM(w_out_0.shape, BF16),
            pltpu.VMEM((W0_STAGE_SLOTS,) + W0_STAGE_CHUNK, F32),
            pltpu.SemaphoreType.DMA((W0_STAGE_SLOTS,)),
        ],
        compiler_params=pltpu.CompilerParams(
            dimension_semantics=("arbitrary", "arbitrary"), vmem_limit_bytes=VMEM_LIMIT),
        name="layer0_mixer",
    )(x, x, x, norm_0.reshape(1, d), w_in_0, a_v_norm_0.reshape(1, A_WIDTH), ws, bsb, wg,
      b_scale_0.reshape(1, B_WIDTH), w_out_0, w_in_1, w_out_1, norm_1.reshape(1, d), w_in_1,
      cos_fm, sin_fm)


def _qkv_project(h, w_ref, ct, st, qt_ref, ka_ref, vt_ref):
    half = ROT_DIM // 2
    n = h.shape[0]
    scale = HEAD_DIM ** -0.5 * LOG2_E

    qkv = _dot_nt(w_ref[...], h)

    def rope_rows(r0):
        x1 = qkv[r0:r0 + half]
        x2 = qkv[r0 + half:r0 + ROT_DIM]
        return jnp.concatenate([x1 * ct - x2 * st, x2 * ct + x1 * st], axis=0)

    for hd in range(N_HEADS):
        r0 = hd * HEAD_DIM
        qt_ref[0, 0, r0:r0 + ROT_DIM, :] = (rope_rows(r0) * scale).astype(BF16)
        qt_ref[0, 0, r0 + ROT_DIM:r0 + HEAD_DIM, :] = (
            qkv[r0 + ROT_DIM:r0 + HEAD_DIM] * scale).astype(BF16)
    vt_ref[0, 0] = qkv[C_WIDTH + KV_WIDTH:].astype(BF16)

    for kp in range(N_KV_HEADS // 2):
        rows = []
        for kh in (2 * kp, 2 * kp + 1):
            r0 = C_WIDTH + kh * HEAD_DIM
            rows += [rope_rows(r0), qkv[r0 + ROT_DIM:r0 + HEAD_DIM]]
        kt = jnp.concatenate(rows, axis=0)
        ka_ref[0, :, kp * LANES:(kp + 1) * LANES] = kt.T.astype(BF16)


def _rope_tables(seq):
    inv = ROPE_THETA ** (-np.arange(0, ROT_DIM, 2, dtype=np.float64) / ROT_DIM)
    ang = np.arange(seq, dtype=np.float64)[None, :] * inv[:, None]
    return jnp.asarray(np.cos(ang), F32), jnp.asarray(np.sin(ang), F32)


def _attn_kernel(sink_ref, x_ref, h_ref, qt_ref, kap_ref, kat_ref, kan_ref, vtp_ref, vtt_ref,
                 vtn_ref, wgate_ref, wout_ref, fn_ref, o_ref,
                 s_ref, p_ref, sg_ref, y_ref, *, tile, seq):
    j = pl.program_id(1)
    nblk = tile // ATTN_BLOCK
    last_blk = seq // ATTN_BLOCK - 1

    def k_band(i, kh):
        cols = slice(kh // 2 * LANES, (kh // 2 + 1) * LANES)
        lo = max(i - 1, 0) * ATTN_BLOCK
        hi = min(i + 2, nblk) * ATTN_BLOCK
        parts = [kat_ref[0, lo:hi, cols]]
        if i == 0:
            parts.insert(0, kap_ref[0, :, cols])
        if i == nblk - 1:
            parts.append(kan_ref[0, :, cols])
        return jnp.concatenate(parts, axis=0)

    def vt_band(i, kh):
        rows = slice(kh * HEAD_DIM, (kh + 1) * HEAD_DIM)
        lo = max(i - 1, 0) * ATTN_BLOCK
        hi = min(i + 2, nblk) * ATTN_BLOCK
        parts = [vtt_ref[0, 0, rows, lo:hi]]
        if i == 0:
            parts.insert(0, vtp_ref[0, 0, rows, :])
        if i == nblk - 1:
            parts.append(vtn_ref[0, 0, rows, :])
        return parts
    diff = (lax.broadcasted_iota(jnp.int32, (ATTN_BLOCK, ATTN_BLOCK), 0)
            - lax.broadcasted_iota(jnp.int32, (ATTN_BLOCK, ATTN_BLOCK), 1))
    never = 2 * ATTN_BLOCK
    ones_rows = jnp.ones((SUM_ROWS, BAND), BF16)

    def scores(u, half):
        kh, i = divmod(u, nblk)
        slot = u % 2
        c0 = i * ATTN_BLOCK
        h0 = kh * GQA_GROUP + half * HEADS_PER_HALF
        qg = jnp.concatenate(
            [qt_ref[0, 0, (h0 + g) * HEAD_DIM:(h0 + g + 1) * HEAD_DIM, c0:c0 + ATTN_BLOCK]
             for g in range(HEADS_PER_HALF)], axis=1)
        zq = jnp.zeros((LANES - HEAD_DIM, HALF_Q), BF16)
        rq = jnp.concatenate([qg, zq] if kh % 2 == 0 else [zq, qg], axis=0)
        ka = k_band(i, kh)
        s_ref[slot, :, half * HALF_Q:(half + 1) * HALF_Q] = _dot(ka, rq)

    def softmax_head(u, g):
        kh, i = divmod(u, nblk)
        slot = u % 2
        gblk = j * nblk + i
        ok_prev = diff >= jnp.where(gblk > 0, 0, never)
        ok_next = diff <= jnp.where(gblk < last_blk, 0, -never)
        cs = slice(g * ATTN_BLOCK, (g + 1) * ATTN_BLOCK)
        sink = sink_ref[kh * GQA_GROUP + g] * LOG2_E
        s0 = jnp.where(ok_prev, s_ref[slot, 0:ATTN_BLOCK, cs], NEG_INF)
        s1 = s_ref[slot, ATTN_BLOCK:2 * ATTN_BLOCK, cs]
        s2 = jnp.where(ok_next, s_ref[slot, 2 * ATTN_BLOCK:, cs], NEG_INF)
        m = jnp.max(jnp.maximum(jnp.maximum(s0, s1), s2), axis=0, keepdims=True)
        m = jnp.maximum(m, sink)
        p0 = jnp.exp2(s0 - m)
        p1 = jnp.exp2(s1 - m)
        p2 = jnp.exp2(s2 - m)
        p_ref[slot, 0:ATTN_BLOCK, cs] = p0.astype(BF16)
        p_ref[slot, ATTN_BLOCK:2 * ATTN_BLOCK, cs] = p1.astype(BF16)
        p_ref[slot, 2 * ATTN_BLOCK:, cs] = p2.astype(BF16)
        return jnp.exp2(sink - m)

    def weighted(u, half, sink_share):
        kh, i = divmod(u, nblk)
        slot = u % 2
        c0 = i * ATTN_BLOCK
        vt = jnp.concatenate([jnp.concatenate(vt_band(i, kh), axis=1), ones_rows],
                             axis=0)
        ot = _dot(vt, p_ref[slot, :, half * HALF_Q:(half + 1) * HALF_Q])
        parts = []
        for g in range(HEADS_PER_HALF):
            cq = slice(g * ATTN_BLOCK, (g + 1) * ATTN_BLOCK)
            l = ot[HEAD_DIM:HEAD_DIM + 1, cq] + sink_share[g]
            parts.append(ot[0:HEAD_DIM, cq] * (1.0 / l))
        yt = jnp.concatenate(parts, axis=0)
        f0 = (kh * GQA_GROUP + half * HEADS_PER_HALF) * HEAD_DIM
        cs = slice(f0, f0 + HEADS_PER_HALF * HEAD_DIM)
        y = yt.T * sg_ref[c0:c0 + ATTN_BLOCK, cs]
        y_ref[c0:c0 + ATTN_BLOCK, cs] = y.astype(BF16)

    def gate_chunk(kh):
        cs = slice(kh * GQA_GROUP * HEAD_DIM, (kh + 1) * GQA_GROUP * HEAD_DIM)
        sg_ref[:, cs] = _silu(_dot(h_ref[0], wgate_ref[:, cs]))

    def project_head_group(kh):
        cs = slice(kh * GQA_GROUP * HEAD_DIM, (kh + 1) * GQA_GROUP * HEAD_DIM)
        part = _dot(y_ref[:, cs], wout_ref[cs, :])
        if kh == 0:
            o_ref[0] = x_ref[0] + part
        elif kh < N_KV_HEADS - 1:
            o_ref[0] = o_ref[0] + part
        else:
            o_ref[0] = _rms(o_ref[0] + part, fn_ref[...])

    units = nblk * N_KV_HEADS
    inv = {}
    for step in range(units + 2):
        sc = step if step < units else None
        sm = step - 1 if 1 <= step <= units else None
        wt = step - 2 if step >= 2 else None
        if step < units and step % nblk == 0:
            gate_chunk(step // nblk)
        for half in range(2):
            if sc is not None:
                scores(sc, half)
            if sm is not None:
                inv[sm, 2 * half] = softmax_head(sm, 2 * half)
            if wt is not None:
                weighted(wt, half, [inv.pop((wt, half * HEADS_PER_HALF + g))
                                    for g in range(HEADS_PER_HALF)])
            if sm is not None:
                inv[sm, 2 * half + 1] = softmax_head(sm, 2 * half + 1)
        if wt is not None and wt % nblk == nblk - 1:
            project_head_group(wt // nblk)


def _attention(x, h, qt, ka, vt, w_gate, sink_1, w_out_1, final_norm):
    bn, seq, d = x.shape
    tile = TILE_ATTN
    assert tile == TILE_L0
    nt = seq // tile
    per = tile // ATTN_BLOCK
    nb = seq // ATTN_BLOCK

    def const(shape):
        return pl.BlockSpec(shape, lambda b, j: (0,) * len(shape), pipeline_mode=pl.Buffered(1))

    def prev_blk(j):
        return jnp.maximum(j * per - 1, 0)

    def next_blk(j):
        return jnp.minimum((j + 1) * per, nb - 1)

    kern = functools.partial(_attn_kernel, tile=tile, seq=seq)
    return pl.pallas_call(
        kern,
        grid=(bn, nt),
        in_specs=[
            pl.BlockSpec(memory_space=pltpu.SMEM),
            pl.BlockSpec((1, tile, d), lambda b, j: (b, j, 0)),
            pl.BlockSpec((1, tile, d), lambda b, j: (b, j, 0)),
            pl.BlockSpec((1, 1, C_WIDTH, tile), lambda b, j: (b, j, 0, 0)),
            pl.BlockSpec((1, ATTN_BLOCK, KV_WIDTH), lambda b, j: (b, prev_blk(j), 0)),
            pl.BlockSpec((1, tile, KV_WIDTH), lambda b, j: (b, j, 0)),
            pl.BlockSpec((1, ATTN_BLOCK, KV_WIDTH), lambda b, j: (b, next_blk(j), 0)),
            pl.BlockSpec((1, 1, KV_WIDTH, ATTN_BLOCK),
                         lambda b, j: (b, jnp.maximum(j - 1, 0), 0, per - 1)),
            pl.BlockSpec((1, 1, KV_WIDTH, tile), lambda b, j: (b, j, 0, 0)),
            pl.BlockSpec((1, 1, KV_WIDTH, ATTN_BLOCK),
                         lambda b, j: (b, jnp.minimum(j + 1, nt - 1), 0, 0)),
            const(w_gate.shape),
            const(w_out_1.shape),
            const((1, d)),
        ],
        out_specs=pl.BlockSpec((1, tile, d), lambda b, j: (b, j, 0)),
        out_shape=jax.ShapeDtypeStruct(x.shape, F32),
        scratch_shapes=[
            pltpu.VMEM((2, BAND, GROUP_Q), F32),
            pltpu.VMEM((2, BAND, GROUP_Q), BF16),
            pltpu.VMEM((tile, C_WIDTH), F32),
            pltpu.VMEM((tile, C_WIDTH), BF16),
        ],
        compiler_params=pltpu.CompilerParams(
            dimension_semantics=("arbitrary", "arbitrary"), vmem_limit_bytes=VMEM_LIMIT),
        name="layer1_attention",
    )(sink_1, x, h, qt, ka, ka, ka, vt, vt, vt, w_gate, w_out_1, final_norm.reshape(1, d))


def kernel(x, norm_0, w_in_0, a_v_norm_0, a_spatial_w_0, a_spatial_b_0, b_group_w_0, b_scale_0,
           w_out_0, norm_1, w_in_1, sink_1, w_out_1, final_norm):
    x1, w_gate, w_out_1b, h1, qt, ka, vt = _layer0(
        x, norm_0, w_in_0, a_v_norm_0, a_spatial_w_0, a_spatial_b_0, b_group_w_0, b_scale_0,
        w_out_0, norm_1, w_in_1, w_out_1)
    return _attention(x1, h1, qt, ka, vt, w_gate, sink_1, w_out_1b, final_norm)
```
